```python
import jax, jax.numpy as jnp
from jax import lax
import numpy as np

D_MODEL = 1024
BATCH = 1
SEQ = 16384
DEPTH = 4

N_MIXERS = 2
N_MLA_LAYERS = (DEPTH + N_MIXERS - 1) // N_MIXERS
N_RWKV_LAYERS = DEPTH // N_MIXERS
NORM_EPS = 1e-6

MLA_HEADS = 8
QK_NOPE_DIM = 128
QK_ROPE_DIM = 64
QK_DIM = QK_NOPE_DIM + QK_ROPE_DIM
V_HEAD_DIM = 128
Q_LORA_RANK = 384
KV_LORA_RANK = 256
MLA_IN_DIM = Q_LORA_RANK + KV_LORA_RANK + QK_ROPE_DIM
ROPE_THETA = 10000.0
Q_BLOCK = 128

RWKV_HEAD = 64
RWKV_HEADS = D_MODEL // RWKV_HEAD
DECAY_LORA = 64
AAA_LORA = 64
MV_LORA = 32
GATE_LORA = 160
GN_EPS = 64e-5

D_FF = 2816
CONV_WIDTH = 3

kernel_name = "hybrid_mla_rwkv7_convglu"


def _rms_norm(x, g):
    xf = x.astype(jnp.float32)
    y = xf * lax.rsqrt(jnp.mean(xf * xf, axis=-1, keepdims=True) + NORM_EPS)
    return (y * g.astype(jnp.float32)).astype(x.dtype)


def _rope(t, positions):
    half = QK_ROPE_DIM // 2
    inv_freq = ROPE_THETA ** (-jnp.arange(half, dtype=jnp.float32) / half)
    ang = positions.astype(jnp.float32)[..., None] * inv_freq
    ang = ang.reshape(ang.shape[:2] + (1,) * (t.ndim - 3) + (half,))
    cos, sin = jnp.cos(ang), jnp.sin(ang)
    tf = t.astype(jnp.float32)
    t1, t2 = tf[..., :half], tf[..., half:]
    return jnp.concatenate([t1 * cos - t2 * sin, t2 * cos + t1 * sin], axis=-1).astype(t.dtype)


def _mla(h, positions, w_in, q_norm, w_uq, kv_norm, w_ukv, w_o):
    B, S, _ = h.shape
    lat = h @ w_in
    q_lat = lat[..., :Q_LORA_RANK]
    c_kv = lat[..., Q_LORA_RANK:Q_LORA_RANK + KV_LORA_RANK]
    k_rope = _rope(lat[..., Q_LORA_RANK + KV_LORA_RANK:], positions)
    q = (_rms_norm(q_lat, q_norm) @ w_uq).reshape(B, S, MLA_HEADS, QK_DIM)
    q_nope = q[..., :QK_NOPE_DIM]
    q_rope = _rope(q[..., QK_NOPE_DIM:], positions)
    kv = (_rms_norm(c_kv, kv_norm) @ w_ukv).reshape(B, S, MLA_HEADS, QK_NOPE_DIM + V_HEAD_DIM)
    k_nope = kv[..., :QK_NOPE_DIM]
    v = kv[..., QK_NOPE_DIM:]
    scale = QK_DIM ** -0.5
    n_blk = S // Q_BLOCK
    key_pos = jnp.arange(S)

    def to_blocks(t):
        return t.reshape((B, n_blk, Q_BLOCK) + t.shape[2:]).swapaxes(0, 1)

    def attend(args):
        qn, qr, blk = args
        s = (jnp.einsum('bqhd,bkhd->bhqk', qn, k_nope)
             + jnp.einsum('bqhr,bkr->bhqk', qr, k_rope)).astype(jnp.float32) * scale
        q_pos = blk * Q_BLOCK + jnp.arange(Q_BLOCK)
        causal = key_pos[None, :] <= q_pos[:, None]
        s = jnp.where(causal, s, jnp.finfo(jnp.float32).min)
        p = jax.nn.softmax(s, axis=-1).astype(v.dtype)
        return jnp.einsum('bhqk,bkhd->bqhd', p, v)

    o = lax.map(attend, (to_blocks(q_nope), to_blocks(q_rope), jnp.arange(n_blk)))
    o = o.swapaxes(0, 1).reshape(B, S, MLA_HEADS * V_HEAD_DIM)
    return o @ w_o


def _token_shift(x):
    return jnp.pad(x, ((0, 0), (1, 0), (0, 0)))[:, :-1]


def _wkv7_scan(r, w, k, v, a, b):
    B, S, H, N = r.shape

    def step(state, inp):
        r_t, w_t, k_t, v_t, a_t, b_t = inp
        sa = jnp.einsum('bhij,bhj->bhi', state, a_t)
        state = (state * w_t[:, :, None, :] + sa[..., None] * b_t[:, :, None, :]
                 + v_t[..., None] * k_t[:, :, None, :])
        return state, jnp.einsum('bhij,bhj->bhi', state, r_t)

    xs = tuple(jnp.moveaxis(t, 1, 0) for t in (r, w, k, v, a, b))
    _, y = lax.scan(step, jnp.zeros((B, H, N, N), jnp.float32), xs)
    return jnp.moveaxis(y, 0, 1)


def _rwkv7(h, v_first, v_lora, mix, w_rkv, w0, w1, w2, a0, a1, a2, g1, g2,
           k_k, k_a, r_k, ln_w, ln_b, w_o):
    B, S, D = h.shape
    H, N = RWKV_HEADS, RWKV_HEAD
    xx = _token_shift(h) - h
    xm = h[None] + xx[None] * mix[:, None, None, :]
    rkv = jnp.einsum('nbsd,nde->nbse', xm[:3], w_rkv)
    r, k, v = rkv[0], rkv[1], rkv[2]
    xw, xa, xg = xm[3], xm[4], xm[5]
    log_w = -jax.nn.softplus(-(w0 + jnp.tanh(xw @ w1) @ w2)) - 0.5
    decay = jnp.exp(-jnp.exp(log_w.astype(jnp.float32)))
    a = jax.nn.sigmoid(a0 + (xa @ a1) @ a2)
    g = jax.nn.sigmoid(xg @ g1) @ g2
    if v_lora is None:
        v_first = v
    else:
        v0, v1, v2 = v_lora
        v = v + (v_first - v) * jax.nn.sigmoid(v0 + (xv_lora_in := xm[2]) @ v1 @ v2)
    kk = (k * k_k).astype(jnp.float32).reshape(B, S, H, N)
    kk = kk / jnp.maximum(jnp.sqrt(jnp.sum(kk * kk, axis=-1, keepdims=True)), 1e-12)
    k = k * (1 + (a - 1) * k_a)
    heads = lambda t: t.astype(jnp.float32).reshape(B, S, H, N)
    rf, kf, vf, af = heads(r), heads(k), heads(v), heads(a)
    y = _wkv7_scan(rf, heads(decay), kf, vf, -kk, kk * af)
    mu = jnp.mean(y, axis=-1, keepdims=True)
    var = jnp.mean(jnp.square(y - mu), axis=-1, keepdims=True)
    yn = ((y - mu) * lax.rsqrt(var + GN_EPS)).reshape(B, S, D) * ln_w.astype(jnp.float32) + ln_b.astype(jnp.float32)
    bonus = (jnp.sum(rf * kf * r_k.astype(jnp.float32), axis=-1, keepdims=True) * vf).reshape(B, S, D)
    out = ((yn + bonus).astype(h.dtype) * g) @ w_o
    return out, v_first


def _conv_glu(h, w_in, conv_w, conv_b, w_out):
    hu = h @ w_in
    gate, up = hu[..., :D_FF], hu[..., D_FF:]
    gate = lax.conv_general_dilated(
        gate, conv_w[:, None, :], window_strides=(1,), padding=[(CONV_WIDTH - 1, 0)],
        dimension_numbers=('NWC', 'WIO', 'NWC'), feature_group_count=D_FF) + conv_b
    return (jax.nn.silu(gate) * up) @ w_out


def setup_inputs(seed: int = 0) -> dict:
    key = jax.random.key(seed)
    ks = iter(jax.random.split(key, 48))
    nrm = lambda shape, scale: jax.random.normal(next(ks), shape, jnp.float32) * scale
    uni = lambda shape, lo, hi: jax.random.uniform(next(ks), shape, jnp.float32, lo, hi)
    Lm, Lr, D = N_MLA_LAYERS, N_RWKV_LAYERS, D_MODEL
    return {
        "x": nrm((BATCH, SEQ, D), 1.0),
        "positions": jnp.broadcast_to(jnp.arange(SEQ, dtype=jnp.int32), (BATCH, SEQ)),
        "norm_mix": 1.0 + nrm((DEPTH, D), 0.05),
        "norm_ffn": 1.0 + nrm((DEPTH, D), 0.05),
        "norm_final": 1.0 + nrm((D,), 0.05),
        "mla_w_in": nrm((Lm, D, MLA_IN_DIM), D ** -0.5),
        "mla_q_norm": 1.0 + nrm((Lm, Q_LORA_RANK), 0.05),
        "mla_w_uq": nrm((Lm, Q_LORA_RANK, MLA_HEADS * QK_DIM), Q_LORA_RANK ** -0.5),
        "mla_kv_norm": 1.0 + nrm((Lm, KV_LORA_RANK), 0.05),
        "mla_w_ukv": nrm((Lm, KV_LORA_RANK, MLA_HEADS * (QK_NOPE_DIM + V_HEAD_DIM)), KV_LORA_RANK ** -0.5),
        "mla_w_o": nrm((Lm, MLA_HEADS * V_HEAD_DIM, D), (MLA_HEADS * V_HEAD_DIM) ** -0.5),
        "rwkv_mix": uni((Lr, 6, D), 0.0, 1.0),
        "rwkv_w_rkv": nrm((Lr, 3, D, D), D ** -0.5),
        "rwkv_w0": uni((Lr, D), -5.0, -0.5),
        "rwkv_w1": nrm((Lr, D, DECAY_LORA), D ** -0.5),
        "rwkv_w2": nrm((Lr, DECAY_LORA, D), 0.1 * DECAY_LORA ** -0.5),
        "rwkv_a0": nrm((Lr, D), 0.1),
        "rwkv_a1": nrm((Lr, D, AAA_LORA), D ** -0.5),
        "rwkv_a2": nrm((Lr, AAA_LORA, D), 0.1 * AAA_LORA ** -0.5),
        "rwkv_v0": 1.0 + nrm((Lr - 1, D), 0.1),
        "rwkv_v1": nrm((Lr - 1, D, MV_LORA), D ** -0.5),
        "rwkv_v2": nrm((Lr - 1, MV_LORA, D), 0.1 * MV_LORA ** -0.5),
        "rwkv_g1": nrm((Lr, D, GATE_LORA), D ** -0.5),
        "rwkv_g2": nrm((Lr, GATE_LORA, D), GATE_LORA ** -0.5),
        "rwkv_k_k": 0.85 + nrm((Lr, D), 0.05),
        "rwkv_k_a": 1.0 + nrm((Lr, D), 0.05),
        "rwkv_r_k": nrm((Lr, RWKV_HEADS, RWKV_HEAD), 0.1),
        "rwkv_ln_w": 1.0 + nrm((Lr, D), 0.05),
        "rwkv_ln_b": nrm((Lr, D), 0.01),
        "rwkv_w_o": nrm((Lr, D, D), D ** -0.5),
        "ffn_w_in": nrm((DEPTH, D, 2 * D_FF), D ** -0.5),
        "ffn_conv_w": nrm((DEPTH, CONV_WIDTH, D_FF), CONV_WIDTH ** -0.5),
        "ffn_conv_b": nrm((DEPTH, D_FF), 0.01),
        "ffn_w_out": nrm((DEPTH, D_FF, D), D_FF ** -0.5),
    }


def reference(x, positions, norm_mix, norm_ffn, norm_final,
              mla_w_in, mla_q_norm, mla_w_uq, mla_kv_norm, mla_w_ukv, mla_w_o,
              rwkv_mix, rwkv_w_rkv, rwkv_w0, rwkv_w1, rwkv_w2, rwkv_a0, rwkv_a1, rwkv_a2,
              rwkv_v0, rwkv_v1, rwkv_v2, rwkv_g1, rwkv_g2, rwkv_k_k, rwkv_k_a, rwkv_r_k,
              rwkv_ln_w, rwkv_ln_b, rwkv_w_o,
              ffn_w_in, ffn_conv_w, ffn_conv_b, ffn_w_out):
    v_first = None
    for i in range(DEPTH):
        h = _rms_norm(x, norm_mix[i])
        j = i // N_MIXERS
        if i % N_MIXERS == 0:
            y = _mla(h, positions, mla_w_in[j], mla_q_norm[j], mla_w_uq[j],
                     mla_kv_norm[j], mla_w_ukv[j], mla_w_o[j])
        else:
            v_lora = None if j == 0 else (rwkv_v0[j - 1], rwkv_v1[j - 1], rwkv_v2[j - 1])
            y, v_first = _rwkv7(h, v_first, v_lora, rwkv_mix[j], rwkv_w_rkv[j],
                                rwkv_w0[j], rwkv_w1[j], rwkv_w2[j],
                                rwkv_a0[j], rwkv_a1[j], rwkv_a2[j],
                                rwkv_g1[j], rwkv_g2[j], rwkv_k_k[j], rwkv_k_a[j], rwkv_r_k[j],
                                rwkv_ln_w[j], rwkv_ln_b[j], rwkv_w_o[j])
        x = x + y
        x = x + _conv_glu(_rms_norm(x, norm_ffn[i]), ffn_w_in[i], ffn_conv_w[i],
                          ffn_conv_b[i], ffn_w_out[i])
    return _rms_norm(x, norm_final)
```

```python
import functools
import math

import jax
import jax.numpy as jnp
from jax import lax
from jax.experimental import pallas as pl
from jax.experimental.pallas import tpu as pltpu

NORM_EPS = 1e-6
GN_EPS = 64e-5
MLA_HEADS = 8
QK_NOPE_DIM = 128
QK_ROPE_DIM = 64
QK_DIM = QK_NOPE_DIM + QK_ROPE_DIM
QK_PAD = 256
V_HEAD_DIM = 128
Q_LORA_RANK = 384
KV_LORA_RANK = 256
ROPE_THETA = 10000.0
RWKV_HEAD = 64
LANES = 128
D_FF = 2816
HALO = 16
WKV_CHUNK = 64

BF16 = jnp.bfloat16
F32 = jnp.float32
VMEM_LIMIT = 56 * 1024 * 1024


def _params(*sem):
    return pltpu.CompilerParams(dimension_semantics=sem, vmem_limit_bytes=VMEM_LIMIT)


def _dot(a, b, precision=None):
    return jnp.dot(a, b, preferred_element_type=F32, precision=precision)


def _dot_nt(a, b, precision=None):
    return lax.dot_general(a, b, (((1,), (1,)), ((), ())),
                           preferred_element_type=F32, precision=precision)


def _dot_tn(a, b, precision=None):
    return lax.dot_general(a, b, (((0,), (0,)), ((), ())),
                           preferred_element_type=F32, precision=precision)


def _rms(x, g):
    return x * lax.rsqrt(jnp.mean(x * x, axis=-1, keepdims=True) + NORM_EPS) * g


def _row_tile(s, want):
    return min(want, s)


def _full(shape):
    return pl.BlockSpec(shape, lambda *_: (0,) * len(shape))


def _rope_table_kernel(pos_ref, freq_ref, cos_ref, sin_ref):
    ang = pos_ref[...] * freq_ref[...]
    cos_ref[...] = jnp.cos(ang)
    sin_ref[...] = jnp.sin(ang)


def _rope_tables(pos_col, freq_row):
    s = pos_col.shape[0]
    tm = _row_tile(s, 1024)
    return pl.pallas_call(
        _rope_table_kernel,
        grid=(s // tm,),
        in_specs=[pl.BlockSpec((tm, 1), lambda i: (i, 0)), _full((1, LANES))],
        out_specs=[pl.BlockSpec((tm, LANES), lambda i: (i, 0))] * 2,
        out_shape=[jax.ShapeDtypeStruct((s, LANES), F32)] * 2,
        compiler_params=_params("parallel"),
        name="rope_tables",
    )(pos_col, freq_row)


def _mla_pre_kernel(x_ref, g_ref, w_in_ref, qn_ref, w_uq_ref, kvn_ref, w_ukv_ref,
                    cos_ref, sin_ref, q_ref, k_ref, v_ref):
    h = _rms(x_ref[...], g_ref[...]).astype(BF16)
    lat = _dot(h, w_in_ref[...])
    q_lat = lat[:, :Q_LORA_RANK]
    c_kv = lat[:, Q_LORA_RANK:Q_LORA_RANK + KV_LORA_RANK]
    kr = lat[:, Q_LORA_RANK + KV_LORA_RANK:]
    cos = cos_ref[...]
    sin = sin_ref[...]
    k_rope = (kr[:, :QK_ROPE_DIM] * cos[:, :QK_ROPE_DIM]
              + kr[:, QK_ROPE_DIM:] * sin[:, :QK_ROPE_DIM])
    qn = _rms(q_lat, qn_ref[...]).astype(BF16)
    q_all = _dot(qn, w_uq_ref[...])
    n_nope = MLA_HEADS * QK_NOPE_DIM
    n_rope = MLA_HEADS * QK_ROPE_DIM
    reps = n_rope // LANES
    q_rope = (q_all[:, n_nope:n_nope + n_rope] * jnp.tile(cos, (1, reps))
              + q_all[:, n_nope + n_rope:] * jnp.tile(sin, (1, reps)))
    kvn = _rms(c_kv, kvn_ref[...]).astype(BF16)
    kv = _dot(kvn, w_ukv_ref[...])
    qscale = QK_DIM ** -0.5 * math.log2(math.e)
    tm = x_ref.shape[0]
    zpad = jnp.zeros((tm, QK_PAD - QK_DIM), BF16)
    k_rope_b = k_rope.astype(BF16)
    for hd in range(MLA_HEADS):
        q_ref[hd, :, 0:QK_NOPE_DIM] = (
            q_all[:, hd * QK_NOPE_DIM:(hd + 1) * QK_NOPE_DIM] * qscale).astype(BF16)
        q_ref[hd, :, QK_NOPE_DIM:QK_DIM] = (
            q_rope[:, hd * QK_ROPE_DIM:(hd + 1) * QK_ROPE_DIM] * qscale).astype(BF16)
        q_ref[hd, :, QK_DIM:QK_PAD] = zpad
        k_ref[hd, :, 0:QK_NOPE_DIM] = kv[:, hd * QK_NOPE_DIM:(hd + 1) * QK_NOPE_DIM].astype(BF16)
        k_ref[hd, :, QK_NOPE_DIM:QK_DIM] = k_rope_b
        k_ref[hd, :, QK_DIM:QK_PAD] = zpad
        v_ref[hd] = kv[:, n_nope + hd * V_HEAD_DIM:n_nope + (hd + 1) * V_HEAD_DIM].astype(BF16)


def _mla_pre(x, g, w_in_e, q_norm, w_uq_e, kv_norm, w_ukv_p, cos_t, sin_t):
    s, d = x.shape
    tm = _row_tile(s, 256)
    row = lambda w: pl.BlockSpec((tm, w), lambda i: (i, 0))
    head = lambda w: pl.BlockSpec((MLA_HEADS, tm, w), lambda i: (0, i, 0))
    return pl.pallas_call(
        _mla_pre_kernel,
        grid=(s // tm,),
        in_specs=[row(d), _full(g.shape), _full(w_in_e.shape), _full(q_norm.shape),
                  _full(w_uq_e.shape), _full(kv_norm.shape), _full(w_ukv_p.shape),
                  row(LANES), row(LANES)],
        out_specs=[head(QK_PAD), head(QK_PAD), head(V_HEAD_DIM)],
        out_shape=[jax.ShapeDtypeStruct((MLA_HEADS, s, QK_PAD), BF16),
                   jax.ShapeDtypeStruct((MLA_HEADS, s, QK_PAD), BF16),
                   jax.ShapeDtypeStruct((MLA_HEADS, s, V_HEAD_DIM), BF16)],
        compiler_params=_params("parallel"),
        name="mla_pre",
    )(x, g, w_in_e, q_norm, w_uq_e, kv_norm, w_ukv_p, cos_t, sin_t)


def _attn_kernel(q_ref, k_ref, v_ref, o_ref, m_ref, l_ref, acc_ref, *, tq, tk):
    qi = pl.program_id(1)
    q = q_ref[...]
    m_ref[...] = jnp.full(m_ref.shape, -1e30, F32)
    l_ref[...] = jnp.zeros(l_ref.shape, F32)
    acc_ref[...] = jnp.zeros(acc_ref.shape, F32)

    def step(j, masked):
        off = pl.multiple_of(j * tk, tk)
        s = _dot_nt(q, k_ref[pl.ds(off, tk), :])
        if masked:
            row = lax.broadcasted_iota(jnp.int32, (tq, tk), 0)
            col = lax.broadcasted_iota(jnp.int32, (tq, tk), 1)
            s = jnp.where(col <= row, s, jnp.finfo(F32).min)
        m_old = m_ref[...]
        m_new = jnp.maximum(m_old, jnp.max(s, axis=-1, keepdims=True))
        p = jnp.exp2(s - m_new)
        alpha = jnp.exp2(m_old - m_new)
        l_ref[...] = alpha * l_ref[...] + jnp.sum(p, axis=-1, keepdims=True)
        acc_ref[...] = alpha * acc_ref[...] + _dot(p.astype(BF16), v_ref[pl.ds(off, tk), :])
        m_ref[...] = m_new

    n_full = qi * (tq // tk)

    def body(j, carry):
        step(j, False)
        return carry

    lax.fori_loop(0, n_full, body, 0)
    step(n_full, True)
    o_ref[...] = (acc_ref[...] / l_ref[...]).astype(o_ref.dtype)


def _attention(q_cat, k_cat, v):
    nh, s, _ = q_cat.shape
    tq = _row_tile(s, 512)
    tk = tq
    return pl.pallas_call(
        functools.partial(_attn_kernel, tq=tq, tk=tk),
        grid=(nh, s // tq),
        in_specs=[pl.BlockSpec((None, tq, QK_PAD), lambda h, i: (h, i, 0)),
                  pl.BlockSpec((None, s, QK_PAD), lambda h, i: (h, 0, 0)),
                  pl.BlockSpec((None, s, V_HEAD_DIM), lambda h, i: (h, 0, 0))],
        out_specs=pl.BlockSpec((tq, V_HEAD_DIM), lambda h, i: (i, h)),
        out_shape=jax.ShapeDtypeStruct((s, nh * V_HEAD_DIM), BF16),
        scratch_shapes=[pltpu.VMEM((tq, 1), F32), pltpu.VMEM((tq, 1), F32),
                        pltpu.VMEM((tq, V_HEAD_DIM), F32)],
        compiler_params=_params("parallel", "arbitrary"),
        name="mla_attention",
    )(q_cat, k_cat, v)


def _proj_kernel(x_ref, z_ref, w_ref, o_ref):
    o_ref[...] = x_ref[...] + _dot(z_ref[...].astype(BF16), w_ref[...])


def _proj_gated_kernel(x_ref, z_ref, g_ref, w_ref, o_ref):
    o_ref[...] = x_ref[...] + _dot((z_ref[...] * g_ref[...]).astype(BF16), w_ref[...])


def _proj_residual(x, z, w, gate=None):
    s, d = x.shape
    tm = _row_tile(s, 512)
    row = pl.BlockSpec((tm, d), lambda i: (i, 0))
    zrow = pl.BlockSpec((tm, z.shape[1]), lambda i: (i, 0))
    if gate is None:
        kern, args, specs = _proj_kernel, (x, z, w), [row, zrow, _full(w.shape)]
    else:
        kern, args, specs = _proj_gated_kernel, (x, z, gate, w), [row, zrow, zrow, _full(w.shape)]
    return pl.pallas_call(
        kern, grid=(s // tm,), in_specs=specs, out_specs=row,
        out_shape=jax.ShapeDtypeStruct((s, d), F32),
        compiler_params=_params("parallel"), name="proj_residual",
    )(*args)


def _ffn_kernel(x_ref, xp_ref, g_ref, wg_ref, wu_ref, cw_ref, cb_ref, wo_ref, o_ref, hn_ref):
    i = pl.program_id(0)
    c = pl.program_id(1)
    tm = x_ref.shape[0]

    @pl.when(c == 0)
    def _():
        hn_ref[HALO:, :] = _rms(x_ref[...], g_ref[...]).astype(BF16)
        keep = (i > 0).astype(F32)
        hn_ref[:HALO, :] = (_rms(xp_ref[...], g_ref[...]) * keep).astype(BF16)

    hn = hn_ref[...]
    gate = _dot(hn, wg_ref[...])
    up = _dot(hn[HALO:], wu_ref[...])
    cw = cw_ref[...]
    g0 = gate[HALO:]
    g1 = pltpu.roll(gate, 1, 0)[HALO:]
    g2 = pltpu.roll(gate, 2, 0)[HALO:]
    conv = g0 * cw[2:3] + g1 * cw[1:2] + g2 * cw[0:1] + cb_ref[...]
    act = (conv * jax.nn.sigmoid(conv) * up).astype(BF16)
    contrib = _dot(act, wo_ref[...])

    @pl.when(c == 0)
    def _():
        o_ref[...] = x_ref[...] + contrib

    @pl.when(c != 0)
    def _():
        o_ref[...] += contrib


def _ffn(x, g, w_in, conv_w, conv_b, w_out):
    s, d = x.shape
    tm = _row_tile(s, 512)
    nc = 2
    tf = D_FF // nc
    return pl.pallas_call(
        _ffn_kernel,
        grid=(s // tm, nc),
        in_specs=[pl.BlockSpec((tm, d), lambda i, c: (i, 0)),
                  pl.BlockSpec((HALO, d), lambda i, c: (jnp.maximum(i * (tm // HALO) - 1, 0), 0)),
                  _full(g.shape),
                  pl.BlockSpec((d, tf), lambda i, c: (0, c)),
                  pl.BlockSpec((d, tf), lambda i, c: (0, c + nc)),
                  pl.BlockSpec((3, tf), lambda i, c: (0, c)),
                  pl.BlockSpec((1, tf), lambda i, c: (0, c)),
                  pl.BlockSpec((tf, d), lambda i, c: (c, 0))],
        out_specs=pl.BlockSpec((tm, d), lambda i, c: (i, 0)),
        out_shape=jax.ShapeDtypeStruct((s, d), F32),
        scratch_shapes=[pltpu.VMEM((HALO + tm, d), BF16)],
        compiler_params=_params("parallel", "arbitrary"),
        name="conv_glu",
    )(x, x, g, w_in, w_in, conv_w, conv_b, w_out)


def _group_sum(x):
    r = lax.broadcasted_iota(jnp.int32, (LANES, LANES), 0) // RWKV_HEAD
    c = lax.broadcasted_iota(jnp.int32, (LANES, LANES), 1) // RWKV_HEAD
    ones_bd = (r == c).astype(F32)
    cols = [_dot(x[:, j:j + LANES], ones_bd, precision=lax.Precision.HIGHEST)
            for j in range(0, x.shape[1], LANES)]
    return cols[0] if len(cols) == 1 else jnp.concatenate(cols, axis=1)


def _rwkv_pre_kernel(*refs, has_vres):
    if has_vres:
        (x_ref, xp_ref, g_ref, mix_ref, wrkv_ref, w0_ref, w1_ref, w2_ref, a0_ref, a1_ref, a2_ref,
         g1_ref, g2_ref, kk_ref, ka_ref, vf_ref, v0_ref, v1_ref, v2_ref,
         r_o, lw_o, k_o, v_o, a_o, b_o, g_o) = refs
    else:
        (x_ref, xp_ref, g_ref, mix_ref, wrkv_ref, w0_ref, w1_ref, w2_ref, a0_ref, a1_ref, a2_ref,
         g1_ref, g2_ref, kk_ref, ka_ref,
         r_o, lw_o, k_o, v_o, a_o, b_o, g_o) = refs
    i = pl.program_id(0)
    tm = x_ref.shape[0]
    h = _rms(x_ref[...], g_ref[...])
    hp = _rms(xp_ref[...], g_ref[...])[HALO - 1:HALO] * (i > 0).astype(F32)
    row = lax.broadcasted_iota(jnp.int32, h.shape, 0)
    shifted = jnp.where(row == 0, jnp.broadcast_to(hp, h.shape), pltpu.roll(h, 1, 0))
    xx = shifted - h
    mix = mix_ref[...]
    xm = lambda n: (h + xx * mix[n:n + 1]).astype(BF16)
    xv = xm(2)
    r = _dot(xm(0), wrkv_ref[0])
    k = _dot(xm(1), wrkv_ref[1])
    v = _dot(xv, wrkv_ref[2])
    lora_w = _dot(jnp.tanh(_dot(xm(3), w1_ref[...])).astype(BF16), w2_ref[...])
    z = -(w0_ref[...] + lora_w)
    softplus = jnp.maximum(z, 0.0) + jnp.log(1.0 + jnp.exp(-jnp.abs(z)))
    lw_o[...] = -jnp.exp(-softplus - 0.5)
    a = jax.nn.sigmoid(a0_ref[...] + _dot(_dot(xm(4), a1_ref[...]).astype(BF16), a2_ref[...]))
    g_o[...] = _dot(jax.nn.sigmoid(_dot(xm(5), g1_ref[...])).astype(BF16), g2_ref[...])
    if has_vres:
        gate_v = jax.nn.sigmoid(v0_ref[...] + _dot(_dot(xv, v1_ref[...]).astype(BF16), v2_ref[...]))
        v = v + (vf_ref[...] - v) * gate_v
    kk = k * kk_ref[...]
    kk = kk / jnp.maximum(jnp.sqrt(_group_sum(kk * kk)), 1e-12)
    r_o[...] = r
    k_o[...] = k * (1.0 + (a - 1.0) * ka_ref[...])
    v_o[...] = v
    a_o[...] = -kk
    b_o[...] = kk * a


def _rwkv_pre(x, g, p, v_first):
    s, d = x.shape
    tm = _row_tile(s, 256)
    row = pl.BlockSpec((tm, d), lambda i: (i, 0))
    prev = pl.BlockSpec((HALO, d), lambda i: (jnp.maximum(i * (tm // HALO) - 1, 0), 0))
    names = ["mix", "w_rkv", "w0", "w1", "w2", "a0", "a1", "a2", "g1", "g2", "k_k", "k_a"]
    args = [x, x, g] + [p[n] for n in names]
    specs = [row, prev, _full(g.shape)] + [_full(p[n].shape) for n in names]
    has_vres = v_first is not None
    if has_vres:
        args += [v_first, p["v0"], p["v1"], p["v2"]]
        specs += [row] + [_full(p[n].shape) for n in ("v0", "v1", "v2")]
    return pl.pallas_call(
        functools.partial(_rwkv_pre_kernel, has_vres=has_vres),
        grid=(s // tm,),
        in_specs=specs,
        out_specs=[row] * 7,
        out_shape=[jax.ShapeDtypeStruct((s, d), F32)] * 7,
        compiler_params=_params("parallel"),
        name="rwkv_pre",
    )(*args)


def _wkv_kernel(r_ref, lw_ref, k_ref, v_ref, a_ref, b_ref, rk_ref, lnw_ref, lnb_ref,
                o_ref, s_ref, *, n_chunks, prec):
    C = WKV_CHUNK
    C2 = 2 * C

    @pl.when(pl.program_id(1) == 0)
    def _():
        s_ref[...] = jnp.zeros(s_ref.shape, F32)

    lane = lax.broadcasted_iota(jnp.int32, (1, LANES), 1)
    m0 = (lane < RWKV_HEAD).astype(F32)
    m1 = 1.0 - m0
    ti = lax.broadcasted_iota(jnp.int32, (C, C), 0)
    si = lax.broadcasted_iota(jnp.int32, (C, C), 1)
    tri_incl = (si <= ti).astype(F32)
    t2 = lax.broadcasted_iota(jnp.int32, (C2, C2), 0)
    s2 = lax.broadcasted_iota(jnp.int32, (C2, C2), 1)
    same_head = (t2 // C) == (s2 // C)
    strict = (same_head & (s2 < t2)).astype(F32)
    incl = (same_head & (s2 <= t2)).astype(F32)
    eye = (t2 == s2).astype(F32)

    def level_mask(b):
        return (((t2 // (2 * b)) == (s2 // (2 * b))) & ((t2 % (2 * b)) >= b)
                & ((s2 % (2 * b)) < b)).astype(F32)

    def stack(x):
        return jnp.concatenate([x * m0, x * m1], axis=0)

    def unstack(x):
        return x[:C] + x[C:]

    def mm(a, b):
        return _dot(a.astype(prec), b.astype(prec))

    def mm_nt(a, b):
        return _dot_nt(a.astype(prec), b.astype(prec))

    def mm_tn(a, b):
        return _dot_tn(a.astype(prec), b.astype(prec))

    state = s_ref[...]
    for c in range(n_chunks):
        rows = pl.ds(c * C, C)
        r, lw, k, v, a, b = (ref[rows, :] for ref in (r_ref, lw_ref, k_ref, v_ref, a_ref, b_ref))
        cl = _dot(tri_incl, lw, precision=lax.Precision.HIGHEST)
        cl_end = cl[C - 1:C]
        e_neg = jnp.exp(-cl)
        e_rem = jnp.exp(cl_end - cl)
        a_s = stack(a * jnp.exp(cl - lw))
        r_s = stack(r * jnp.exp(cl))
        v_s = stack(v)
        bt = b * e_neg
        kt = k * e_neg
        bt2 = jnp.concatenate([bt, bt], axis=0)
        kt2 = jnp.concatenate([kt, kt], axis=0)
        l_ab = mm_nt(a_s, bt2) * strict
        l_ak = mm_nt(a_s, kt2) * strict
        m_rb = mm_nt(r_s, bt2) * incl
        m_rk = mm_nt(r_s, kt2) * incl
        t_inv = eye + l_ab * level_mask(1)
        b_sz = 2
        while b_sz < C:
            t_inv = t_inv + mm(mm(t_inv, l_ab * level_mask(b_sz)), t_inv)
            b_sz *= 2
        x = mm_nt(a_s, state) + mm(l_ak, v_s)
        u_s = mm(t_inv, x)
        y = unstack(mm_nt(r_s, state) + mm(m_rb, u_s) + mm(m_rk, v_s))
        state = (state * jnp.exp(cl_end) + mm_tn(u_s, stack(b * e_rem))
                 + mm_tn(v_s, stack(k * e_rem)))
        mu0 = jnp.sum(y * m0, axis=-1, keepdims=True) * (1.0 / RWKV_HEAD)
        mu1 = jnp.sum(y * m1, axis=-1, keepdims=True) * (1.0 / RWKV_HEAD)
        yc = y - (mu0 * m0 + mu1 * m1)
        yc2 = yc * yc
        var0 = jnp.sum(yc2 * m0, axis=-1, keepdims=True) * (1.0 / RWKV_HEAD)
        var1 = jnp.sum(yc2 * m1, axis=-1, keepdims=True) * (1.0 / RWKV_HEAD)
        yn = yc * lax.rsqrt(var0 * m0 + var1 * m1 + GN_EPS)
        rkr = r * k * rk_ref[...]
        bonus = (jnp.sum(rkr * m0, axis=-1, keepdims=True) * m0
                 + jnp.sum(rkr * m1, axis=-1, keepdims=True) * m1) * v
        o_ref[rows, :] = yn * lnw_ref[...] + lnb_ref[...] + bonus
    s_ref[...] = state


def _wkv(r, lw, k, v, a, b, r_k, ln_w, ln_b, prec=BF16):
    s, d = r.shape
    n_chunks = min(4, s // WKV_CHUNK)
    tt = n_chunks * WKV_CHUNK
    blk = pl.BlockSpec((tt, LANES), lambda p, t: (t, p))
    vec = pl.BlockSpec((1, LANES), lambda p, t: (0, p))
    return pl.pallas_call(
        functools.partial(_wkv_kernel, n_chunks=n_chunks, prec=prec),
        grid=(d // LANES, s // tt),
        in_specs=[blk] * 6 + [vec] * 3,
        out_specs=blk,
        out_shape=jax.ShapeDtypeStruct((s, d), F32),
        scratch_shapes=[pltpu.VMEM((LANES, LANES), F32)],
        compiler_params=_params("parallel", "arbitrary"),
        name="wkv7_scan",
    )(r, lw, k, v, a, b, r_k, ln_w, ln_b)


def _final_norm_kernel(x_ref, g_ref, o_ref):
    o_ref[...] = _rms(x_ref[...], g_ref[...])


def _final_norm(x, g):
    s, d = x.shape
    tm = _row_tile(s, 1024)
    row = pl.BlockSpec((tm, d), lambda i: (i, 0))
    return pl.pallas_call(
        _final_norm_kernel, grid=(s // tm,), in_specs=[row, _full(g.shape)], out_specs=row,
        out_shape=jax.ShapeDtypeStruct((s, d), F32),
        compiler_params=_params("parallel"), name="final_norm",
    )(x, g)


def _rot_cols(w):
    half = QK_ROPE_DIM // 2
    return jnp.concatenate([-w[..., half:], w[..., :half]], axis=-1)


def _mla_weights(w_in, w_uq, w_ukv):
    d = w_in.shape[0]
    lat = Q_LORA_RANK + KV_LORA_RANK
    w_kr = w_in[:, lat:]
    w_in_e = jnp.concatenate([w_in[:, :lat], w_kr, _rot_cols(w_kr)], axis=1).astype(BF16)
    uq = w_uq.reshape(Q_LORA_RANK, MLA_HEADS, QK_DIM)
    uq_rope = uq[:, :, QK_NOPE_DIM:]
    w_uq_e = jnp.concatenate(
        [uq[:, :, :QK_NOPE_DIM].reshape(Q_LORA_RANK, -1),
         uq_rope.reshape(Q_LORA_RANK, -1),
         _rot_cols(uq_rope).reshape(Q_LORA_RANK, -1)], axis=1).astype(BF16)
    ukv = w_ukv.reshape(KV_LORA_RANK, MLA_HEADS, QK_NOPE_DIM + V_HEAD_DIM)
    w_ukv_p = jnp.concatenate(
        [ukv[:, :, :QK_NOPE_DIM].reshape(KV_LORA_RANK, -1),
         ukv[:, :, QK_NOPE_DIM:].reshape(KV_LORA_RANK, -1)], axis=1).astype(BF16)
    del d
    return w_in_e, w_uq_e, w_ukv_p


def _mla_layer(x, g, cos_t, sin_t, w_in, q_norm, w_uq, kv_norm, w_ukv, w_o):
    w_in_e, w_uq_e, w_ukv_p = _mla_weights(w_in, w_uq, w_ukv)
    q_cat, k_cat, v = _mla_pre(x, g[None], w_in_e, q_norm[None], w_uq_e, kv_norm[None],
                               w_ukv_p, cos_t, sin_t)
    o = _attention(q_cat, k_cat, v)
    return _proj_residual(x, o, w_o.astype(BF16))


def _rwkv_layer(x, g, v_first, p, wkv_prec=BF16):
    row = lambda t: t.reshape(1, -1)
    pp = {
        "mix": p["mix"], "w_rkv": p["w_rkv"].astype(BF16),
        "w0": row(p["w0"]), "w1": p["w1"].astype(BF16), "w2": p["w2"].astype(BF16),
        "a0": row(p["a0"]), "a1": p["a1"].astype(BF16), "a2": p["a2"].astype(BF16),
        "g1": p["g1"].astype(BF16), "g2": p["g2"].astype(BF16),
        "k_k": row(p["k_k"]), "k_a": row(p["k_a"]),
    }
    if v_first is not None:
        pp.update(v0=row(p["v0"]), v1=p["v1"].astype(BF16), v2=p["v2"].astype(BF16))
    r, lw, k, v, a, b, gate = _rwkv_pre(x, g[None], pp, v_first)
    z = _wkv(r, lw, k, v, a, b, row(p["r_k"]), row(p["ln_w"]), row(p["ln_b"]), prec=wkv_prec)
    out = _proj_residual(x, z, p["w_o"].astype(BF16), gate=gate)
    return out, (v if v_first is None else v_first)


def _ffn_layer(x, g, w_in, conv_w, conv_b, w_out):
    return _ffn(x, g[None], w_in.astype(BF16), conv_w, conv_b[None], w_out.astype(BF16))


def kernel(x, positions, norm_mix, norm_ffn, norm_final, mla_w_in, mla_q_norm, mla_w_uq, mla_kv_norm, mla_w_ukv, mla_w_o, rwkv_mix, rwkv_w_rkv, rwkv_w0, rwkv_w1, rwkv_w2, rwkv_a0, rwkv_a1, rwkv_a2, rwkv_v0, rwkv_v1, rwkv_v2, rwkv_g1, rwkv_g2, rwkv_k_k, rwkv_k_a, rwkv_r_k, rwkv_ln_w, rwkv_ln_b, rwkv_w_o, ffn_w_in, ffn_conv_w, ffn_conv_b, ffn_w_out):
    batch, seq, d = x.shape
    assert batch == 1
    depth = norm_mix.shape[0]
    half = QK_ROPE_DIM // 2
    inv_freq = ROPE_THETA ** (-jnp.arange(half, dtype=F32) / half)
    freq_row = jnp.tile(inv_freq, LANES // half)[None]
    cos_t, sin_t = _rope_tables(positions.astype(F32).reshape(seq, 1), freq_row)
    xs = x.reshape(seq, d)
    v_first = None
    for i in range(depth):
        j = i // 2
        if i % 2 == 0:
            xs = _mla_layer(xs, norm_mix[i], cos_t, sin_t, mla_w_in[j], mla_q_norm[j], mla_w_uq[j],
                            mla_kv_norm[j], mla_w_ukv[j], mla_w_o[j])
        else:
            p = {"mix": rwkv_mix[j], "w_rkv": rwkv_w_rkv[j], "w0": rwkv_w0[j], "w1": rwkv_w1[j],
                 "w2": rwkv_w2[j], "a0": rwkv_a0[j], "a1": rwkv_a1[j], "a2": rwkv_a2[j],
                 "g1": rwkv_g1[j], "g2": rwkv_g2[j], "k_k": rwkv_k_k[j], "k_a": rwkv_k_a[j],
                 "r_k": rwkv_r_k[j], "ln_w": rwkv_ln_w[j], "ln_b": rwkv_ln_b[j], "w_o": rwkv_w_o[j]}
            if j > 0:
                p.update(v0=rwkv_v0[j - 1], v1=rwkv_v1[j - 1], v2=rwkv_v2[j - 1])
            xs, v_first = _rwkv_layer(xs, norm_mix[i], v_first, p)
        xs = _ffn_layer(xs, norm_ffn[i], ffn_w_in[i], ffn_conv_w[i], ffn_conv_b[i], ffn_w_out[i])
    return _final_norm(xs, norm_final[None]).reshape(batch, seq, d)
```

```python
import functools
import math

import jax
import jax.numpy as jnp
from jax import lax
from jax.experimental import pallas as pl
from jax.experimental.pallas import tpu as pltpu

NORM_EPS = 1e-6
GN_EPS = 64e-5
MLA_HEADS = 8
QK_NOPE_DIM = 128
QK_ROPE_DIM = 64
QK_DIM = QK_NOPE_DIM + QK_ROPE_DIM
QK_PAD = 256
V_HEAD_DIM = 128
Q_LORA_RANK = 384
KV_LORA_RANK = 256
ROPE_THETA = 10000.0
RWKV_HEAD = 64
LANES = 128
D_FF = 2816
HALO = 16
WKV_CHUNK = 64
ATTN_TK = 512

BF16 = jnp.bfloat16
F32 = jnp.float32
VMEM_LIMIT = 56 * 1024 * 1024


def _params(*sem):
    return pltpu.CompilerParams(dimension_semantics=sem, vmem_limit_bytes=VMEM_LIMIT)


def _dot(a, b, precision=None):
    return jnp.dot(a, b, preferred_element_type=F32, precision=precision)


def _dot_nt(a, b, precision=None):
    return lax.dot_general(a, b, (((1,), (1,)), ((), ())),
                           preferred_element_type=F32, precision=precision)


def _dot_tn(a, b, precision=None):
    return lax.dot_general(a, b, (((0,), (0,)), ((), ())),
                           preferred_element_type=F32, precision=precision)


def _rms(x, g):
    return x * lax.rsqrt(jnp.mean(x * x, axis=-1, keepdims=True) + NORM_EPS) * g


def _row_tile(s, want):
    return min(want, s)


def _full(shape):
    return pl.BlockSpec(shape, lambda *_: (0,) * len(shape))


def _rope_table_kernel(pos_ref, freq_ref, cos_ref, sin_ref):
    ang = pos_ref[...] * freq_ref[...]
    cos_ref[...] = jnp.cos(ang)
    sin_ref[...] = jnp.sin(ang)


def _rope_tables(pos_col, freq_row):
    s = pos_col.shape[0]
    tm = _row_tile(s, 1024)
    return pl.pallas_call(
        _rope_table_kernel,
        grid=(s // tm,),
        in_specs=[pl.BlockSpec((tm, 1), lambda i: (i, 0)), _full((1, LANES))],
        out_specs=[pl.BlockSpec((tm, LANES), lambda i: (i, 0))] * 2,
        out_shape=[jax.ShapeDtypeStruct((s, LANES), F32)] * 2,
        compiler_params=_params("parallel"),
        name="rope_tables",
    )(pos_col, freq_row)


def _mla_pre_kernel(x_ref, g_ref, w_in_ref, qn_ref, w_uq_ref, kvn_ref, w_ukv_ref,
                    cos_ref, sin_ref, q_ref, k_ref, v_ref):
    h = _rms(x_ref[...], g_ref[...]).astype(BF16)
    lat = _dot(h, w_in_ref[...])
    q_lat = lat[:, :Q_LORA_RANK]
    c_kv = lat[:, Q_LORA_RANK:Q_LORA_RANK + KV_LORA_RANK]
    kr = lat[:, Q_LORA_RANK + KV_LORA_RANK:]
    cos = cos_ref[...]
    sin = sin_ref[...]
    k_rope = (kr[:, :QK_ROPE_DIM] * cos[:, :QK_ROPE_DIM]
              + kr[:, QK_ROPE_DIM:] * sin[:, :QK_ROPE_DIM])
    qn = _rms(q_lat, qn_ref[...]).astype(BF16)
    q_all = _dot(qn, w_uq_ref[...])
    n_nope = MLA_HEADS * QK_NOPE_DIM
    n_rope = MLA_HEADS * QK_ROPE_DIM
    reps = n_rope // LANES
    qscale = QK_DIM ** -0.5 * math.log2(math.e)
    q_rope = (q_all[:, n_nope:n_nope + n_rope] * jnp.tile(cos, (1, reps))
              + q_all[:, n_nope + n_rope:] * jnp.tile(sin, (1, reps))) * qscale
    kvn = _rms(c_kv, kvn_ref[...]).astype(BF16)
    kv = _dot(kvn, w_ukv_ref[...])
    tm = x_ref.shape[0]
    qn_t = (q_all[:, :n_nope] * qscale).T.astype(BF16)
    qr_t = q_rope.T.astype(BF16)
    v_t = kv[:, n_nope:].T.astype(BF16)
    zpad_t = jnp.zeros((QK_PAD - QK_DIM, tm), BF16)
    zpad = jnp.zeros((tm, QK_PAD - QK_DIM), BF16)
    k_rope_b = k_rope.astype(BF16)
    for hd in range(MLA_HEADS):
        q_ref[hd, 0:QK_NOPE_DIM, :] = qn_t[hd * QK_NOPE_DIM:(hd + 1) * QK_NOPE_DIM]
        q_ref[hd, QK_NOPE_DIM:QK_DIM, :] = qr_t[hd * QK_ROPE_DIM:(hd + 1) * QK_ROPE_DIM]
        q_ref[hd, QK_DIM:QK_PAD, :] = zpad_t
        k_ref[hd, :, 0:QK_NOPE_DIM] = kv[:, hd * QK_NOPE_DIM:(hd + 1) * QK_NOPE_DIM].astype(BF16)
        k_ref[hd, :, QK_NOPE_DIM:QK_DIM] = k_rope_b
        k_ref[hd, :, QK_DIM:QK_PAD] = zpad
        v_ref[hd, 0] = v_t[hd * V_HEAD_DIM:(hd + 1) * V_HEAD_DIM]


def _mla_pre(x, g, w_in_e, q_norm, w_uq_e, kv_norm, w_ukv_p, cos_t, sin_t):
    s, d = x.shape
    tm = _row_tile(s, ATTN_TK)
    row = lambda w: pl.BlockSpec((tm, w), lambda i: (i, 0))
    return pl.pallas_call(
        _mla_pre_kernel,
        grid=(s // tm,),
        in_specs=[row(d), _full(g.shape), _full(w_in_e.shape), _full(q_norm.shape),
                  _full(w_uq_e.shape), _full(kv_norm.shape), _full(w_ukv_p.shape),
                  row(LANES), row(LANES)],
        out_specs=[pl.BlockSpec((MLA_HEADS, QK_PAD, tm), lambda i: (0, 0, i)),
                   pl.BlockSpec((MLA_HEADS, tm, QK_PAD), lambda i: (0, i, 0)),
                   pl.BlockSpec((MLA_HEADS, 1, V_HEAD_DIM, tm), lambda i: (0, i, 0, 0))],
        out_shape=[jax.ShapeDtypeStruct((MLA_HEADS, QK_PAD, s), BF16),
                   jax.ShapeDtypeStruct((MLA_HEADS, s, QK_PAD), BF16),
                   jax.ShapeDtypeStruct((MLA_HEADS, s // tm, V_HEAD_DIM, tm), BF16)],
        compiler_params=_params("parallel"),
        name="mla_pre",
    )(x, g, w_in_e, q_norm, w_uq_e, kv_norm, w_ukv_p, cos_t, sin_t)


def _attn_kernel(q_ref, k_ref, v_ref, o_ref, acc_a, acc_b, *, tk):
    qi = pl.program_id(1)
    accs = (acc_a, acc_b)
    for acc in accs:
        acc[...] = jnp.zeros(acc.shape, F32)

    def step(half, j, m_old, l_old, masked):
        q_t = q_ref[:, half * tk:(half + 1) * tk]
        off = pl.multiple_of(j * tk, tk)
        s = _dot(k_ref[pl.ds(off, tk), :], q_t)
        if masked:
            key = lax.broadcasted_iota(jnp.int32, s.shape, 0)
            qry = lax.broadcasted_iota(jnp.int32, s.shape, 1)
            s = jnp.where(key <= qry, s, jnp.finfo(F32).min)
        m_new = jnp.maximum(m_old, jnp.max(s, axis=0, keepdims=True))
        p = jnp.exp2(s - m_new)
        alpha = jnp.exp2(m_old - m_new)
        l_new = alpha * l_old + jnp.sum(p, axis=0, keepdims=True)
        acc = accs[half]
        acc[...] = alpha * acc[...] + _dot(v_ref[j], p.astype(BF16))
        return m_new, l_new

    m0 = jnp.full((1, tk), -1e30, F32)
    l0 = jnp.zeros((1, tk), F32)
    n_full = 2 * qi

    def body(j, carry):
        ma, la, mb, lb = carry
        ma, la = step(0, j, ma, la, False)
        mb, lb = step(1, j, mb, lb, False)
        return ma, la, mb, lb

    ma, la, mb, lb = lax.fori_loop(0, n_full, body, (m0, l0, m0, l0))
    ma, la = step(0, n_full, ma, la, True)
    mb, lb = step(1, n_full, mb, lb, False)
    mb, lb = step(1, n_full + 1, mb, lb, True)
    o_ref[0:tk, :] = (acc_a[...] / la).T.astype(o_ref.dtype)
    o_ref[tk:2 * tk, :] = (acc_b[...] / lb).T.astype(o_ref.dtype)


def _attention(q_t, k_cat, v_t):
    nh, s, _ = k_cat.shape
    tk = v_t.shape[-1]
    tq = 2 * tk
    return pl.pallas_call(
        functools.partial(_attn_kernel, tk=tk),
        grid=(nh, s // tq),
        in_specs=[pl.BlockSpec((None, QK_PAD, tq), lambda h, i: (h, 0, i)),
                  pl.BlockSpec((None, s, QK_PAD), lambda h, i: (h, 0, 0)),
                  pl.BlockSpec((None, s // tk, V_HEAD_DIM, tk), lambda h, i: (h, 0, 0, 0))],
        out_specs=pl.BlockSpec((tq, V_HEAD_DIM), lambda h, i: (i, h)),
        out_shape=jax.ShapeDtypeStruct((s, nh * V_HEAD_DIM), BF16),
        scratch_shapes=[pltpu.VMEM((V_HEAD_DIM, tk), F32), pltpu.VMEM((V_HEAD_DIM, tk), F32)],
        compiler_params=_params("parallel", "arbitrary"),
        name="mla_attention",
    )(q_t, k_cat, v_t)


def _proj_kernel(x_ref, z_ref, w_ref, o_ref):
    o_ref[...] = x_ref[...] + _dot(z_ref[...].astype(BF16), w_ref[...])


def _proj_gated_kernel(x_ref, z_ref, g_ref, w_ref, o_ref):
    o_ref[...] = x_ref[...] + _dot((z_ref[...] * g_ref[...]).astype(BF16), w_ref[...])


def _proj_residual(x, z, w, gate=None):
    s, d = x.shape
    tm = _row_tile(s, 512)
    row = pl.BlockSpec((tm, d), lambda i: (i, 0))
    zrow = pl.BlockSpec((tm, z.shape[1]), lambda i: (i, 0))
    if gate is None:
        kern, args, specs = _proj_kernel, (x, z, w), [row, zrow, _full(w.shape)]
    else:
        kern, args, specs = _proj_gated_kernel, (x, z, gate, w), [row, zrow, zrow, _full(w.shape)]
    return pl.pallas_call(
        kern, grid=(s // tm,), in_specs=specs, out_specs=row,
        out_shape=jax.ShapeDtypeStruct((s, d), F32),
        compiler_params=_params("parallel"), name="proj_residual",
    )(*args)


def _ffn_kernel(x_ref, xp_ref, g_ref, wg_ref, wu_ref, cw_ref, cb_ref, wo_ref, o_ref, hn_ref):
    i = pl.program_id(0)
    c = pl.program_id(1)
    tm = x_ref.shape[0]

    @pl.when(c == 0)
    def _():
        hn_ref[HALO:, :] = _rms(x_ref[...], g_ref[...]).astype(BF16)
        keep = (i > 0).astype(F32)
        hn_ref[:HALO, :] = (_rms(xp_ref[...], g_ref[...]) * keep).astype(BF16)

    hn = hn_ref[...]
    gate = _dot(hn, wg_ref[...])
    up = _dot(hn[HALO:], wu_ref[...])
    cw = cw_ref[...]
    g0 = gate[HALO:]
    g1 = pltpu.roll(gate, 1, 0)[HALO:]
    g2 = pltpu.roll(gate, 2, 0)[HALO:]
    conv = g0 * cw[2:3] + g1 * cw[1:2] + g2 * cw[0:1] + cb_ref[...]
    act = (conv * jax.nn.sigmoid(conv) * up).astype(BF16)
    contrib = _dot(act, wo_ref[...])

    @pl.when(c == 0)
    def _():
        o_ref[...] = x_ref[...] + contrib

    @pl.when(c != 0)
    def _():
        o_ref[...] += contrib


def _ffn(x, g, w_in, conv_w, conv_b, w_out):
    s, d = x.shape
    tm = _row_tile(s, 512)
    nc = 2
    tf = D_FF // nc
    return pl.pallas_call(
        _ffn_kernel,
        grid=(s // tm, nc),
        in_specs=[pl.BlockSpec((tm, d), lambda i, c: (i, 0)),
                  pl.BlockSpec((HALO, d), lambda i, c: (jnp.maximum(i * (tm // HALO) - 1, 0), 0)),
                  _full(g.shape),
                  pl.BlockSpec((d, tf), lambda i, c: (0, c)),
                  pl.BlockSpec((d, tf), lambda i, c: (0, c + nc)),
                  pl.BlockSpec((3, tf), lambda i, c: (0, c)),
                  pl.BlockSpec((1, tf), lambda i, c: (0, c)),
                  pl.BlockSpec((tf, d), lambda i, c: (c, 0))],
        out_specs=pl.BlockSpec((tm, d), lambda i, c: (i, 0)),
        out_shape=jax.ShapeDtypeStruct((s, d), F32),
        scratch_shapes=[pltpu.VMEM((HALO + tm, d), BF16)],
        compiler_params=_params("parallel", "arbitrary"),
        name="conv_glu",
    )(x, x, g, w_in, w_in, conv_w, conv_b, w_out)


def _group_sum(x):
    r = lax.broadcasted_iota(jnp.int32, (LANES, LANES), 0) // RWKV_HEAD
    c = lax.broadcasted_iota(jnp.int32, (LANES, LANES), 1) // RWKV_HEAD
    ones_bd = (r == c).astype(F32)
    cols = [_dot(x[:, j:j + LANES], ones_bd, precision=lax.Precision.HIGHEST)
            for j in range(0, x.shape[1], LANES)]
    return cols[0] if len(cols) == 1 else jnp.concatenate(cols, axis=1)


def _rwkv_pre_kernel(*refs, has_vres):
    if has_vres:
        (x_ref, xp_ref, g_ref, mix_ref, wrkv_ref, w0_ref, w1_ref, w2_ref, a0_ref, a1_ref, a2_ref,
         g1_ref, g2_ref, kk_ref, ka_ref, vf_ref, v0_ref, v1_ref, v2_ref,
         r_o, lw_o, k_o, v_o, a_o, b_o, g_o) = refs
    else:
        (x_ref, xp_ref, g_ref, mix_ref, wrkv_ref, w0_ref, w1_ref, w2_ref, a0_ref, a1_ref, a2_ref,
         g1_ref, g2_ref, kk_ref, ka_ref,
         r_o, lw_o, k_o, v_o, a_o, b_o, g_o) = refs
    i = pl.program_id(0)
    tm = x_ref.shape[0]
    h = _rms(x_ref[...], g_ref[...])
    hp = _rms(xp_ref[...], g_ref[...])[HALO - 1:HALO] * (i > 0).astype(F32)
    row = lax.broadcasted_iota(jnp.int32, h.shape, 0)
    shifted = jnp.where(row == 0, jnp.broadcast_to(hp, h.shape), pltpu.roll(h, 1, 0))
    xx = shifted - h
    mix = mix_ref[...]
    xm = lambda n: (h + xx * mix[n:n + 1]).astype(BF16)
    xv = xm(2)
    r = _dot(xm(0), wrkv_ref[0])
    k = _dot(xm(1), wrkv_ref[1])
    v = _dot(xv, wrkv_ref[2])
    lora_w = _dot(jnp.tanh(_dot(xm(3), w1_ref[...])).astype(BF16), w2_ref[...])
    z = -(w0_ref[...] + lora_w)
    softplus = jnp.maximum(z, 0.0) + jnp.log(1.0 + jnp.exp(-jnp.abs(z)))
    lw_o[...] = -jnp.exp(-softplus - 0.5)
    a = jax.nn.sigmoid(a0_ref[...] + _dot(_dot(xm(4), a1_ref[...]).astype(BF16), a2_ref[...]))
    g_o[...] = _dot(jax.nn.sigmoid(_dot(xm(5), g1_ref[...])).astype(BF16), g2_ref[...])
    if has_vres:
        gate_v = jax.nn.sigmoid(v0_ref[...] + _dot(_dot(xv, v1_ref[...]).astype(BF16), v2_ref[...]))
        v = v + (vf_ref[...] - v) * gate_v
    kk = k * kk_ref[...]
    kk = kk / jnp.maximum(jnp.sqrt(_group_sum(kk * kk)), 1e-12)
    r_o[...] = r
    k_o[...] = k * (1.0 + (a - 1.0) * ka_ref[...])
    v_o[...] = v
    a_o[...] = -kk
    b_o[...] = kk * a


def _rwkv_pre(x, g, p, v_first):
    s, d = x.shape
    tm = _row_tile(s, 256)
    row = pl.BlockSpec((tm, d), lambda i: (i, 0))
    prev = pl.BlockSpec((HALO, d), lambda i: (jnp.maximum(i * (tm // HALO) - 1, 0), 0))
    names = ["mix", "w_rkv", "w0", "w1", "w2", "a0", "a1", "a2", "g1", "g2", "k_k", "k_a"]
    args = [x, x, g] + [p[n] for n in names]
    specs = [row, prev, _full(g.shape)] + [_full(p[n].shape) for n in names]
    has_vres = v_first is not None
    if has_vres:
        args += [v_first, p["v0"], p["v1"], p["v2"]]
        specs += [row] + [_full(p[n].shape) for n in ("v0", "v1", "v2")]
    return pl.pallas_call(
        functools.partial(_rwkv_pre_kernel, has_vres=has_vres),
        grid=(s // tm,),
        in_specs=specs,
        out_specs=[row] * 7,
        out_shape=[jax.ShapeDtypeStruct((s, d), F32)] * 7,
        compiler_params=_params("parallel"),
        name="rwkv_pre",
    )(*args)


def _wkv_kernel(r_ref, lw_ref, k_ref, v_ref, a_ref, b_ref, rk_ref, lnw_ref, lnb_ref,
                o_ref, s_ref, *, n_chunks, prec):
    C = WKV_CHUNK
    C2 = 2 * C

    @pl.when(pl.program_id(1) == 0)
    def _():
        s_ref[...] = jnp.zeros(s_ref.shape, F32)

    lane = lax.broadcasted_iota(jnp.int32, (1, LANES), 1)
    m0 = (lane < RWKV_HEAD).astype(F32)
    m1 = 1.0 - m0
    ti = lax.broadcasted_iota(jnp.int32, (C, C), 0)
    si = lax.broadcasted_iota(jnp.int32, (C, C), 1)
    tri_incl = (si <= ti).astype(F32)
    t2 = lax.broadcasted_iota(jnp.int32, (C2, C2), 0)
    s2 = lax.broadcasted_iota(jnp.int32, (C2, C2), 1)
    same_head = (t2 // C) == (s2 // C)
    strict = (same_head & (s2 < t2)).astype(F32)
    incl = (same_head & (s2 <= t2)).astype(F32)
    eye = (t2 == s2).astype(F32)

    def level_mask(b):
        return (((t2 // (2 * b)) == (s2 // (2 * b))) & ((t2 % (2 * b)) >= b)
                & ((s2 % (2 * b)) < b)).astype(F32)

    def stack(x):
        return jnp.concatenate([x * m0, x * m1], axis=0)

    def unstack(x):
        return x[:C] + x[C:]

    def mm(a, b):
        return _dot(a.astype(prec), b.astype(prec))

    def mm_nt(a, b):
        return _dot_nt(a.astype(prec), b.astype(prec))

    def mm_tn(a, b):
        return _dot_tn(a.astype(prec), b.astype(prec))

    state = s_ref[...]
    for c in range(n_chunks):
        rows = pl.ds(c * C, C)
        r, lw, k, v, a, b = (ref[rows, :] for ref in (r_ref, lw_ref, k_ref, v_ref, a_ref, b_ref))
        cl = _dot(tri_incl, lw, precision=lax.Precision.HIGHEST)
        cl_end = cl[C - 1:C]
        e_neg = jnp.exp(-cl)
        e_rem = jnp.exp(cl_end - cl)
        a_s = stack(a * jnp.exp(cl - lw))
        r_s = stack(r * jnp.exp(cl))
        v_s = stack(v)
        bt = b * e_neg
        kt = k * e_neg
        bt2 = jnp.concatenate([bt, bt], axis=0)
        kt2 = jnp.concatenate([kt, kt], axis=0)
        l_ab = mm_nt(a_s, bt2) * strict
        l_ak = mm_nt(a_s, kt2) * strict
        m_rb = mm_nt(r_s, bt2) * incl
        m_rk = mm_nt(r_s, kt2) * incl
        t_inv = eye + l_ab * level_mask(1)
        b_sz = 2
        while b_sz < C:
            t_inv = t_inv + mm(mm(t_inv, l_ab * level_mask(b_sz)), t_inv)
            b_sz *= 2
        x = mm_nt(a_s, state) + mm(l_ak, v_s)
        u_s = mm(t_inv, x)
        y = unstack(mm_nt(r_s, state) + mm(m_rb, u_s) + mm(m_rk, v_s))
        state = (state * jnp.exp(cl_end) + mm_tn(u_s, stack(b * e_rem))
                 + mm_tn(v_s, stack(k * e_rem)))
        mu0 = jnp.sum(y * m0, axis=-1, keepdims=True) * (1.0 / RWKV_HEAD)
        mu1 = jnp.sum(y * m1, axis=-1, keepdims=True) * (1.0 / RWKV_HEAD)
        yc = y - (mu0 * m0 + mu1 * m1)
        yc2 = yc * yc
        var0 = jnp.sum(yc2 * m0, axis=-1, keepdims=True) * (1.0 / RWKV_HEAD)
        var1 = jnp.sum(yc2 * m1, axis=-1, keepdims=True) * (1.0 / RWKV_HEAD)
        yn = yc * lax.rsqrt(var0 * m0 + var1 * m1 + GN_EPS)
        rkr = r * k * rk_ref[...]
        bonus = (jnp.sum(rkr * m0, axis=-1, keepdims=True) * m0
                 + jnp.sum(rkr * m1, axis=-1, keepdims=True) * m1) * v
        o_ref[rows, :] = yn * lnw_ref[...] + lnb_ref[...] + bonus
    s_ref[...] = state


def _wkv(r, lw, k, v, a, b, r_k, ln_w, ln_b, prec=BF16):
    s, d = r.shape
    n_chunks = min(4, s // WKV_CHUNK)
    tt = n_chunks * WKV_CHUNK
    blk = pl.BlockSpec((tt, LANES), lambda p, t: (t, p))
    vec = pl.BlockSpec((1, LANES), lambda p, t: (0, p))
    return pl.pallas_call(
        functools.partial(_wkv_kernel, n_chunks=n_chunks, prec=prec),
        grid=(d // LANES, s // tt),
        in_specs=[blk] * 6 + [vec] * 3,
        out_specs=blk,
        out_shape=jax.ShapeDtypeStruct((s, d), F32),
        scratch_shapes=[pltpu.VMEM((LANES, LANES), F32)],
        compiler_params=_params("parallel", "arbitrary"),
        name="wkv7_scan",
    )(r, lw, k, v, a, b, r_k, ln_w, ln_b)


def _final_norm_kernel(x_ref, g_ref, o_ref):
    o_ref[...] = _rms(x_ref[...], g_ref[...])


def _final_norm(x, g):
    s, d = x.shape
    tm = _row_tile(s, 1024)
    row = pl.BlockSpec((tm, d), lambda i: (i, 0))
    return pl.pallas_call(
        _final_norm_kernel, grid=(s // tm,), in_specs=[row, _full(g.shape)], out_specs=row,
        out_shape=jax.ShapeDtypeStruct((s, d), F32),
        compiler_params=_params("parallel"), name="final_norm",
    )(x, g)


def _rot_cols(w):
    half = QK_ROPE_DIM // 2
    return jnp.concatenate([-w[..., half:], w[..., :half]], axis=-1)


def _mla_weights(w_in, w_uq, w_ukv):
    d = w_in.shape[0]
    lat = Q_LORA_RANK + KV_LORA_RANK
    w_kr = w_in[:, lat:]
    w_in_e = jnp.concatenate([w_in[:, :lat], w_kr, _rot_cols(w_kr)], axis=1).astype(BF16)
    uq = w_uq.reshape(Q_LORA_RANK, MLA_HEADS, QK_DIM)
    uq_rope = uq[:, :, QK_NOPE_DIM:]
    w_uq_e = jnp.concatenate(
        [uq[:, :, :QK_NOPE_DIM].reshape(Q_LORA_RANK, -1),
         uq_rope.reshape(Q_LORA_RANK, -1),
         _rot_cols(uq_rope).reshape(Q_LORA_RANK, -1)], axis=1).astype(BF16)
    ukv = w_ukv.reshape(KV_LORA_RANK, MLA_HEADS, QK_NOPE_DIM + V_HEAD_DIM)
    w_ukv_p = jnp.concatenate(
        [ukv[:, :, :QK_NOPE_DIM].reshape(KV_LORA_RANK, -1),
         ukv[:, :, QK_NOPE_DIM:].reshape(KV_LORA_RANK, -1)], axis=1).astype(BF16)
    del d
    return w_in_e, w_uq_e, w_ukv_p


def _mla_layer(x, g, cos_t, sin_t, w_in, q_norm, w_uq, kv_norm, w_ukv, w_o):
    w_in_e, w_uq_e, w_ukv_p = _mla_weights(w_in, w_uq, w_ukv)
    q_cat, k_cat, v = _mla_pre(x, g[None], w_in_e, q_norm[None], w_uq_e, kv_norm[None],
                               w_ukv_p, cos_t, sin_t)
    o = _attention(q_cat, k_cat, v)
    return _proj_residual(x, o, w_o.astype(BF16))


def _rwkv_layer(x, g, v_first, p, wkv_prec=BF16):
    row = lambda t: t.reshape(1, -1)
    pp = {
        "mix": p["mix"], "w_rkv": p["w_rkv"].astype(BF16),
        "w0": row(p["w0"]), "w1": p["w1"].astype(BF16), "w2": p["w2"].astype(BF16),
        "a0": row(p["a0"]), "a1": p["a1"].astype(BF16), "a2": p["a2"].astype(BF16),
        "g1": p["g1"].astype(BF16), "g2": p["g2"].astype(BF16),
        "k_k": row(p["k_k"]), "k_a": row(p["k_a"]),
    }
    if v_first is not None:
        pp.update(v0=row(p["v0"]), v1=p["v1"].astype(BF16), v2=p["v2"].astype(BF16))
    r, lw, k, v, a, b, gate = _rwkv_pre(x, g[None], pp, v_first)
    z = _wkv(r, lw, k, v, a, b, row(p["r_k"]), row(p["ln_w"]), row(p["ln_b"]), prec=wkv_prec)
    out = _proj_residual(x, z, p["w_o"].astype(BF16), gate=gate)
    return out, (v if v_first is None else v_first)


def _ffn_layer(x, g, w_in, conv_w, conv_b, w_out):
    return _ffn(x, g[None], w_in.astype(BF16), conv_w, conv_b[None], w_out.astype(BF16))


def kernel(x, positions, norm_mix, norm_ffn, norm_final, mla_w_in, mla_q_norm, mla_w_uq, mla_kv_norm, mla_w_ukv, mla_w_o, rwkv_mix, rwkv_w_rkv, rwkv_w0, rwkv_w1, rwkv_w2, rwkv_a0, rwkv_a1, rwkv_a2, rwkv_v0, rwkv_v1, rwkv_v2, rwkv_g1, rwkv_g2, rwkv_k_k, rwkv_k_a, rwkv_r_k, rwkv_ln_w, rwkv_ln_b, rwkv_w_o, ffn_w_in, ffn_conv_w, ffn_conv_b, ffn_w_out):
    batch, seq, d = x.shape
    assert batch == 1
    depth = norm_mix.shape[0]
    half = QK_ROPE_DIM // 2
    inv_freq = ROPE_THETA ** (-jnp.arange(half, dtype=F32) / half)
    freq_row = jnp.tile(inv_freq, LANES // half)[None]
    cos_t, sin_t = _rope_tables(positions.astype(F32).reshape(seq, 1), freq_row)
    xs = x.reshape(seq, d)
    v_first = None
    for i in range(depth):
        j = i // 2
        if i % 2 == 0:
            xs = _mla_layer(xs, norm_mix[i], cos_t, sin_t, mla_w_in[j], mla_q_norm[j], mla_w_uq[j],
                            mla_kv_norm[j], mla_w_ukv[j], mla_w_o[j])
        else:
            p = {"mix": rwkv_mix[j], "w_rkv": rwkv_w_rkv[j], "w0": rwkv_w0[j], "w1": rwkv_w1[j],
                 "w2": rwkv_w2[j], "a0": rwkv_a0[j], "a1": rwkv_a1[j], "a2": rwkv_a2[j],
                 "g1": rwkv_g1[j], "g2": rwkv_g2[j], "k_k": rwkv_k_k[j], "k_a": rwkv_k_a[j],
                 "r_k": rwkv_r_k[j], "ln_w": rwkv_ln_w[j], "ln_b": rwkv_ln_b[j], "w_o": rwkv_w_o[j]}
            if j > 0:
                p.update(v0=rwkv_v0[j - 1], v1=rwkv_v1[j - 1], v2=rwkv_v2[j - 1])
            xs, v_first = _rwkv_layer(xs, norm_mix[i], v_first, p)
        xs = _ffn_layer(xs, norm_ffn[i], ffn_w_in[i], ffn_conv_w[i], ffn_conv_b[i], ffn_w_out[i])
    return _final_norm(xs, norm_final[None]).reshape(batch, seq, d)
```

```python
import functools
import math

import jax
import jax.numpy as jnp
from jax import lax
from jax.experimental import pallas as pl
from jax.experimental.pallas import tpu as pltpu

NORM_EPS = 1e-6
GN_EPS = 64e-5
MLA_HEADS = 8
QK_NOPE_DIM = 128
QK_ROPE_DIM = 64
QK_DIM = QK_NOPE_DIM + QK_ROPE_DIM
QK_PAD = 256
V_HEAD_DIM = 128
Q_LORA_RANK = 384
KV_LORA_RANK = 256
ROPE_THETA = 10000.0
RWKV_HEAD = 64
LANES = 128
D_FF = 2816
HALO = 16
WKV_CHUNK = 64
WKV_TILE = 256
WKV_STEP = 512
SUBLANES = 8
ATTN_TK = 512

BF16 = jnp.bfloat16
F32 = jnp.float32
VMEM_LIMIT = 56 * 1024 * 1024


def _params(*sem):
    return pltpu.CompilerParams(dimension_semantics=sem, vmem_limit_bytes=VMEM_LIMIT)


def _dot(a, b, precision=None):
    return jnp.dot(a, b, preferred_element_type=F32, precision=precision)


def _dot_nt(a, b, precision=None):
    return lax.dot_general(a, b, (((1,), (1,)), ((), ())),
                           preferred_element_type=F32, precision=precision)


def _dot_tn(a, b, precision=None):
    return lax.dot_general(a, b, (((0,), (0,)), ((), ())),
                           preferred_element_type=F32, precision=precision)


def _rms(x, g):
    return x * lax.rsqrt(jnp.mean(x * x, axis=-1, keepdims=True) + NORM_EPS) * g


def _row_tile(s, want):
    return min(want, s)


def _full(shape):
    return pl.BlockSpec(shape, lambda *_: (0,) * len(shape))


def _rope_table_kernel(pos_ref, freq_ref, cos_ref, sin_ref):
    ang = pos_ref[...] * freq_ref[...]
    cos_ref[...] = jnp.cos(ang)
    sin_ref[...] = jnp.sin(ang)


def _rope_tables(pos_col, freq_row):
    s = pos_col.shape[0]
    tm = _row_tile(s, 1024)
    return pl.pallas_call(
        _rope_table_kernel,
        grid=(s // tm,),
        in_specs=[pl.BlockSpec((tm, 1), lambda i: (i, 0)), _full((1, LANES))],
        out_specs=[pl.BlockSpec((tm, LANES), lambda i: (i, 0))] * 2,
        out_shape=[jax.ShapeDtypeStruct((s, LANES), F32)] * 2,
        compiler_params=_params("parallel"),
        name="rope_tables",
    )(pos_col, freq_row)


def _mla_pre_kernel(x_ref, g_ref, w_in_ref, qn_ref, w_uq_ref, kvn_ref, w_ukv_ref,
                    cos_ref, sin_ref, q_ref, k_ref, v_ref):
    h = _rms(x_ref[...], g_ref[...]).astype(BF16)
    lat = _dot(h, w_in_ref[...])
    q_lat = lat[:, :Q_LORA_RANK]
    c_kv = lat[:, Q_LORA_RANK:Q_LORA_RANK + KV_LORA_RANK]
    kr = lat[:, Q_LORA_RANK + KV_LORA_RANK:]
    cos = cos_ref[...]
    sin = sin_ref[...]
    k_rope = (kr[:, :QK_ROPE_DIM] * cos[:, :QK_ROPE_DIM]
              + kr[:, QK_ROPE_DIM:] * sin[:, :QK_ROPE_DIM])
    qn = _rms(q_lat, qn_ref[...]).astype(BF16)
    q_all = _dot(qn, w_uq_ref[...])
    n_nope = MLA_HEADS * QK_NOPE_DIM
    n_rope = MLA_HEADS * QK_ROPE_DIM
    reps = n_rope // LANES
    qscale = QK_DIM ** -0.5 * math.log2(math.e)
    q_rope = (q_all[:, n_nope:n_nope + n_rope] * jnp.tile(cos, (1, reps))
              + q_all[:, n_nope + n_rope:] * jnp.tile(sin, (1, reps))) * qscale
    kvn = _rms(c_kv, kvn_ref[...]).astype(BF16)
    kv = _dot(kvn, w_ukv_ref[...])
    tm = x_ref.shape[0]
    qn_t = (q_all[:, :n_nope] * qscale).T.astype(BF16)
    qr_t = q_rope.T.astype(BF16)
    v_t = kv[:, n_nope:].T.astype(BF16)
    zpad_t = jnp.zeros((QK_PAD - QK_DIM, tm), BF16)
    zpad = jnp.zeros((tm, QK_PAD - QK_DIM), BF16)
    k_rope_b = k_rope.astype(BF16)
    for hd in range(MLA_HEADS):
        q_ref[hd, 0:QK_NOPE_DIM, :] = qn_t[hd * QK_NOPE_DIM:(hd + 1) * QK_NOPE_DIM]
        q_ref[hd, QK_NOPE_DIM:QK_DIM, :] = qr_t[hd * QK_ROPE_DIM:(hd + 1) * QK_ROPE_DIM]
        q_ref[hd, QK_DIM:QK_PAD, :] = zpad_t
        k_ref[hd, :, 0:QK_NOPE_DIM] = kv[:, hd * QK_NOPE_DIM:(hd + 1) * QK_NOPE_DIM].astype(BF16)
        k_ref[hd, :, QK_NOPE_DIM:QK_DIM] = k_rope_b
        k_ref[hd, :, QK_DIM:QK_PAD] = zpad
        v_ref[hd, 0] = v_t[hd * V_HEAD_DIM:(hd + 1) * V_HEAD_DIM]


def _mla_pre(x, g, w_in_e, q_norm, w_uq_e, kv_norm, w_ukv_p, cos_t, sin_t):
    s, d = x.shape
    tm = _row_tile(s, ATTN_TK)
    row = lambda w: pl.BlockSpec((tm, w), lambda i: (i, 0))
    return pl.pallas_call(
        _mla_pre_kernel,
        grid=(s // tm,),
        in_specs=[row(d), _full(g.shape), _full(w_in_e.shape), _full(q_norm.shape),
                  _full(w_uq_e.shape), _full(kv_norm.shape), _full(w_ukv_p.shape),
                  row(LANES), row(LANES)],
        out_specs=[pl.BlockSpec((MLA_HEADS, QK_PAD, tm), lambda i: (0, 0, i)),
                   pl.BlockSpec((MLA_HEADS, tm, QK_PAD), lambda i: (0, i, 0)),
                   pl.BlockSpec((MLA_HEADS, 1, V_HEAD_DIM, tm), lambda i: (0, i, 0, 0))],
        out_shape=[jax.ShapeDtypeStruct((MLA_HEADS, QK_PAD, s), BF16),
                   jax.ShapeDtypeStruct((MLA_HEADS, s, QK_PAD), BF16),
                   jax.ShapeDtypeStruct((MLA_HEADS, s // tm, V_HEAD_DIM, tm), BF16)],
        compiler_params=_params("parallel"),
        name="mla_pre",
    )(x, g, w_in_e, q_norm, w_uq_e, kv_norm, w_ukv_p, cos_t, sin_t)


def _attn_kernel(q_ref, k_ref, v_ref, o_ref, acc_a, acc_b, *, tk):
    qi = pl.program_id(1)
    accs = (acc_a, acc_b)
    for acc in accs:
        acc[...] = jnp.zeros(acc.shape, F32)

    def step(half, j, m_old, l_old, masked):
        q_t = q_ref[:, half * tk:(half + 1) * tk]
        off = pl.multiple_of(j * tk, tk)
        s = _dot(k_ref[pl.ds(off, tk), :], q_t)
        if masked:
            key = lax.broadcasted_iota(jnp.int32, s.shape, 0)
            qry = lax.broadcasted_iota(jnp.int32, s.shape, 1)
            s = jnp.where(key <= qry, s, jnp.finfo(F32).min)
        m_new = jnp.maximum(m_old, jnp.max(s, axis=0, keepdims=True))
        p = jnp.exp2(s - m_new)
        alpha = jnp.exp2(m_old - m_new)
        l_new = alpha * l_old + jnp.sum(p, axis=0, keepdims=True)
        acc = accs[half]
        acc[...] = alpha * acc[...] + _dot(v_ref[j], p.astype(BF16))
        return m_new, l_new

    m0 = jnp.full((1, tk), -1e30, F32)
    l0 = jnp.zeros((1, tk), F32)
    n_full = 2 * qi

    def body(j, carry):
        ma, la, mb, lb = carry
        ma, la = step(0, j, ma, la, False)
        mb, lb = step(1, j, mb, lb, False)
        return ma, la, mb, lb

    ma, la, mb, lb = lax.fori_loop(0, n_full, body, (m0, l0, m0, l0))
    ma, la = step(0, n_full, ma, la, True)
    mb, lb = step(1, n_full, mb, lb, False)
    mb, lb = step(1, n_full + 1, mb, lb, True)
    o_ref[0:tk, :] = (acc_a[...] / la).T.astype(o_ref.dtype)
    o_ref[tk:2 * tk, :] = (acc_b[...] / lb).T.astype(o_ref.dtype)


def _attention(q_t, k_cat, v_t):
    nh, s, _ = k_cat.shape
    tk = v_t.shape[-1]
    tq = 2 * tk
    return pl.pallas_call(
        functools.partial(_attn_kernel, tk=tk),
        grid=(nh, s // tq),
        in_specs=[pl.BlockSpec((None, QK_PAD, tq), lambda h, i: (h, 0, i)),
                  pl.BlockSpec((None, s, QK_PAD), lambda h, i: (h, 0, 0)),
                  pl.BlockSpec((None, s // tk, V_HEAD_DIM, tk), lambda h, i: (h, 0, 0, 0))],
        out_specs=pl.BlockSpec((tq, V_HEAD_DIM), lambda h, i: (i, h)),
        out_shape=jax.ShapeDtypeStruct((s, nh * V_HEAD_DIM), BF16),
        scratch_shapes=[pltpu.VMEM((V_HEAD_DIM, tk), F32), pltpu.VMEM((V_HEAD_DIM, tk), F32)],
        compiler_params=_params("parallel", "arbitrary"),
        name="mla_attention",
    )(q_t, k_cat, v_t)


def _proj_kernel(x_ref, z_ref, w_ref, o_ref):
    o_ref[...] = x_ref[...] + _dot(z_ref[...].astype(BF16), w_ref[...])


def _proj_gated_kernel(x_ref, z_ref, g_ref, w_ref, o_ref):
    o_ref[...] = x_ref[...] + _dot((z_ref[...] * g_ref[...]).astype(BF16), w_ref[...])


def _proj_residual(x, z, w, gate=None):
    s, d = x.shape
    tm = _row_tile(s, 512)
    row = pl.BlockSpec((tm, d), lambda i: (i, 0))
    zrow = pl.BlockSpec((tm, z.shape[1]), lambda i: (i, 0))
    if gate is None:
        kern, args, specs = _proj_kernel, (x, z, w), [row, zrow, _full(w.shape)]
    else:
        kern, args, specs = _proj_gated_kernel, (x, z, gate, w), [row, zrow, zrow, _full(w.shape)]
    return pl.pallas_call(
        kern, grid=(s // tm,), in_specs=specs, out_specs=row,
        out_shape=jax.ShapeDtypeStruct((s, d), F32),
        compiler_params=_params("parallel"), name="proj_residual",
    )(*args)


def _ffn_kernel(x_ref, xp_ref, g_ref, wg_ref, wu_ref, cw_ref, cb_ref, wo_ref, o_ref, hn_ref):
    i = pl.program_id(0)
    c = pl.program_id(1)
    tm = x_ref.shape[0]

    @pl.when(c == 0)
    def _():
        hn_ref[HALO:, :] = _rms(x_ref[...], g_ref[...]).astype(BF16)
        keep = (i > 0).astype(F32)
        hn_ref[:HALO, :] = (_rms(xp_ref[...], g_ref[...]) * keep).astype(BF16)

    hn = hn_ref[...]
    gate = _dot(hn, wg_ref[...])
    up = _dot(hn[HALO:], wu_ref[...])
    cw = cw_ref[...]
    g0 = gate[HALO:]
    g1 = pltpu.roll(gate, 1, 0)[HALO:]
    g2 = pltpu.roll(gate, 2, 0)[HALO:]
    conv = g0 * cw[2:3] + g1 * cw[1:2] + g2 * cw[0:1] + cb_ref[...]
    act = (conv * jax.nn.sigmoid(conv) * up).astype(BF16)
    contrib = _dot(act, wo_ref[...])

    @pl.when(c == 0)
    def _():
        o_ref[...] = x_ref[...] + contrib

    @pl.when(c != 0)
    def _():
        o_ref[...] += contrib


def _ffn(x, g, w_in, conv_w, conv_b, w_out):
    s, d = x.shape
    tm = _row_tile(s, 512)
    nc = 2
    tf = D_FF // nc
    return pl.pallas_call(
        _ffn_kernel,
        grid=(s // tm, nc),
        in_specs=[pl.BlockSpec((tm, d), lambda i, c: (i, 0)),
                  pl.BlockSpec((HALO, d), lambda i, c: (jnp.maximum(i * (tm // HALO) - 1, 0), 0)),
                  _full(g.shape),
                  pl.BlockSpec((d, tf), lambda i, c: (0, c)),
                  pl.BlockSpec((d, tf), lambda i, c: (0, c + nc)),
                  pl.BlockSpec((3, tf), lambda i, c: (0, c)),
                  pl.BlockSpec((1, tf), lambda i, c: (0, c)),
                  pl.BlockSpec((tf, d), lambda i, c: (c, 0))],
        out_specs=pl.BlockSpec((tm, d), lambda i, c: (i, 0)),
        out_shape=jax.ShapeDtypeStruct((s, d), F32),
        scratch_shapes=[pltpu.VMEM((HALO + tm, d), BF16)],
        compiler_params=_params("parallel", "arbitrary"),
        name="conv_glu",
    )(x, x, g, w_in, w_in, conv_w, conv_b, w_out)


def _group_sum(x):
    r = lax.broadcasted_iota(jnp.int32, (LANES, LANES), 0) // RWKV_HEAD
    c = lax.broadcasted_iota(jnp.int32, (LANES, LANES), 1) // RWKV_HEAD
    ones_bd = (r == c).astype(F32)
    cols = [_dot(x[:, j:j + LANES], ones_bd, precision=lax.Precision.HIGHEST)
            for j in range(0, x.shape[1], LANES)]
    return cols[0] if len(cols) == 1 else jnp.concatenate(cols, axis=1)


def _rwkv_pre_kernel(*refs, has_vres):
    (x_ref, xp_ref, g_ref, mix_ref, wrkv_ref, w0_ref, w1_ref, w2_ref, a0_ref, a1_ref, a2_ref,
     g1_ref, g2_ref, kk_ref, ka_ref, rk_ref) = refs[:16]
    if has_vres:
        vf_ref, v0_ref, v1_ref, v2_ref = refs[16:20]
        at_o, rt_o, bt_o, kt_o, bh_o, kh_o, vb_o, wc_o, bonus_o, g_o = refs[20:]
    else:
        at_o, rt_o, bt_o, kt_o, bh_o, kh_o, vb_o, wc_o, bonus_o, g_o, v_o = refs[16:]
    i = pl.program_id(0)
    tm = x_ref.shape[0]
    h = _rms(x_ref[...], g_ref[...])
    hp = _rms(xp_ref[...], g_ref[...])[HALO - 1:HALO] * (i > 0).astype(F32)
    row = lax.broadcasted_iota(jnp.int32, h.shape, 0)
    shifted = jnp.where(row == 0, jnp.broadcast_to(hp, h.shape), pltpu.roll(h, 1, 0))
    xx = shifted - h
    mix = mix_ref[...]
    xm = lambda n: (h + xx * mix[n:n + 1]).astype(BF16)
    xv = xm(2)
    r = _dot(xm(0), wrkv_ref[0])
    k = _dot(xm(1), wrkv_ref[1])
    v = _dot(xv, wrkv_ref[2])
    lora_w = _dot(jnp.tanh(_dot(xm(3), w1_ref[...])).astype(BF16), w2_ref[...])
    z = -(w0_ref[...] + lora_w)
    softplus = jnp.maximum(z, 0.0) + jnp.log(1.0 + jnp.exp(-jnp.abs(z)))
    lw = -jnp.exp(-softplus - 0.5)
    a = jax.nn.sigmoid(a0_ref[...] + _dot(_dot(xm(4), a1_ref[...]).astype(BF16), a2_ref[...]))
    g_o[...] = _dot(jax.nn.sigmoid(_dot(xm(5), g1_ref[...])).astype(BF16), g2_ref[...])
    if has_vres:
        gate_v = jax.nn.sigmoid(v0_ref[...] + _dot(_dot(xv, v1_ref[...]).astype(BF16), v2_ref[...]))
        v = v + (vf_ref[...] - v) * gate_v
    else:
        v_o[...] = v
    kk = k * kk_ref[...]
    kk = kk / jnp.maximum(jnp.sqrt(_group_sum(kk * kk)), 1e-12)
    k = k * (1.0 + (a - 1.0) * ka_ref[...])
    bonus_o[...] = _group_sum(r * k * rk_ref[...]) * v
    C = WKV_CHUNK
    rr = lax.broadcasted_iota(jnp.int32, (tm, tm), 0)
    cc = lax.broadcasted_iota(jnp.int32, (tm, tm), 1)
    tri = (((rr // C) == (cc // C)) & (cc <= rr)).astype(BF16)
    hi = lw.astype(BF16)
    rem = lw - hi.astype(F32)
    mid = rem.astype(BF16)
    lo = (rem - mid.astype(F32)).astype(BF16)
    cl = _dot(tri, hi) + _dot(tri, mid) + _dot(tri, lo)
    ends = [cl[c * C + C - 1:c * C + C] for c in range(tm // C)]
    cl_end = jnp.concatenate([jnp.broadcast_to(e, (C, e.shape[1])) for e in ends], axis=0)
    e_neg = jnp.exp(-cl)
    e_rem = jnp.exp(cl_end - cl)
    at_o[...] = (-kk * jnp.exp(cl - lw)).astype(BF16)
    rt_o[...] = (r * jnp.exp(cl)).astype(BF16)
    bt_o[...] = (kk * a * e_neg).astype(BF16)
    kt_o[...] = (k * e_neg).astype(BF16)
    bh_o[...] = (kk * a * e_rem).astype(BF16)
    kh_o[...] = (k * e_rem).astype(BF16)
    vb_o[...] = v.astype(BF16)
    for c, e in enumerate(ends):
        wc_o[c] = jnp.broadcast_to(jnp.exp(e), wc_o.shape[1:])


def _rwkv_pre(x, g, p, v_first):
    s, d = x.shape
    tm = _row_tile(s, WKV_TILE)
    row = pl.BlockSpec((tm, d), lambda i: (i, 0))
    prev = pl.BlockSpec((HALO, d), lambda i: (jnp.maximum(i * (tm // HALO) - 1, 0), 0))
    names = ["mix", "w_rkv", "w0", "w1", "w2", "a0", "a1", "a2", "g1", "g2", "k_k", "k_a", "r_k"]
    args = [x, x, g] + [p[n] for n in names]
    specs = [row, prev, _full(g.shape)] + [_full(p[n].shape) for n in names]
    has_vres = v_first is not None
    if has_vres:
        args += [v_first, p["v0"], p["v1"], p["v2"]]
        specs += [row] + [_full(p[n].shape) for n in ("v0", "v1", "v2")]
    n_ch = tm // WKV_CHUNK
    wc_spec = pl.BlockSpec((n_ch, SUBLANES, d), lambda i: (i, 0, 0))
    out_specs = [row] * 7 + [wc_spec, row, row]
    out_shape = ([jax.ShapeDtypeStruct((s, d), BF16)] * 7
                 + [jax.ShapeDtypeStruct((s // WKV_CHUNK, SUBLANES, d), F32)]
                 + [jax.ShapeDtypeStruct((s, d), F32)] * 2)
    if not has_vres:
        out_specs.append(row)
        out_shape.append(jax.ShapeDtypeStruct((s, d), F32))
    return pl.pallas_call(
        functools.partial(_rwkv_pre_kernel, has_vres=has_vres),
        grid=(s // tm,),
        in_specs=specs,
        out_specs=out_specs,
        out_shape=out_shape,
        compiler_params=_params("parallel"),
        name="rwkv_pre",
    )(*args)


def _wkv_kernel(at_ref, rt_ref, bt_ref, kt_ref, bh_ref, kh_ref, v_ref, wc_ref, bonus_ref,
                lnw_ref, lnb_ref, o_ref, s_ref, *, group, prec):
    C = WKV_CHUNK
    C2 = 2 * C
    n_pairs = o_ref.shape[1] // LANES
    n_groups = o_ref.shape[0] // (C * group)

    @pl.when(pl.program_id(0) == 0)
    def _():
        s_ref[...] = jnp.zeros(s_ref.shape, F32)

    lane = lax.broadcasted_iota(jnp.int32, (1, LANES), 1)
    head0 = lane < RWKV_HEAD
    m0 = head0.astype(F32)
    m1 = 1.0 - m0
    t2 = lax.broadcasted_iota(jnp.int32, (C2, C2), 0)
    s2 = lax.broadcasted_iota(jnp.int32, (C2, C2), 1)
    same_head = (t2 // C) == (s2 // C)
    strict = (same_head & (s2 < t2)).astype(F32)
    incl = (same_head & (s2 <= t2)).astype(F32)
    eye = (t2 == s2).astype(F32)

    def level_mask(b):
        return (((t2 // (2 * b)) == (s2 // (2 * b))) & ((t2 % (2 * b)) >= b)
                & ((s2 % (2 * b)) < b)).astype(F32)

    strict_m, incl_m = strict, incl
    levels = []
    b_sz = 1
    while b_sz < C:
        levels.append(level_mask(b_sz))
        b_sz *= 2

    def stack(x):
        zero = jnp.zeros_like(x)
        return jnp.concatenate([jnp.where(head0, x, zero), jnp.where(head0, zero, x)], axis=0)

    def twice(x):
        return jnp.concatenate([x, x], axis=0)

    def cast(x):
        return x.astype(prec)

    def group_body(g, carry):
        probs = [(ci, p) for ci in range(group) for p in range(n_pairs)]
        offs = [pl.multiple_of((g * group + ci) * C, C) for ci in range(group)]

        def load(ref, ci, p):
            return cast(ref[pl.ds(offs[ci], C), p * LANES:(p + 1) * LANES])

        a_s = [stack(load(at_ref, ci, p)) for ci, p in probs]
        r_s = [stack(load(rt_ref, ci, p)) for ci, p in probs]
        v_s = [stack(load(v_ref, ci, p)) for ci, p in probs]
        hat = [jnp.concatenate([stack(load(bh_ref, ci, p)), stack(load(kh_ref, ci, p))], axis=0)
               for ci, p in probs]
        til = [jnp.concatenate([twice(load(bt_ref, ci, p)), twice(load(kt_ref, ci, p))], axis=0)
               for ci, p in probs]
        lm = [_dot_nt(jnp.concatenate([a, r], axis=0), t) for a, r, t in zip(a_s, r_s, til)]
        l_ab = [x[:C2, :C2] * strict_m for x in lm]
        l_ak = [cast(x[:C2, C2:] * strict_m) for x in lm]
        m_cat = [cast(jnp.concatenate([x[C2:, :C2] * incl_m, x[C2:, C2:] * incl_m], axis=1))
                 for x in lm]
        t_inv = [eye + l * levels[0] for l in l_ab]
        for lvl in levels[1:]:
            t_b = [cast(t) for t in t_inv]
            z = [_dot(t, cast(l * lvl)) for t, l in zip(t_b, l_ab)]
            t_inv = [t + _dot(cast(zz), tb) for t, zz, tb in zip(t_inv, z, t_b)]
        t_b = [cast(t) for t in t_inv]
        p_loc = [_dot(l, v) for l, v in zip(l_ak, v_s)]
        for ci in range(group):
            rows = pl.ds(offs[ci], C)
            c_idx = g * group + ci
            sel = [ci * n_pairs + p for p in range(n_pairs)]
            st = [s_ref[p] for p in range(n_pairs)]
            st_b = [cast(s) for s in st]
            x = [_dot_nt(a_s[i], sb) + p_loc[i] for i, sb in zip(sel, st_b)]
            u = [cast(_dot(t_b[i], cast(xx))) for i, xx in zip(sel, x)]
            uv = [jnp.concatenate([uu, v_s[i]], axis=0) for i, uu in zip(sel, u)]
            y = [_dot_nt(r_s[i], sb) + _dot(m_cat[i], w) for i, sb, w in zip(sel, st_b, uv)]
            for p in range(n_pairs):
                cols = slice(p * LANES, (p + 1) * LANES)
                w_end = wc_ref[c_idx, 0:1, cols]
                s_ref[p] = st[p] * w_end + _dot_tn(uv[p], hat[sel[p]])
                yp = y[p][:C] + y[p][C:]
                mu = (jnp.sum(yp * m0, axis=-1, keepdims=True) * m0
                      + jnp.sum(yp * m1, axis=-1, keepdims=True) * m1) * (1.0 / RWKV_HEAD)
                yc = yp - mu
                yc2 = yc * yc
                var = (jnp.sum(yc2 * m0, axis=-1, keepdims=True) * m0
                       + jnp.sum(yc2 * m1, axis=-1, keepdims=True) * m1) * (1.0 / RWKV_HEAD)
                yn = yc * lax.rsqrt(var + GN_EPS)
                o_ref[rows, cols] = yn * lnw_ref[:, cols] + lnb_ref[:, cols] + bonus_ref[rows, cols]
        return carry

    lax.fori_loop(0, n_groups, group_body, 0)


def _wkv(at, rt, bt, kt, bh, kh, vb, wc, bonus, ln_w, ln_b, prec=BF16):
    s, d = at.shape
    tt = _row_tile(s, WKV_STEP)
    group = 2
    row = pl.BlockSpec((tt, d), lambda t: (t, 0))
    return pl.pallas_call(
        functools.partial(_wkv_kernel, group=group, prec=prec),
        grid=(s // tt,),
        in_specs=[row] * 7 + [pl.BlockSpec((tt // WKV_CHUNK, SUBLANES, d), lambda t: (t, 0, 0)),
                              row, _full(ln_w.shape), _full(ln_b.shape)],
        out_specs=row,
        out_shape=jax.ShapeDtypeStruct((s, d), F32),
        scratch_shapes=[pltpu.VMEM((d // LANES, LANES, LANES), F32)],
        compiler_params=_params("arbitrary"),
        name="wkv7_scan",
    )(at, rt, bt, kt, bh, kh, vb, wc, bonus, ln_w, ln_b)


def _final_norm_kernel(x_ref, g_ref, o_ref):
    o_ref[...] = _rms(x_ref[...], g_ref[...])


def _final_norm(x, g):
    s, d = x.shape
    tm = _row_tile(s, 1024)
    row = pl.BlockSpec((tm, d), lambda i: (i, 0))
    return pl.pallas_call(
        _final_norm_kernel, grid=(s // tm,), in_specs=[row, _full(g.shape)], out_specs=row,
        out_shape=jax.ShapeDtypeStruct((s, d), F32),
        compiler_params=_params("parallel"), name="final_norm",
    )(x, g)


def _rot_cols(w):
    half = QK_ROPE_DIM // 2
    return jnp.concatenate([-w[..., half:], w[..., :half]], axis=-1)


def _mla_weights(w_in, w_uq, w_ukv):
    d = w_in.shape[0]
    lat = Q_LORA_RANK + KV_LORA_RANK
    w_kr = w_in[:, lat:]
    w_in_e = jnp.concatenate([w_in[:, :lat], w_kr, _rot_cols(w_kr)], axis=1).astype(BF16)
    uq = w_uq.reshape(Q_LORA_RANK, MLA_HEADS, QK_DIM)
    uq_rope = uq[:, :, QK_NOPE_DIM:]
    w_uq_e = jnp.concatenate(
        [uq[:, :, :QK_NOPE_DIM].reshape(Q_LORA_RANK, -1),
         uq_rope.reshape(Q_LORA_RANK, -1),
         _rot_cols(uq_rope).reshape(Q_LORA_RANK, -1)], axis=1).astype(BF16)
    ukv = w_ukv.reshape(KV_LORA_RANK, MLA_HEADS, QK_NOPE_DIM + V_HEAD_DIM)
    w_ukv_p = jnp.concatenate(
        [ukv[:, :, :QK_NOPE_DIM].reshape(KV_LORA_RANK, -1),
         ukv[:, :, QK_NOPE_DIM:].reshape(KV_LORA_RANK, -1)], axis=1).astype(BF16)
    del d
    return w_in_e, w_uq_e, w_ukv_p


def _mla_layer(x, g, cos_t, sin_t, w_in, q_norm, w_uq, kv_norm, w_ukv, w_o):
    w_in_e, w_uq_e, w_ukv_p = _mla_weights(w_in, w_uq, w_ukv)
    q_cat, k_cat, v = _mla_pre(x, g[None], w_in_e, q_norm[None], w_uq_e, kv_norm[None],
                               w_ukv_p, cos_t, sin_t)
    o = _attention(q_cat, k_cat, v)
    return _proj_residual(x, o, w_o.astype(BF16))


def _rwkv_layer(x, g, v_first, p, wkv_prec=BF16):
    row = lambda t: t.reshape(1, -1)
    pp = {
        "mix": p["mix"], "w_rkv": p["w_rkv"].astype(BF16),
        "w0": row(p["w0"]), "w1": p["w1"].astype(BF16), "w2": p["w2"].astype(BF16),
        "a0": row(p["a0"]), "a1": p["a1"].astype(BF16), "a2": p["a2"].astype(BF16),
        "g1": p["g1"].astype(BF16), "g2": p["g2"].astype(BF16),
        "k_k": row(p["k_k"]), "k_a": row(p["k_a"]), "r_k": row(p["r_k"]),
    }
    if v_first is not None:
        pp.update(v0=row(p["v0"]), v1=p["v1"].astype(BF16), v2=p["v2"].astype(BF16))
    outs = _rwkv_pre(x, g[None], pp, v_first)
    scan_in, gate = outs[:9], outs[9]
    z = _wkv(*scan_in, row(p["ln_w"]), row(p["ln_b"]), prec=wkv_prec)
    out = _proj_residual(x, z, p["w_o"].astype(BF16), gate=gate)
    return out, (outs[10] if v_first is None else v_first)


def _ffn_layer(x, g, w_in, conv_w, conv_b, w_out):
    return _ffn(x, g[None], w_in.astype(BF16), conv_w, conv_b[None], w_out.astype(BF16))


def kernel(x, positions, norm_mix, norm_ffn, norm_final, mla_w_in, mla_q_norm, mla_w_uq, mla_kv_norm, mla_w_ukv, mla_w_o, rwkv_mix, rwkv_w_rkv, rwkv_w0, rwkv_w1, rwkv_w2, rwkv_a0, rwkv_a1, rwkv_a2, rwkv_v0, rwkv_v1, rwkv_v2, rwkv_g1, rwkv_g2, rwkv_k_k, rwkv_k_a, rwkv_r_k, rwkv_ln_w, rwkv_ln_b, rwkv_w_o, ffn_w_in, ffn_conv_w, ffn_conv_b, ffn_w_out):
    batch, seq, d = x.shape
    assert batch == 1
    depth = norm_mix.shape[0]
    half = QK_ROPE_DIM // 2
    inv_freq = ROPE_THETA ** (-jnp.arange(half, dtype=F32) / half)
    freq_row = jnp.tile(inv_freq, LANES // half)[None]
    cos_t, sin_t = _rope_tables(positions.astype(F32).reshape(seq, 1), freq_row)
    xs = x.reshape(seq, d)
    v_first = None
    for i in range(depth):
        j = i // 2
        if i % 2 == 0:
            xs = _mla_layer(xs, norm_mix[i], cos_t, sin_t, mla_w_in[j], mla_q_norm[j], mla_w_uq[j],
                            mla_kv_norm[j], mla_w_ukv[j], mla_w_o[j])
        else:
            p = {"mix": rwkv_mix[j], "w_rkv": rwkv_w_rkv[j], "w0": rwkv_w0[j], "w1": rwkv_w1[j],
                 "w2": rwkv_w2[j], "a0": rwkv_a0[j], "a1": rwkv_a1[j], "a2": rwkv_a2[j],
                 "g1": rwkv_g1[j], "g2": rwkv_g2[j], "k_k": rwkv_k_k[j], "k_a": rwkv_k_a[j],
                 "r_k": rwkv_r_k[j], "ln_w": rwkv_ln_w[j], "ln_b": rwkv_ln_b[j], "w_o": rwkv_w_o[j]}
            if j > 0:
                p.update(v0=rwkv_v0[j - 1], v1=rwkv_v1[j - 1], v2=rwkv_v2[j - 1])
            xs, v_first = _rwkv_layer(xs, norm_mix[i], v_first, p)
        xs = _ffn_layer(xs, norm_ffn[i], ffn_w_in[i], ffn_conv_w[i], ffn_conv_b[i], ffn_w_out[i])
    return _final_norm(xs, norm_final[None]).reshape(batch, seq, d)
```

```python
import functools
import math

import jax
import jax.numpy as jnp
from jax import lax
from jax.experimental import pallas as pl
from jax.experimental.pallas import tpu as pltpu

NORM_EPS = 1e-6
GN_EPS = 64e-5
MLA_HEADS = 8
QK_NOPE_DIM = 128
QK_ROPE_DIM = 64
QK_DIM = QK_NOPE_DIM + QK_ROPE_DIM
QK_PAD = 256
V_HEAD_DIM = 128
Q_LORA_RANK = 384
KV_LORA_RANK = 256
ROPE_THETA = 10000.0
RWKV_HEAD = 64
LANES = 128
D_FF = 2816
HALO = 16
WKV_CHUNK = 64
WKV_TILE = 256
WKV_STEP = 512
SUBLANES = 8
ATTN_TK = 512

BF16 = jnp.bfloat16
F32 = jnp.float32
VMEM_LIMIT = 56 * 1024 * 1024


def _params(*sem):
    return pltpu.CompilerParams(dimension_semantics=sem, vmem_limit_bytes=VMEM_LIMIT)


def _dot(a, b, precision=None):
    return jnp.dot(a, b, preferred_element_type=F32, precision=precision)


def _dot_nt(a, b, precision=None):
    return lax.dot_general(a, b, (((1,), (1,)), ((), ())),
                           preferred_element_type=F32, precision=precision)


def _dot_tn(a, b, precision=None):
    return lax.dot_general(a, b, (((0,), (0,)), ((), ())),
                           preferred_element_type=F32, precision=precision)


def _rms(x, g):
    return x * lax.rsqrt(jnp.mean(x * x, axis=-1, keepdims=True) + NORM_EPS) * g


def _row_tile(s, want):
    return min(want, s)


def _full(shape):
    return pl.BlockSpec(shape, lambda *_: (0,) * len(shape))


def _rope_table_kernel(pos_ref, freq_ref, cos_ref, sin_ref):
    ang = pos_ref[...] * freq_ref[...]
    cos_ref[...] = jnp.cos(ang)
    sin_ref[...] = jnp.sin(ang)


def _rope_tables(pos_col, freq_row):
    s = pos_col.shape[0]
    tm = _row_tile(s, 1024)
    return pl.pallas_call(
        _rope_table_kernel,
        grid=(s // tm,),
        in_specs=[pl.BlockSpec((tm, 1), lambda i: (i, 0)), _full((1, LANES))],
        out_specs=[pl.BlockSpec((tm, LANES), lambda i: (i, 0))] * 2,
        out_shape=[jax.ShapeDtypeStruct((s, LANES), F32)] * 2,
        compiler_params=_params("parallel"),
        name="rope_tables",
    )(pos_col, freq_row)


def _mla_pre_kernel(x_ref, g_ref, w_in_ref, qn_ref, w_uq_ref, kvn_ref, w_ukv_ref,
                    cos_ref, sin_ref, q_ref, k_ref, v_ref):
    h = _rms(x_ref[...], g_ref[...]).astype(BF16)
    lat = _dot(h, w_in_ref[...])
    q_lat = lat[:, :Q_LORA_RANK]
    c_kv = lat[:, Q_LORA_RANK:Q_LORA_RANK + KV_LORA_RANK]
    kr = lat[:, Q_LORA_RANK + KV_LORA_RANK:]
    cos = cos_ref[...]
    sin = sin_ref[...]
    k_rope = (kr[:, :QK_ROPE_DIM] * cos[:, :QK_ROPE_DIM]
              + kr[:, QK_ROPE_DIM:] * sin[:, :QK_ROPE_DIM])
    qn = _rms(q_lat, qn_ref[...]).astype(BF16)
    q_all = _dot(qn, w_uq_ref[...])
    n_nope = MLA_HEADS * QK_NOPE_DIM
    n_rope = MLA_HEADS * QK_ROPE_DIM
    reps = n_rope // LANES
    qscale = QK_DIM ** -0.5 * math.log2(math.e)
    q_rope = (q_all[:, n_nope:n_nope + n_rope] * jnp.tile(cos, (1, reps))
              + q_all[:, n_nope + n_rope:] * jnp.tile(sin, (1, reps))) * qscale
    kvn = _rms(c_kv, kvn_ref[...]).astype(BF16)
    kv = _dot(kvn, w_ukv_ref[...])
    tm = x_ref.shape[0]
    qn_t = (q_all[:, :n_nope] * qscale).T.astype(BF16)
    qr_t = q_rope.T.astype(BF16)
    v_t = kv[:, n_nope:].T.astype(BF16)
    zpad_t = jnp.zeros((QK_PAD - QK_DIM, tm), BF16)
    zpad = jnp.zeros((tm, QK_PAD - QK_DIM), BF16)
    k_rope_b = k_rope.astype(BF16)
    for hd in range(MLA_HEADS):
        q_ref[hd, 0:QK_NOPE_DIM, :] = qn_t[hd * QK_NOPE_DIM:(hd + 1) * QK_NOPE_DIM]
        q_ref[hd, QK_NOPE_DIM:QK_DIM, :] = qr_t[hd * QK_ROPE_DIM:(hd + 1) * QK_ROPE_DIM]
        q_ref[hd, QK_DIM:QK_PAD, :] = zpad_t
        k_ref[hd, :, 0:QK_NOPE_DIM] = kv[:, hd * QK_NOPE_DIM:(hd + 1) * QK_NOPE_DIM].astype(BF16)
        k_ref[hd, :, QK_NOPE_DIM:QK_DIM] = k_rope_b
        k_ref[hd, :, QK_DIM:QK_PAD] = zpad
        v_ref[hd, 0] = v_t[hd * V_HEAD_DIM:(hd + 1) * V_HEAD_DIM]


def _mla_pre(x, g, w_in_e, q_norm, w_uq_e, kv_norm, w_ukv_p, cos_t, sin_t):
    s, d = x.shape
    tm = _row_tile(s, ATTN_TK)
    row = lambda w: pl.BlockSpec((tm, w), lambda i: (i, 0))
    return pl.pallas_call(
        _mla_pre_kernel,
        grid=(s // tm,),
        in_specs=[row(d), _full(g.shape), _full(w_in_e.shape), _full(q_norm.shape),
                  _full(w_uq_e.shape), _full(kv_norm.shape), _full(w_ukv_p.shape),
                  row(LANES), row(LANES)],
        out_specs=[pl.BlockSpec((MLA_HEADS, QK_PAD, tm), lambda i: (0, 0, i)),
                   pl.BlockSpec((MLA_HEADS, tm, QK_PAD), lambda i: (0, i, 0)),
                   pl.BlockSpec((MLA_HEADS, 1, V_HEAD_DIM, tm), lambda i: (0, i, 0, 0))],
        out_shape=[jax.ShapeDtypeStruct((MLA_HEADS, QK_PAD, s), BF16),
                   jax.ShapeDtypeStruct((MLA_HEADS, s, QK_PAD), BF16),
                   jax.ShapeDtypeStruct((MLA_HEADS, s // tm, V_HEAD_DIM, tm), BF16)],
        compiler_params=_params("parallel"),
        name="mla_pre",
    )(x, g, w_in_e, q_norm, w_uq_e, kv_norm, w_ukv_p, cos_t, sin_t)


def _attn_kernel(q_ref, k_ref, v_ref, o_ref, acc_a, acc_b, s_a, s_b, *, tk):
    qi = pl.program_id(1)
    accs = (acc_a, acc_b)
    for acc in accs:
        acc[...] = jnp.zeros(acc.shape, F32)

    def scores(half, j, s_ref):
        q_t = q_ref[:, half * tk:(half + 1) * tk]
        off = pl.multiple_of(j * tk, tk)
        s = _dot(k_ref[pl.ds(off, tk), :], q_t)
        s_ref[...] = s
        return jnp.max(s, axis=0, keepdims=True)

    def absorb(half, j, s_ref, m_blk, m_old, l_old, masked):
        if masked:
            s = s_ref[...]
            key = lax.broadcasted_iota(jnp.int32, s.shape, 0)
            qry = lax.broadcasted_iota(jnp.int32, s.shape, 1)
            s = jnp.where(key <= qry, s, jnp.finfo(F32).min)
            m_blk = jnp.max(s, axis=0, keepdims=True)
        else:
            s = s_ref[...]
        m_new = jnp.maximum(m_old, m_blk)
        p = jnp.exp2(s - m_new)
        alpha = jnp.exp2(m_old - m_new)
        l_new = alpha * l_old + jnp.sum(p, axis=0, keepdims=True)
        acc = accs[half]
        acc[...] = alpha * acc[...] + _dot(v_ref[j], p.astype(BF16))
        return m_new, l_new

    m0 = jnp.full((1, tk), -1e30, F32)
    l0 = jnp.zeros((1, tk), F32)
    n_full = 2 * qi

    def body(j, carry):
        ma, la, mb, lb, blk_a = carry
        blk_b = scores(1, j, s_b)
        ma, la = absorb(0, j, s_a, blk_a, ma, la, False)
        blk_a = scores(0, j + 1, s_a)
        mb, lb = absorb(1, j, s_b, blk_b, mb, lb, False)
        return ma, la, mb, lb, blk_a

    blk_a = scores(0, 0, s_a)
    ma, la, mb, lb, _ = lax.fori_loop(0, n_full, body, (m0, l0, m0, l0, blk_a))
    blk_b = scores(1, n_full, s_b)
    ma, la = absorb(0, n_full, s_a, None, ma, la, True)
    scores(1, n_full + 1, s_a)
    mb, lb = absorb(1, n_full, s_b, blk_b, mb, lb, False)
    mb, lb = absorb(1, n_full + 1, s_a, None, mb, lb, True)
    o_ref[0:tk, :] = (acc_a[...] / la).T.astype(o_ref.dtype)
    o_ref[tk:2 * tk, :] = (acc_b[...] / lb).T.astype(o_ref.dtype)


def _attention(q_t, k_cat, v_t):
    nh, s, _ = k_cat.shape
    tk = v_t.shape[-1]
    tq = 2 * tk
    return pl.pallas_call(
        functools.partial(_attn_kernel, tk=tk),
        grid=(nh, s // tq),
        in_specs=[pl.BlockSpec((None, QK_PAD, tq), lambda h, i: (h, 0, i)),
                  pl.BlockSpec((None, s, QK_PAD), lambda h, i: (h, 0, 0)),
                  pl.BlockSpec((None, s // tk, V_HEAD_DIM, tk), lambda h, i: (h, 0, 0, 0))],
        out_specs=pl.BlockSpec((tq, V_HEAD_DIM), lambda h, i: (i, h)),
        out_shape=jax.ShapeDtypeStruct((s, nh * V_HEAD_DIM), BF16),
        scratch_shapes=[pltpu.VMEM((V_HEAD_DIM, tk), F32), pltpu.VMEM((V_HEAD_DIM, tk), F32),
                        pltpu.VMEM((tk, tk), F32), pltpu.VMEM((tk, tk), F32)],
        compiler_params=_params("parallel", "arbitrary"),
        name="mla_attention",
    )(q_t, k_cat, v_t)


def _proj_kernel(x_ref, z_ref, w_ref, o_ref):
    o_ref[...] = x_ref[...] + _dot(z_ref[...].astype(BF16), w_ref[...])


def _proj_gated_kernel(x_ref, z_ref, g_ref, w_ref, o_ref):
    o_ref[...] = x_ref[...] + _dot((z_ref[...] * g_ref[...]).astype(BF16), w_ref[...])


def _proj_residual(x, z, w, gate=None):
    s, d = x.shape
    tm = _row_tile(s, 512)
    row = pl.BlockSpec((tm, d), lambda i: (i, 0))
    zrow = pl.BlockSpec((tm, z.shape[1]), lambda i: (i, 0))
    if gate is None:
        kern, args, specs = _proj_kernel, (x, z, w), [row, zrow, _full(w.shape)]
    else:
        kern, args, specs = _proj_gated_kernel, (x, z, gate, w), [row, zrow, zrow, _full(w.shape)]
    return pl.pallas_call(
        kern, grid=(s // tm,), in_specs=specs, out_specs=row,
        out_shape=jax.ShapeDtypeStruct((s, d), F32),
        compiler_params=_params("parallel"), name="proj_residual",
    )(*args)


def _ffn_kernel(x_ref, xp_ref, g_ref, wg_ref, wu_ref, cw_ref, cb_ref, wo_ref, o_ref, hn_ref):
    i = pl.program_id(0)
    c = pl.program_id(1)
    tm = x_ref.shape[0]

    @pl.when(c == 0)
    def _():
        hn_ref[HALO:, :] = _rms(x_ref[...], g_ref[...]).astype(BF16)
        keep = (i > 0).astype(F32)
        hn_ref[:HALO, :] = (_rms(xp_ref[...], g_ref[...]) * keep).astype(BF16)

    hn = hn_ref[...]
    gate = _dot(hn, wg_ref[...])
    up = _dot(hn[HALO:], wu_ref[...])
    cw = cw_ref[...]
    g0 = gate[HALO:]
    g1 = pltpu.roll(gate, 1, 0)[HALO:]
    g2 = pltpu.roll(gate, 2, 0)[HALO:]
    conv = g0 * cw[2:3] + g1 * cw[1:2] + g2 * cw[0:1] + cb_ref[...]
    act = (conv * jax.nn.sigmoid(conv) * up).astype(BF16)
    contrib = _dot(act, wo_ref[...])

    @pl.when(c == 0)
    def _():
        o_ref[...] = x_ref[...] + contrib

    @pl.when(c != 0)
    def _():
        o_ref[...] += contrib


def _ffn(x, g, w_in, conv_w, conv_b, w_out):
    s, d = x.shape
    tm = _row_tile(s, 512)
    nc = 2
    tf = D_FF // nc
    return pl.pallas_call(
        _ffn_kernel,
        grid=(s // tm, nc),
        in_specs=[pl.BlockSpec((tm, d), lambda i, c: (i, 0)),
                  pl.BlockSpec((HALO, d), lambda i, c: (jnp.maximum(i * (tm // HALO) - 1, 0), 0)),
                  _full(g.shape),
                  pl.BlockSpec((d, tf), lambda i, c: (0, c)),
                  pl.BlockSpec((d, tf), lambda i, c: (0, c + nc)),
                  pl.BlockSpec((3, tf), lambda i, c: (0, c)),
                  pl.BlockSpec((1, tf), lambda i, c: (0, c)),
                  pl.BlockSpec((tf, d), lambda i, c: (c, 0))],
        out_specs=pl.BlockSpec((tm, d), lambda i, c: (i, 0)),
        out_shape=jax.ShapeDtypeStruct((s, d), F32),
        scratch_shapes=[pltpu.VMEM((HALO + tm, d), BF16)],
        compiler_params=_params("parallel", "arbitrary"),
        name="conv_glu",
    )(x, x, g, w_in, w_in, conv_w, conv_b, w_out)


def _group_sum(x):
    r = lax.broadcasted_iota(jnp.int32, (LANES, LANES), 0) // RWKV_HEAD
    c = lax.broadcasted_iota(jnp.int32, (LANES, LANES), 1) // RWKV_HEAD
    ones_bd = (r == c).astype(F32)
    cols = [_dot(x[:, j:j + LANES], ones_bd, precision=lax.Precision.HIGHEST)
            for j in range(0, x.shape[1], LANES)]
    return cols[0] if len(cols) == 1 else jnp.concatenate(cols, axis=1)


def _rwkv_pre_kernel(*refs, has_vres):
    (x_ref, xp_ref, g_ref, mix_ref, wrkv_ref, w0_ref, w1_ref, w2_ref, a0_ref, a1_ref, a2_ref,
     g1_ref, g2_ref, kk_ref, ka_ref, rk_ref) = refs[:16]
    if has_vres:
        vf_ref, v0_ref, v1_ref, v2_ref = refs[16:20]
        at_o, rt_o, bt_o, kt_o, bh_o, kh_o, vb_o, wc_o, bonus_o, g_o = refs[20:]
    else:
        at_o, rt_o, bt_o, kt_o, bh_o, kh_o, vb_o, wc_o, bonus_o, g_o, v_o = refs[16:]
    i = pl.program_id(0)
    tm = x_ref.shape[0]
    h = _rms(x_ref[...], g_ref[...])
    hp = _rms(xp_ref[...], g_ref[...])[HALO - 1:HALO] * (i > 0).astype(F32)
    row = lax.broadcasted_iota(jnp.int32, h.shape, 0)
    shifted = jnp.where(row == 0, jnp.broadcast_to(hp, h.shape), pltpu.roll(h, 1, 0))
    xx = shifted - h
    mix = mix_ref[...]
    xm = lambda n: (h + xx * mix[n:n + 1]).astype(BF16)
    xv = xm(2)
    r = _dot(xm(0), wrkv_ref[0])
    k = _dot(xm(1), wrkv_ref[1])
    v = _dot(xv, wrkv_ref[2])
    lora_w = _dot(jnp.tanh(_dot(xm(3), w1_ref[...])).astype(BF16), w2_ref[...])
    z = -(w0_ref[...] + lora_w)
    softplus = jnp.maximum(z, 0.0) + jnp.log(1.0 + jnp.exp(-jnp.abs(z)))
    lw = -jnp.exp(-softplus - 0.5)
    a = jax.nn.sigmoid(a0_ref[...] + _dot(_dot(xm(4), a1_ref[...]).astype(BF16), a2_ref[...]))
    g_o[...] = _dot(jax.nn.sigmoid(_dot(xm(5), g1_ref[...])).astype(BF16), g2_ref[...])
    if has_vres:
        gate_v = jax.nn.sigmoid(v0_ref[...] + _dot(_dot(xv, v1_ref[...]).astype(BF16), v2_ref[...]))
        v = v + (vf_ref[...] - v) * gate_v
    else:
        v_o[...] = v
    kk = k * kk_ref[...]
    kk = kk / jnp.maximum(jnp.sqrt(_group_sum(kk * kk)), 1e-12)
    k = k * (1.0 + (a - 1.0) * ka_ref[...])
    bonus_o[...] = _group_sum(r * k * rk_ref[...]) * v
    C = WKV_CHUNK
    rr = lax.broadcasted_iota(jnp.int32, (tm, tm), 0)
    cc = lax.broadcasted_iota(jnp.int32, (tm, tm), 1)
    tri = (((rr // C) == (cc // C)) & (cc <= rr)).astype(BF16)
    hi = lw.astype(BF16)
    rem = lw - hi.astype(F32)
    mid = rem.astype(BF16)
    lo = (rem - mid.astype(F32)).astype(BF16)
    cl = _dot(tri, hi) + _dot(tri, mid) + _dot(tri, lo)
    ends = [cl[c * C + C - 1:c * C + C] for c in range(tm // C)]
    cl_end = jnp.concatenate([jnp.broadcast_to(e, (C, e.shape[1])) for e in ends], axis=0)
    e_neg = jnp.exp(-cl)
    e_rem = jnp.exp(cl_end - cl)
    at_o[...] = (-kk * jnp.exp(cl - lw)).astype(BF16)
    rt_o[...] = (r * jnp.exp(cl)).astype(BF16)
    bt_o[...] = (kk * a * e_neg).astype(BF16)
    kt_o[...] = (k * e_neg).astype(BF16)
    bh_o[...] = (kk * a * e_rem).astype(BF16)
    kh_o[...] = (k * e_rem).astype(BF16)
    vb_o[...] = v.astype(BF16)
    for c, e in enumerate(ends):
        wc_o[c] = jnp.broadcast_to(jnp.exp(e), wc_o.shape[1:])


def _rwkv_pre(x, g, p, v_first):
    s, d = x.shape
    tm = _row_tile(s, WKV_TILE)
    row = pl.BlockSpec((tm, d), lambda i: (i, 0))
    prev = pl.BlockSpec((HALO, d), lambda i: (jnp.maximum(i * (tm // HALO) - 1, 0), 0))
    names = ["mix", "w_rkv", "w0", "w1", "w2", "a0", "a1", "a2", "g1", "g2", "k_k", "k_a", "r_k"]
    args = [x, x, g] + [p[n] for n in names]
    specs = [row, prev, _full(g.shape)] + [_full(p[n].shape) for n in names]
    has_vres = v_first is not None
    if has_vres:
        args += [v_first, p["v0"], p["v1"], p["v2"]]
        specs += [row] + [_full(p[n].shape) for n in ("v0", "v1", "v2")]
    n_ch = tm // WKV_CHUNK
    wc_spec = pl.BlockSpec((n_ch, SUBLANES, d), lambda i: (i, 0, 0))
    out_specs = [row] * 7 + [wc_spec, row, row]
    out_shape = ([jax.ShapeDtypeStruct((s, d), BF16)] * 7
                 + [jax.ShapeDtypeStruct((s // WKV_CHUNK, SUBLANES, d), F32)]
                 + [jax.ShapeDtypeStruct((s, d), F32)] * 2)
    if not has_vres:
        out_specs.append(row)
        out_shape.append(jax.ShapeDtypeStruct((s, d), F32))
    return pl.pallas_call(
        functools.partial(_rwkv_pre_kernel, has_vres=has_vres),
        grid=(s // tm,),
        in_specs=specs,
        out_specs=out_specs,
        out_shape=out_shape,
        compiler_params=_params("parallel"),
        name="rwkv_pre",
    )(*args)


def _wkv_kernel(at_ref, rt_ref, bt_ref, kt_ref, bh_ref, kh_ref, v_ref, wc_ref, bonus_ref,
                lnw_ref, lnb_ref, o_ref, s_ref, *, group, prec):
    C = WKV_CHUNK
    C2 = 2 * C
    n_pairs = o_ref.shape[1] // LANES
    n_groups = o_ref.shape[0] // (C * group)

    @pl.when(pl.program_id(0) == 0)
    def _():
        s_ref[...] = jnp.zeros(s_ref.shape, F32)

    lane = lax.broadcasted_iota(jnp.int32, (1, LANES), 1)
    head0 = lane < RWKV_HEAD
    m0 = head0.astype(F32)
    m1 = 1.0 - m0
    t2 = lax.broadcasted_iota(jnp.int32, (C2, C2), 0)
    s2 = lax.broadcasted_iota(jnp.int32, (C2, C2), 1)
    same_head = (t2 // C) == (s2 // C)
    strict = (same_head & (s2 < t2)).astype(F32)
    incl = (same_head & (s2 <= t2)).astype(F32)
    eye = (t2 == s2).astype(F32)

    def level_mask(b):
        return (((t2 // (2 * b)) == (s2 // (2 * b))) & ((t2 % (2 * b)) >= b)
                & ((s2 % (2 * b)) < b)).astype(F32)

    strict_m, incl_m = strict, incl
    levels = []
    b_sz = 1
    while b_sz < C:
        levels.append(level_mask(b_sz))
        b_sz *= 2

    def stack(x):
        zero = jnp.zeros_like(x)
        return jnp.concatenate([jnp.where(head0, x, zero), jnp.where(head0, zero, x)], axis=0)

    def twice(x):
        return jnp.concatenate([x, x], axis=0)

    def cast(x):
        return x.astype(prec)

    def group_body(g, carry):
        probs = [(ci, p) for ci in range(group) for p in range(n_pairs)]
        offs = [pl.multiple_of((g * group + ci) * C, C) for ci in range(group)]

        def load(ref, ci, p):
            return cast(ref[pl.ds(offs[ci], C), p * LANES:(p + 1) * LANES])

        a_s = [stack(load(at_ref, ci, p)) for ci, p in probs]
        r_s = [stack(load(rt_ref, ci, p)) for ci, p in probs]
        v_s = [stack(load(v_ref, ci, p)) for ci, p in probs]
        hat = [jnp.concatenate([stack(load(bh_ref, ci, p)), stack(load(kh_ref, ci, p))], axis=0)
               for ci, p in probs]
        til = [jnp.concatenate([twice(load(bt_ref, ci, p)), twice(load(kt_ref, ci, p))], axis=0)
               for ci, p in probs]
        lm = [_dot_nt(jnp.concatenate([a, r], axis=0), t) for a, r, t in zip(a_s, r_s, til)]
        l_ab = [x[:C2, :C2] * strict_m for x in lm]
        l_ak = [cast(x[:C2, C2:] * strict_m) for x in lm]
        m_cat = [cast(jnp.concatenate([x[C2:, :C2] * incl_m, x[C2:, C2:] * incl_m], axis=1))
                 for x in lm]
        t_inv = [eye + l * levels[0] for l in l_ab]
        for lvl in levels[1:]:
            t_b = [cast(t) for t in t_inv]
            z = [_dot(t, cast(l * lvl)) for t, l in zip(t_b, l_ab)]
            t_inv = [t + _dot(cast(zz), tb) for t, zz, tb in zip(t_inv, z, t_b)]
        t_b = [cast(t) for t in t_inv]
        p_loc = [_dot(l, v) for l, v in zip(l_ak, v_s)]
        for ci in range(group):
            rows = pl.ds(offs[ci], C)
            c_idx = g * group + ci
            sel = [ci * n_pairs + p for p in range(n_pairs)]
            st = [s_ref[p] for p in range(n_pairs)]
            st_b = [cast(s) for s in st]
            x = [_dot_nt(a_s[i], sb) + p_loc[i] for i, sb in zip(sel, st_b)]
            u = [cast(_dot(t_b[i], cast(xx))) for i, xx in zip(sel, x)]
            uv = [jnp.concatenate([uu, v_s[i]], axis=0) for i, uu in zip(sel, u)]
            y = [_dot_nt(r_s[i], sb) + _dot(m_cat[i], w) for i, sb, w in zip(sel, st_b, uv)]
            for p in range(n_pairs):
                cols = slice(p * LANES, (p + 1) * LANES)
                w_end = wc_ref[c_idx, 0:1, cols]
                s_ref[p] = st[p] * w_end + _dot_tn(uv[p], hat[sel[p]])
                yp = y[p][:C] + y[p][C:]
                mu = (jnp.sum(yp * m0, axis=-1, keepdims=True) * m0
                      + jnp.sum(yp * m1, axis=-1, keepdims=True) * m1) * (1.0 / RWKV_HEAD)
                yc = yp - mu
                yc2 = yc * yc
                var = (jnp.sum(yc2 * m0, axis=-1, keepdims=True) * m0
                       + jnp.sum(yc2 * m1, axis=-1, keepdims=True) * m1) * (1.0 / RWKV_HEAD)
                yn = yc * lax.rsqrt(var + GN_EPS)
                o_ref[rows, cols] = yn * lnw_ref[:, cols] + lnb_ref[:, cols] + bonus_ref[rows, cols]
        return carry

    lax.fori_loop(0, n_groups, group_body, 0)


def _wkv(at, rt, bt, kt, bh, kh, vb, wc, bonus, ln_w, ln_b, prec=BF16):
    s, d = at.shape
    tt = _row_tile(s, WKV_STEP)
    group = 2
    row = pl.BlockSpec((tt, d), lambda t: (t, 0))
    return pl.pallas_call(
        functools.partial(_wkv_kernel, group=group, prec=prec),
        grid=(s // tt,),
        in_specs=[row] * 7 + [pl.BlockSpec((tt // WKV_CHUNK, SUBLANES, d), lambda t: (t, 0, 0)),
                              row, _full(ln_w.shape), _full(ln_b.shape)],
        out_specs=row,
        out_shape=jax.ShapeDtypeStruct((s, d), F32),
        scratch_shapes=[pltpu.VMEM((d // LANES, LANES, LANES), F32)],
        compiler_params=_params("arbitrary"),
        name="wkv7_scan",
    )(at, rt, bt, kt, bh, kh, vb, wc, bonus, ln_w, ln_b)


def _final_norm_kernel(x_ref, g_ref, o_ref):
    o_ref[...] = _rms(x_ref[...], g_ref[...])


def _final_norm(x, g):
    s, d = x.shape
    tm = _row_tile(s, 1024)
    row = pl.BlockSpec((tm, d), lambda i: (i, 0))
    return pl.pallas_call(
        _final_norm_kernel, grid=(s // tm,), in_specs=[row, _full(g.shape)], out_specs=row,
        out_shape=jax.ShapeDtypeStruct((s, d), F32),
        compiler_params=_params("parallel"), name="final_norm",
    )(x, g)


def _rot_cols(w):
    half = QK_ROPE_DIM // 2
    return jnp.concatenate([-w[..., half:], w[..., :half]], axis=-1)


def _mla_weights(w_in, w_uq, w_ukv):
    d = w_in.shape[0]
    lat = Q_LORA_RANK + KV_LORA_RANK
    w_kr = w_in[:, lat:]
    w_in_e = jnp.concatenate([w_in[:, :lat], w_kr, _rot_cols(w_kr)], axis=1).astype(BF16)
    uq = w_uq.reshape(Q_LORA_RANK, MLA_HEADS, QK_DIM)
    uq_rope = uq[:, :, QK_NOPE_DIM:]
    w_uq_e = jnp.concatenate(
        [uq[:, :, :QK_NOPE_DIM].reshape(Q_LORA_RANK, -1),
         uq_rope.reshape(Q_LORA_RANK, -1),
         _rot_cols(uq_rope).reshape(Q_LORA_RANK, -1)], axis=1).astype(BF16)
    ukv = w_ukv.reshape(KV_LORA_RANK, MLA_HEADS, QK_NOPE_DIM + V_HEAD_DIM)
    w_ukv_p = jnp.concatenate(
        [ukv[:, :, :QK_NOPE_DIM].reshape(KV_LORA_RANK, -1),
         ukv[:, :, QK_NOPE_DIM:].reshape(KV_LORA_RANK, -1)], axis=1).astype(BF16)
    del d
    return w_in_e, w_uq_e, w_ukv_p


def _mla_layer(x, g, cos_t, sin_t, w_in, q_norm, w_uq, kv_norm, w_ukv, w_o):
    w_in_e, w_uq_e, w_ukv_p = _mla_weights(w_in, w_uq, w_ukv)
    q_cat, k_cat, v = _mla_pre(x, g[None], w_in_e, q_norm[None], w_uq_e, kv_norm[None],
                               w_ukv_p, cos_t, sin_t)
    o = _attention(q_cat, k_cat, v)
    return _proj_residual(x, o, w_o.astype(BF16))


def _rwkv_layer(x, g, v_first, p, wkv_prec=BF16):
    row = lambda t: t.reshape(1, -1)
    pp = {
        "mix": p["mix"], "w_rkv": p["w_rkv"].astype(BF16),
        "w0": row(p["w0"]), "w1": p["w1"].astype(BF16), "w2": p["w2"].astype(BF16),
        "a0": row(p["a0"]), "a1": p["a1"].astype(BF16), "a2": p["a2"].astype(BF16),
        "g1": p["g1"].astype(BF16), "g2": p["g2"].astype(BF16),
        "k_k": row(p["k_k"]), "k_a": row(p["k_a"]), "r_k": row(p["r_k"]),
    }
    if v_first is not None:
        pp.update(v0=row(p["v0"]), v1=p["v1"].astype(BF16), v2=p["v2"].astype(BF16))
    outs = _rwkv_pre(x, g[None], pp, v_first)
    scan_in, gate = outs[:9], outs[9]
    z = _wkv(*scan_in, row(p["ln_w"]), row(p["ln_b"]), prec=wkv_prec)
    out = _proj_residual(x, z, p["w_o"].astype(BF16), gate=gate)
    return out, (outs[10] if v_first is None else v_first)


def _ffn_layer(x, g, w_in, conv_w, conv_b, w_out):
    return _ffn(x, g[None], w_in.astype(BF16), conv_w, conv_b[None], w_out.astype(BF16))


def kernel(x, positions, norm_mix, norm_ffn, norm_final, mla_w_in, mla_q_norm, mla_w_uq, mla_kv_norm, mla_w_ukv, mla_w_o, rwkv_mix, rwkv_w_rkv, rwkv_w0, rwkv_w1, rwkv_w2, rwkv_a0, rwkv_a1, rwkv_a2, rwkv_v0, rwkv_v1, rwkv_v2, rwkv_g1, rwkv_g2, rwkv_k_k, rwkv_k_a, rwkv_r_k, rwkv_ln_w, rwkv_ln_b, rwkv_w_o, ffn_w_in, ffn_conv_w, ffn_conv_b, ffn_w_out):
    batch, seq, d = x.shape
    assert batch == 1
    depth = norm_mix.shape[0]
    half = QK_ROPE_DIM // 2
    inv_freq = ROPE_THETA ** (-jnp.arange(half, dtype=F32) / half)
    freq_row = jnp.tile(inv_freq, LANES // half)[None]
    cos_t, sin_t = _rope_tables(positions.astype(F32).reshape(seq, 1), freq_row)
    xs = x.reshape(seq, d)
    v_first = None
    for i in range(depth):
        j = i // 2
        if i % 2 == 0:
            xs = _mla_layer(xs, norm_mix[i], cos_t, sin_t, mla_w_in[j], mla_q_norm[j], mla_w_uq[j],
                            mla_kv_norm[j], mla_w_ukv[j], mla_w_o[j])
        else:
            p = {"mix": rwkv_mix[j], "w_rkv": rwkv_w_rkv[j], "w0": rwkv_w0[j], "w1": rwkv_w1[j],
                 "w2": rwkv_w2[j], "a0": rwkv_a0[j], "a1": rwkv_a1[j], "a2": rwkv_a2[j],
                 "g1": rwkv_g1[j], "g2": rwkv_g2[j], "k_k": rwkv_k_k[j], "k_a": rwkv_k_a[j],
                 "r_k": rwkv_r_k[j], "ln_w": rwkv_ln_w[j], "ln_b": rwkv_ln_b[j], "w_o": rwkv_w_o[j]}
            if j > 0:
                p.update(v0=rwkv_v0[j - 1], v1=rwkv_v1[j - 1], v2=rwkv_v2[j - 1])
            xs, v_first = _rwkv_layer(xs, norm_mix[i], v_first, p)
        xs = _ffn_layer(xs, norm_ffn[i], ffn_w_in[i], ffn_conv_w[i], ffn_conv_b[i], ffn_w_out[i])
    return _final_norm(xs, norm_final[None]).reshape(batch, seq, d)
```

```python
import functools
import math

import jax
import jax.numpy as jnp
from jax import lax
from jax.experimental import pallas as pl
from jax.experimental.pallas import tpu as pltpu

NORM_EPS = 1e-6
GN_EPS = 64e-5
MLA_HEADS = 8
QK_NOPE_DIM = 128
QK_ROPE_DIM = 64
QK_DIM = QK_NOPE_DIM + QK_ROPE_DIM
QK_PAD = 256
V_HEAD_DIM = 128
V_ROWS = V_HEAD_DIM + 16
Q_LORA_RANK = 384
KV_LORA_RANK = 256
ROPE_THETA = 10000.0
RWKV_HEAD = 64
LANES = 128
D_FF = 2816
HALO = 16
WKV_CHUNK = 64
WKV_TILE = 256
WKV_STEP = 512
SUBLANES = 8
ATTN_TK = 512

BF16 = jnp.bfloat16
F32 = jnp.float32
VMEM_LIMIT = 56 * 1024 * 1024


def _params(*sem):
    return pltpu.CompilerParams(dimension_semantics=sem, vmem_limit_bytes=VMEM_LIMIT)


def _dot(a, b, precision=None):
    return jnp.dot(a, b, preferred_element_type=F32, precision=precision)


def _dot_nt(a, b, precision=None):
    return lax.dot_general(a, b, (((1,), (1,)), ((), ())),
                           preferred_element_type=F32, precision=precision)


def _dot_tn(a, b, precision=None):
    return lax.dot_general(a, b, (((0,), (0,)), ((), ())),
                           preferred_element_type=F32, precision=precision)


def _rms(x, g):
    return x * lax.rsqrt(jnp.mean(x * x, axis=-1, keepdims=True) + NORM_EPS) * g


def _row_tile(s, want):
    return min(want, s)


def _full(shape):
    return pl.BlockSpec(shape, lambda *_: (0,) * len(shape))


def _rope_table_kernel(pos_ref, freq_ref, cos_ref, sin_ref):
    ang = pos_ref[...] * freq_ref[...]
    cos_ref[...] = jnp.cos(ang)
    sin_ref[...] = jnp.sin(ang)


def _rope_tables(pos_col, freq_row):
    s = pos_col.shape[0]
    tm = _row_tile(s, 1024)
    return pl.pallas_call(
        _rope_table_kernel,
        grid=(s // tm,),
        in_specs=[pl.BlockSpec((tm, 1), lambda i: (i, 0)), _full((1, LANES))],
        out_specs=[pl.BlockSpec((tm, LANES), lambda i: (i, 0))] * 2,
        out_shape=[jax.ShapeDtypeStruct((s, LANES), F32)] * 2,
        compiler_params=_params("parallel"),
        name="rope_tables",
    )(pos_col, freq_row)


def _mla_pre_kernel(x_ref, g_ref, w_in_ref, qn_ref, w_uq_ref, kvn_ref, w_ukv_ref,
                    cos_ref, sin_ref, q_ref, k_ref, v_ref):
    h = _rms(x_ref[...], g_ref[...]).astype(BF16)
    lat = _dot(h, w_in_ref[...])
    q_lat = lat[:, :Q_LORA_RANK]
    c_kv = lat[:, Q_LORA_RANK:Q_LORA_RANK + KV_LORA_RANK]
    kr = lat[:, Q_LORA_RANK + KV_LORA_RANK:]
    cos = cos_ref[...]
    sin = sin_ref[...]
    k_rope = (kr[:, :QK_ROPE_DIM] * cos[:, :QK_ROPE_DIM]
              + kr[:, QK_ROPE_DIM:] * sin[:, :QK_ROPE_DIM])
    qn = _rms(q_lat, qn_ref[...]).astype(BF16)
    q_all = _dot(qn, w_uq_ref[...])
    n_nope = MLA_HEADS * QK_NOPE_DIM
    n_rope = MLA_HEADS * QK_ROPE_DIM
    reps = n_rope // LANES
    qscale = QK_DIM ** -0.5 * math.log2(math.e)
    q_rope = (q_all[:, n_nope:n_nope + n_rope] * jnp.tile(cos, (1, reps))
              + q_all[:, n_nope + n_rope:] * jnp.tile(sin, (1, reps))) * qscale
    kvn = _rms(c_kv, kvn_ref[...]).astype(BF16)
    kv = _dot(kvn, w_ukv_ref[...])
    tm = x_ref.shape[0]
    qn_t = (q_all[:, :n_nope] * qscale).T.astype(BF16)
    qr_t = q_rope.T.astype(BF16)
    v_t = kv[:, n_nope:].T.astype(BF16)
    zpad_t = jnp.zeros((QK_PAD - QK_DIM, tm), BF16)
    ones_t = jnp.ones((V_ROWS - V_HEAD_DIM, tm), BF16)
    zpad = jnp.zeros((tm, QK_PAD - QK_DIM), BF16)
    k_rope_b = k_rope.astype(BF16)
    for hd in range(MLA_HEADS):
        q_ref[hd, 0:QK_NOPE_DIM, :] = qn_t[hd * QK_NOPE_DIM:(hd + 1) * QK_NOPE_DIM]
        q_ref[hd, QK_NOPE_DIM:QK_DIM, :] = qr_t[hd * QK_ROPE_DIM:(hd + 1) * QK_ROPE_DIM]
        q_ref[hd, QK_DIM:QK_PAD, :] = zpad_t
        k_ref[hd, :, 0:QK_NOPE_DIM] = kv[:, hd * QK_NOPE_DIM:(hd + 1) * QK_NOPE_DIM].astype(BF16)
        k_ref[hd, :, QK_NOPE_DIM:QK_DIM] = k_rope_b
        k_ref[hd, :, QK_DIM:QK_PAD] = zpad
        v_ref[hd, 0, 0:V_HEAD_DIM, :] = v_t[hd * V_HEAD_DIM:(hd + 1) * V_HEAD_DIM]
        v_ref[hd, 0, V_HEAD_DIM:V_ROWS, :] = ones_t


def _mla_pre(x, g, w_in_e, q_norm, w_uq_e, kv_norm, w_ukv_p, cos_t, sin_t):
    s, d = x.shape
    tm = _row_tile(s, ATTN_TK)
    row = lambda w: pl.BlockSpec((tm, w), lambda i: (i, 0))
    return pl.pallas_call(
        _mla_pre_kernel,
        grid=(s // tm,),
        in_specs=[row(d), _full(g.shape), _full(w_in_e.shape), _full(q_norm.shape),
                  _full(w_uq_e.shape), _full(kv_norm.shape), _full(w_ukv_p.shape),
                  row(LANES), row(LANES)],
        out_specs=[pl.BlockSpec((MLA_HEADS, QK_PAD, tm), lambda i: (0, 0, i)),
                   pl.BlockSpec((MLA_HEADS, tm, QK_PAD), lambda i: (0, i, 0)),
                   pl.BlockSpec((MLA_HEADS, 1, V_ROWS, tm), lambda i: (0, i, 0, 0))],
        out_shape=[jax.ShapeDtypeStruct((MLA_HEADS, QK_PAD, s), BF16),
                   jax.ShapeDtypeStruct((MLA_HEADS, s, QK_PAD), BF16),
                   jax.ShapeDtypeStruct((MLA_HEADS, s // tm, V_ROWS, tm), BF16)],
        compiler_params=_params("parallel"),
        name="mla_pre",
    )(x, g, w_in_e, q_norm, w_uq_e, kv_norm, w_ukv_p, cos_t, sin_t)


def _attn_kernel(q_ref, k_ref, v_ref, o_ref, acc_a, acc_b, s_a, s_b, *, tk):
    qi = pl.program_id(1)
    accs = (acc_a, acc_b)
    for acc in accs:
        acc[...] = jnp.zeros(acc.shape, F32)

    def scores(half, j, s_ref):
        q_t = q_ref[:, half * tk:(half + 1) * tk]
        off = pl.multiple_of(j * tk, tk)
        s = _dot(k_ref[pl.ds(off, tk), :], q_t)
        s_ref[...] = s
        return jnp.max(s, axis=0, keepdims=True)

    def absorb(half, j, s_ref, m_blk, m_old, masked):
        if masked:
            s = s_ref[...]
            key = lax.broadcasted_iota(jnp.int32, s.shape, 0)
            qry = lax.broadcasted_iota(jnp.int32, s.shape, 1)
            s = jnp.where(key <= qry, s, jnp.finfo(F32).min)
            m_blk = jnp.max(s, axis=0, keepdims=True)
        else:
            s = s_ref[...]
        m_new = jnp.maximum(m_old, m_blk)
        p = jnp.exp2(s - m_new)
        alpha = jnp.exp2(m_old - m_new)
        acc = accs[half]
        acc[...] = alpha * acc[...] + _dot(v_ref[j], p.astype(BF16))
        return m_new

    m0 = jnp.full((1, tk), -1e30, F32)
    n_full = 2 * qi

    def body(j, carry):
        ma, mb, blk_a = carry
        blk_b = scores(1, j, s_b)
        ma = absorb(0, j, s_a, blk_a, ma, False)
        blk_a = scores(0, j + 1, s_a)
        mb = absorb(1, j, s_b, blk_b, mb, False)
        return ma, mb, blk_a

    blk_a = scores(0, 0, s_a)
    ma, mb, _ = lax.fori_loop(0, n_full, body, (m0, m0, blk_a))
    blk_b = scores(1, n_full, s_b)
    absorb(0, n_full, s_a, None, ma, True)
    scores(1, n_full + 1, s_a)
    mb = absorb(1, n_full, s_b, blk_b, mb, False)
    absorb(1, n_full + 1, s_a, None, mb, True)
    for half, acc in enumerate(accs):
        out = acc[0:V_HEAD_DIM, :] / acc[V_HEAD_DIM:V_HEAD_DIM + 1, :]
        o_ref[half * tk:(half + 1) * tk, :] = out.T.astype(o_ref.dtype)


def _attention(q_t, k_cat, v_t):
    nh, s, _ = k_cat.shape
    tk = v_t.shape[-1]
    tq = 2 * tk
    return pl.pallas_call(
        functools.partial(_attn_kernel, tk=tk),
        grid=(nh, s // tq),
        in_specs=[pl.BlockSpec((None, QK_PAD, tq), lambda h, i: (h, 0, i)),
                  pl.BlockSpec((None, s, QK_PAD), lambda h, i: (h, 0, 0)),
                  pl.BlockSpec((None, s // tk, V_ROWS, tk), lambda h, i: (h, 0, 0, 0))],
        out_specs=pl.BlockSpec((tq, V_HEAD_DIM), lambda h, i: (i, h)),
        out_shape=jax.ShapeDtypeStruct((s, nh * V_HEAD_DIM), BF16),
        scratch_shapes=[pltpu.VMEM((V_ROWS, tk), F32), pltpu.VMEM((V_ROWS, tk), F32),
                        pltpu.VMEM((tk, tk), F32), pltpu.VMEM((tk, tk), F32)],
        compiler_params=_params("parallel", "arbitrary"),
        name="mla_attention",
    )(q_t, k_cat, v_t)


FFN_CHUNKS = ((0, 1024), (1024, 2048), (2048, D_FF))


def _mix_ffn_kernel(*refs, gated):
    if gated:
        (x_ref, xp_ref, z_ref, zp_ref, gt_ref, gtp_ref, wo_ref, g_ref, wi_ref, cw_ref, cb_ref,
         wout_ref, o_ref) = refs
    else:
        (x_ref, xp_ref, z_ref, zp_ref, wo_ref, g_ref, wi_ref, cw_ref, cb_ref,
         wout_ref, o_ref) = refs
        gt_ref = gtp_ref = None
    i = pl.program_id(0)

    def mixed(xr, zr, gr):
        z = zr[...]
        if gated:
            z = z * gr[...]
        return xr[...] + _dot(z.astype(BF16), wo_ref[...])

    x1 = mixed(x_ref, z_ref, gt_ref)
    x1p = mixed(xp_ref, zp_ref, gtp_ref)
    keep = (i > 0).astype(F32)
    hn = jnp.concatenate([(_rms(x1p, g_ref[...]) * keep).astype(BF16),
                          _rms(x1, g_ref[...]).astype(BF16)], axis=0)
    acc = x1
    for lo, hi in FFN_CHUNKS:
        gate = _dot(hn, wi_ref[:, lo:hi])
        up = _dot(hn[HALO:], wi_ref[:, D_FF + lo:D_FF + hi])
        conv = (gate[HALO:] * cw_ref[2:3, lo:hi]
                + pltpu.roll(gate, 1, 0)[HALO:] * cw_ref[1:2, lo:hi]
                + pltpu.roll(gate, 2, 0)[HALO:] * cw_ref[0:1, lo:hi] + cb_ref[:, lo:hi])
        act = (conv * jax.nn.sigmoid(conv) * up).astype(BF16)
        acc = acc + _dot(act, wout_ref[lo:hi, :])
    o_ref[...] = acc


def _mix_ffn(x, z, gate, w_o, g, w_in, conv_w, conv_b, w_out):
    s, d = x.shape
    tm = _row_tile(s, 512)
    row = pl.BlockSpec((tm, d), lambda i: (i, 0))
    prev = pl.BlockSpec((HALO, d), lambda i: (jnp.maximum(i * (tm // HALO) - 1, 0), 0))
    const = lambda a: pl.BlockSpec(a.shape, lambda i: (0,) * a.ndim, pipeline_mode=pl.Buffered(1))
    gated = gate is not None
    args = [x, x, z, z] + ([gate, gate] if gated else []) + [w_o, g, w_in, conv_w, conv_b, w_out]
    specs = ([row, prev, row, prev] + ([row, prev] if gated else [])
             + [const(a) for a in (w_o, g, w_in, conv_w, conv_b, w_out)])
    return pl.pallas_call(
        functools.partial(_mix_ffn_kernel, gated=gated),
        grid=(s // tm,),
        in_specs=specs,
        out_specs=row,
        out_shape=jax.ShapeDtypeStruct((s, d), F32),
        compiler_params=_params("parallel"),
        name="mix_ffn",
    )(*args)


def _group_sum(x):
    w = 2 * LANES
    r = lax.broadcasted_iota(jnp.int32, (w, w), 0) // RWKV_HEAD
    c = lax.broadcasted_iota(jnp.int32, (w, w), 1) // RWKV_HEAD
    ones_bd = (r == c).astype(BF16)
    hi = x.astype(BF16)
    lo = (x - hi.astype(F32)).astype(BF16)
    cols = [_dot(hi[:, j:j + w], ones_bd) + _dot(lo[:, j:j + w], ones_bd)
            for j in range(0, x.shape[1], w)]
    return cols[0] if len(cols) == 1 else jnp.concatenate(cols, axis=1)


def _rwkv_pre_kernel(*refs, has_vres):
    (x_ref, xp_ref, g_ref, mix_ref, wrkv_ref, w0_ref, w1_ref, w2_ref, a0_ref, a1_ref, a2_ref,
     g1_ref, g2_ref, kk_ref, ka_ref, rk_ref) = refs[:16]
    if has_vres:
        vf_ref, v0_ref, v1_ref, v2_ref = refs[16:20]
        at_o, rt_o, bt_o, kt_o, bh_o, kh_o, vb_o, wc_o, bonus_o, g_o = refs[20:]
    else:
        at_o, rt_o, bt_o, kt_o, bh_o, kh_o, vb_o, wc_o, bonus_o, g_o, v_o = refs[16:]
    i = pl.program_id(0)
    tm = x_ref.shape[0]
    h = _rms(x_ref[...], g_ref[...])
    hp = _rms(xp_ref[...], g_ref[...])[HALO - 1:HALO] * (i > 0).astype(F32)
    row = lax.broadcasted_iota(jnp.int32, h.shape, 0)
    shifted = jnp.where(row == 0, jnp.broadcast_to(hp, h.shape), pltpu.roll(h, 1, 0))
    xx = shifted - h
    mix = mix_ref[...]
    xm = lambda n: (h + xx * mix[n:n + 1]).astype(BF16)
    xv = xm(2)
    r = _dot(xm(0), wrkv_ref[0])
    k = _dot(xm(1), wrkv_ref[1])
    v = _dot(xv, wrkv_ref[2])
    lora_w = _dot(jnp.tanh(_dot(xm(3), w1_ref[...])).astype(BF16), w2_ref[...])
    z = -(w0_ref[...] + lora_w)
    softplus = jnp.maximum(z, 0.0) + jnp.log(1.0 + jnp.exp(-jnp.abs(z)))
    lw = -jnp.exp(-softplus - 0.5)
    a = jax.nn.sigmoid(a0_ref[...] + _dot(_dot(xm(4), a1_ref[...]).astype(BF16), a2_ref[...]))
    g_o[...] = _dot(jax.nn.sigmoid(_dot(xm(5), g1_ref[...])).astype(BF16), g2_ref[...])
    if has_vres:
        gate_v = jax.nn.sigmoid(v0_ref[...] + _dot(_dot(xv, v1_ref[...]).astype(BF16), v2_ref[...]))
        v = v + (vf_ref[...] - v) * gate_v
    else:
        v_o[...] = v
    kk = k * kk_ref[...]
    kk = kk / jnp.maximum(jnp.sqrt(_group_sum(kk * kk)), 1e-12)
    k = k * (1.0 + (a - 1.0) * ka_ref[...])
    bonus_o[...] = _group_sum(r * k * rk_ref[...]) * v
    C = WKV_CHUNK
    rr = lax.broadcasted_iota(jnp.int32, (tm, tm), 0)
    cc = lax.broadcasted_iota(jnp.int32, (tm, tm), 1)
    tri = (((rr // C) == (cc // C)) & (cc <= rr)).astype(BF16)
    hi = lw.astype(BF16)
    rem = lw - hi.astype(F32)
    mid = rem.astype(BF16)
    lo = (rem - mid.astype(F32)).astype(BF16)
    cl = _dot(tri, hi) + _dot(tri, mid) + _dot(tri, lo)
    ends = [cl[c * C + C - 1:c * C + C] for c in range(tm // C)]
    cl_end = jnp.concatenate([jnp.broadcast_to(e, (C, e.shape[1])) for e in ends], axis=0)
    e_neg = jnp.exp(-cl)
    e_rem = jnp.exp(cl_end - cl)
    at_o[...] = (-kk * jnp.exp(cl - lw)).astype(BF16)
    rt_o[...] = (r * jnp.exp(cl)).astype(BF16)
    bt_o[...] = (kk * a * e_neg).astype(BF16)
    kt_o[...] = (k * e_neg).astype(BF16)
    bh_o[...] = (kk * a * e_rem).astype(BF16)
    kh_o[...] = (k * e_rem).astype(BF16)
    vb_o[...] = v.astype(BF16)
    for c, e in enumerate(ends):
        wc_o[c] = jnp.broadcast_to(jnp.exp(e), wc_o.shape[1:])


def _rwkv_pre(x, g, p, v_first):
    s, d = x.shape
    tm = _row_tile(s, WKV_TILE)
    row = pl.BlockSpec((tm, d), lambda i: (i, 0))
    prev = pl.BlockSpec((HALO, d), lambda i: (jnp.maximum(i * (tm // HALO) - 1, 0), 0))
    names = ["mix", "w_rkv", "w0", "w1", "w2", "a0", "a1", "a2", "g1", "g2", "k_k", "k_a", "r_k"]
    args = [x, x, g] + [p[n] for n in names]
    specs = [row, prev, _full(g.shape)] + [_full(p[n].shape) for n in names]
    has_vres = v_first is not None
    if has_vres:
        args += [v_first, p["v0"], p["v1"], p["v2"]]
        specs += [row] + [_full(p[n].shape) for n in ("v0", "v1", "v2")]
    n_ch = tm // WKV_CHUNK
    wc_spec = pl.BlockSpec((n_ch, SUBLANES, d), lambda i: (i, 0, 0))
    out_specs = [row] * 7 + [wc_spec, row, row]
    out_shape = ([jax.ShapeDtypeStruct((s, d), BF16)] * 7
                 + [jax.ShapeDtypeStruct((s // WKV_CHUNK, SUBLANES, d), F32)]
                 + [jax.ShapeDtypeStruct((s, d), F32)] * 2)
    if not has_vres:
        out_specs.append(row)
        out_shape.append(jax.ShapeDtypeStruct((s, d), F32))
    return pl.pallas_call(
        functools.partial(_rwkv_pre_kernel, has_vres=has_vres),
        grid=(s // tm,),
        in_specs=specs,
        out_specs=out_specs,
        out_shape=out_shape,
        compiler_params=_params("parallel"),
        name="rwkv_pre",
    )(*args)


def _wkv_kernel(at_ref, rt_ref, bt_ref, kt_ref, bh_ref, kh_ref, v_ref, wc_ref, bonus_ref,
                lnw_ref, lnb_ref, o_ref, s_ref, *, group, prec):
    C = WKV_CHUNK
    C2 = 2 * C
    n_pairs = o_ref.shape[1] // LANES
    n_groups = o_ref.shape[0] // (C * group)

    @pl.when(pl.program_id(0) == 0)
    def _():
        s_ref[...] = jnp.zeros(s_ref.shape, F32)

    lane = lax.broadcasted_iota(jnp.int32, (1, LANES), 1)
    head0 = lane < RWKV_HEAD
    m0 = head0.astype(F32)
    m1 = 1.0 - m0
    t2 = lax.broadcasted_iota(jnp.int32, (C2, C2), 0)
    s2 = lax.broadcasted_iota(jnp.int32, (C2, C2), 1)
    same_head = (t2 // C) == (s2 // C)
    strict = (same_head & (s2 < t2)).astype(F32)
    incl = (same_head & (s2 <= t2)).astype(F32)
    eye = (t2 == s2).astype(F32)

    def level_mask(b):
        return (((t2 // (2 * b)) == (s2 // (2 * b))) & ((t2 % (2 * b)) >= b)
                & ((s2 % (2 * b)) < b)).astype(F32)

    strict_m, incl_m = strict, incl
    levels = []
    b_sz = 1
    while b_sz < C:
        levels.append(level_mask(b_sz))
        b_sz *= 2

    def stack(x):
        zero = jnp.zeros_like(x)
        return jnp.concatenate([jnp.where(head0, x, zero), jnp.where(head0, zero, x)], axis=0)

    def twice(x):
        return jnp.concatenate([x, x], axis=0)

    def cast(x):
        return x.astype(prec)

    def group_body(g, carry):
        probs = [(ci, p) for ci in range(group) for p in range(n_pairs)]
        offs = [pl.multiple_of((g * group + ci) * C, C) for ci in range(group)]

        def load(ref, ci, p):
            return cast(ref[pl.ds(offs[ci], C), p * LANES:(p + 1) * LANES])

        a_s = [stack(load(at_ref, ci, p)) for ci, p in probs]
        r_s = [stack(load(rt_ref, ci, p)) for ci, p in probs]
        v_s = [stack(load(v_ref, ci, p)) for ci, p in probs]
        hat = [jnp.concatenate([stack(load(bh_ref, ci, p)), stack(load(kh_ref, ci, p))], axis=0)
               for ci, p in probs]
        til = [jnp.concatenate([twice(load(bt_ref, ci, p)), twice(load(kt_ref, ci, p))], axis=0)
               for ci, p in probs]
        lm = [_dot_nt(jnp.concatenate([a, r], axis=0), t) for a, r, t in zip(a_s, r_s, til)]
        l_ab = [x[:C2, :C2] * strict_m for x in lm]
        l_ak = [cast(x[:C2, C2:] * strict_m) for x in lm]
        m_cat = [cast(jnp.concatenate([x[C2:, :C2] * incl_m, x[C2:, C2:] * incl_m], axis=1))
                 for x in lm]
        t_inv = [eye + l * levels[0] for l in l_ab]
        for lvl in levels[1:]:
            t_b = [cast(t) for t in t_inv]
            z = [_dot(t, cast(l * lvl)) for t, l in zip(t_b, l_ab)]
            t_inv = [t + _dot(cast(zz), tb) for t, zz, tb in zip(t_inv, z, t_b)]
        t_b = [cast(t) for t in t_inv]
        p_loc = [_dot(l, v) for l, v in zip(l_ak, v_s)]
        for ci in range(group):
            rows = pl.ds(offs[ci], C)
            c_idx = g * group + ci
            sel = [ci * n_pairs + p for p in range(n_pairs)]
            st = [s_ref[p] for p in range(n_pairs)]
            st_b = [cast(s) for s in st]
            x = [_dot_nt(a_s[i], sb) + p_loc[i] for i, sb in zip(sel, st_b)]
            u = [cast(_dot(t_b[i], cast(xx))) for i, xx in zip(sel, x)]
            uv = [jnp.concatenate([uu, v_s[i]], axis=0) for i, uu in zip(sel, u)]
            y = [_dot_nt(r_s[i], sb) + _dot(m_cat[i], w) for i, sb, w in zip(sel, st_b, uv)]
            for p in range(n_pairs):
                cols = slice(p * LANES, (p + 1) * LANES)
                w_end = wc_ref[c_idx, 0:1, cols]
                s_ref[p] = st[p] * w_end + _dot_tn(uv[p], hat[sel[p]])
                yp = y[p][:C] + y[p][C:]
                mu = (jnp.sum(yp * m0, axis=-1, keepdims=True) * m0
                      + jnp.sum(yp * m1, axis=-1, keepdims=True) * m1) * (1.0 / RWKV_HEAD)
                yc = yp - mu
                yc2 = yc * yc
                var = (jnp.sum(yc2 * m0, axis=-1, keepdims=True) * m0
                       + jnp.sum(yc2 * m1, axis=-1, keepdims=True) * m1) * (1.0 / RWKV_HEAD)
                yn = yc * lax.rsqrt(var + GN_EPS)
                o_ref[rows, cols] = yn * lnw_ref[:, cols] + lnb_ref[:, cols] + bonus_ref[rows, cols]
        return carry

    lax.fori_loop(0, n_groups, group_body, 0)


def _wkv(at, rt, bt, kt, bh, kh, vb, wc, bonus, ln_w, ln_b, prec=BF16):
    s, d = at.shape
    tt = _row_tile(s, WKV_STEP)
    group = 2
    row = pl.BlockSpec((tt, d), lambda t: (t, 0))
    return pl.pallas_call(
        functools.partial(_wkv_kernel, group=group, prec=prec),
        grid=(s // tt,),
        in_specs=[row] * 7 + [pl.BlockSpec((tt // WKV_CHUNK, SUBLANES, d), lambda t: (t, 0, 0)),
                              row, _full(ln_w.shape), _full(ln_b.shape)],
        out_specs=row,
        out_shape=jax.ShapeDtypeStruct((s, d), F32),
        scratch_shapes=[pltpu.VMEM((d // LANES, LANES, LANES), F32)],
        compiler_params=_params("arbitrary"),
        name="wkv7_scan",
    )(at, rt, bt, kt, bh, kh, vb, wc, bonus, ln_w, ln_b)


def _final_norm_kernel(x_ref, g_ref, o_ref):
    o_ref[...] = _rms(x_ref[...], g_ref[...])


def _final_norm(x, g):
    s, d = x.shape
    tm = _row_tile(s, 1024)
    row = pl.BlockSpec((tm, d), lambda i: (i, 0))
    return pl.pallas_call(
        _final_norm_kernel, grid=(s // tm,), in_specs=[row, _full(g.shape)], out_specs=row,
        out_shape=jax.ShapeDtypeStruct((s, d), F32),
        compiler_params=_params("parallel"), name="final_norm",
    )(x, g)


def _rot_cols(w):
    half = QK_ROPE_DIM // 2
    return jnp.concatenate([-w[..., half:], w[..., :half]], axis=-1)


def _mla_weights(w_in, w_uq, w_ukv):
    d = w_in.shape[0]
    lat = Q_LORA_RANK + KV_LORA_RANK
    w_kr = w_in[:, lat:]
    w_in_e = jnp.concatenate([w_in[:, :lat], w_kr, _rot_cols(w_kr)], axis=1).astype(BF16)
    uq = w_uq.reshape(Q_LORA_RANK, MLA_HEADS, QK_DIM)
    uq_rope = uq[:, :, QK_NOPE_DIM:]
    w_uq_e = jnp.concatenate(
        [uq[:, :, :QK_NOPE_DIM].reshape(Q_LORA_RANK, -1),
         uq_rope.reshape(Q_LORA_RANK, -1),
         _rot_cols(uq_rope).reshape(Q_LORA_RANK, -1)], axis=1).astype(BF16)
    ukv = w_ukv.reshape(KV_LORA_RANK, MLA_HEADS, QK_NOPE_DIM + V_HEAD_DIM)
    w_ukv_p = jnp.concatenate(
        [ukv[:, :, :QK_NOPE_DIM].reshape(KV_LORA_RANK, -1),
         ukv[:, :, QK_NOPE_DIM:].reshape(KV_LORA_RANK, -1)], axis=1).astype(BF16)
    del d
    return w_in_e, w_uq_e, w_ukv_p


def _mla_layer(x, g, cos_t, sin_t, w_in, q_norm, w_uq, kv_norm, w_ukv):
    w_in_e, w_uq_e, w_ukv_p = _mla_weights(w_in, w_uq, w_ukv)
    q_cat, k_cat, v = _mla_pre(x, g[None], w_in_e, q_norm[None], w_uq_e, kv_norm[None],
                               w_ukv_p, cos_t, sin_t)
    return _attention(q_cat, k_cat, v)


def _rwkv_layer(x, g, v_first, p, wkv_prec=BF16):
    row = lambda t: t.reshape(1, -1)
    pp = {
        "mix": p["mix"], "w_rkv": p["w_rkv"].astype(BF16),
        "w0": row(p["w0"]), "w1": p["w1"].astype(BF16), "w2": p["w2"].astype(BF16),
        "a0": row(p["a0"]), "a1": p["a1"].astype(BF16), "a2": p["a2"].astype(BF16),
        "g1": p["g1"].astype(BF16), "g2": p["g2"].astype(BF16),
        "k_k": row(p["k_k"]), "k_a": row(p["k_a"]), "r_k": row(p["r_k"]),
    }
    if v_first is not None:
        pp.update(v0=row(p["v0"]), v1=p["v1"].astype(BF16), v2=p["v2"].astype(BF16))
    outs = _rwkv_pre(x, g[None], pp, v_first)
    scan_in, gate = outs[:9], outs[9]
    z = _wkv(*scan_in, row(p["ln_w"]), row(p["ln_b"]), prec=wkv_prec)
    return z, gate, (outs[10] if v_first is None else v_first)


def _mix_ffn_layer(x, z, gate, w_o, g, w_in, conv_w, conv_b, w_out):
    return _mix_ffn(x, z, gate, w_o.astype(BF16), g[None], w_in.astype(BF16), conv_w, conv_b[None],
                    w_out.astype(BF16))


def kernel(x, positions, norm_mix, norm_ffn, norm_final, mla_w_in, mla_q_norm, mla_w_uq, mla_kv_norm, mla_w_ukv, mla_w_o, rwkv_mix, rwkv_w_rkv, rwkv_w0, rwkv_w1, rwkv_w2, rwkv_a0, rwkv_a1, rwkv_a2, rwkv_v0, rwkv_v1, rwkv_v2, rwkv_g1, rwkv_g2, rwkv_k_k, rwkv_k_a, rwkv_r_k, rwkv_ln_w, rwkv_ln_b, rwkv_w_o, ffn_w_in, ffn_conv_w, ffn_conv_b, ffn_w_out):
    batch, seq, d = x.shape
    assert batch == 1
    depth = norm_mix.shape[0]
    half = QK_ROPE_DIM // 2
    inv_freq = ROPE_THETA ** (-jnp.arange(half, dtype=F32) / half)
    freq_row = jnp.tile(inv_freq, LANES // half)[None]
    cos_t, sin_t = _rope_tables(positions.astype(F32).reshape(seq, 1), freq_row)
    xs = x.reshape(seq, d)
    v_first = None
    for i in range(depth):
        j = i // 2
        if i % 2 == 0:
            z = _mla_layer(xs, norm_mix[i], cos_t, sin_t, mla_w_in[j], mla_q_norm[j], mla_w_uq[j],
                           mla_kv_norm[j], mla_w_ukv[j])
            gate, w_o = None, mla_w_o[j]
        else:
            p = {"mix": rwkv_mix[j], "w_rkv": rwkv_w_rkv[j], "w0": rwkv_w0[j], "w1": rwkv_w1[j],
                 "w2": rwkv_w2[j], "a0": rwkv_a0[j], "a1": rwkv_a1[j], "a2": rwkv_a2[j],
                 "g1": rwkv_g1[j], "g2": rwkv_g2[j], "k_k": rwkv_k_k[j], "k_a": rwkv_k_a[j],
                 "r_k": rwkv_r_k[j], "ln_w": rwkv_ln_w[j], "ln_b": rwkv_ln_b[j]}
            if j > 0:
                p.update(v0=rwkv_v0[j - 1], v1=rwkv_v1[j - 1], v2=rwkv_v2[j - 1])
            z, gate, v_first = _rwkv_layer(xs, norm_mix[i], v_first, p)
            w_o = rwkv_w_o[j]
        xs = _mix_ffn_layer(xs, z, gate, w_o, norm_ffn[i], ffn_w_in[i], ffn_conv_w[i], ffn_conv_b[i],
                            ffn_w_out[i])
    return _final_norm(xs, norm_final[None]).reshape(batch, seq, d)
```

```python
import functools
import math

import jax
import jax.numpy as jnp
from jax import lax
from jax.experimental import pallas as pl
from jax.experimental.pallas import tpu as pltpu

NORM_EPS = 1e-6
GN_EPS = 64e-5
MLA_HEADS = 8
QK_NOPE_DIM = 128
QK_ROPE_DIM = 64
QK_DIM = QK_NOPE_DIM + QK_ROPE_DIM
QK_PAD = 256
V_HEAD_DIM = 128
V_ROWS = V_HEAD_DIM + 16
Q_LORA_RANK = 384
KV_LORA_RANK = 256
ROPE_THETA = 10000.0
RWKV_HEAD = 64
LANES = 128
D_FF = 2816
HALO = 16
WKV_CHUNK = 64
WKV_TILE = 256
WKV_STEP = 512
SUBLANES = 8
ATTN_TK = 512

BF16 = jnp.bfloat16
F32 = jnp.float32
VMEM_LIMIT = 56 * 1024 * 1024


def _params(*sem):
    return pltpu.CompilerParams(dimension_semantics=sem, vmem_limit_bytes=VMEM_LIMIT)


def _dot(a, b, precision=None):
    return jnp.dot(a, b, preferred_element_type=F32, precision=precision)


def _dot_nt(a, b, precision=None):
    return lax.dot_general(a, b, (((1,), (1,)), ((), ())),
                           preferred_element_type=F32, precision=precision)


def _dot_tn(a, b, precision=None):
    return lax.dot_general(a, b, (((0,), (0,)), ((), ())),
                           preferred_element_type=F32, precision=precision)


def _rms(x, g):
    return x * lax.rsqrt(jnp.mean(x * x, axis=-1, keepdims=True) + NORM_EPS) * g


def _row_tile(s, want):
    return min(want, s)


def _full(shape):
    return pl.BlockSpec(shape, lambda *_: (0,) * len(shape))


def _rope_table_kernel(pos_ref, freq_ref, cos_ref, sin_ref):
    ang = pos_ref[...] * freq_ref[...]
    cos_ref[...] = jnp.cos(ang)
    sin_ref[...] = jnp.sin(ang)


def _rope_tables(pos_col, freq_row):
    s = pos_col.shape[0]
    tm = _row_tile(s, 1024)
    return pl.pallas_call(
        _rope_table_kernel,
        grid=(s // tm,),
        in_specs=[pl.BlockSpec((tm, 1), lambda i: (i, 0)), _full((1, LANES))],
        out_specs=[pl.BlockSpec((tm, LANES), lambda i: (i, 0))] * 2,
        out_shape=[jax.ShapeDtypeStruct((s, LANES), F32)] * 2,
        compiler_params=_params("parallel"),
        name="rope_tables",
    )(pos_col, freq_row)


def _mla_pre_kernel(x_ref, g_ref, w_in_ref, qn_ref, w_uq_ref, kvn_ref, w_ukv_ref,
                    cos_ref, sin_ref, q_ref, k_ref, v_ref):
    h = _rms(x_ref[...], g_ref[...]).astype(BF16)
    lat = _dot(h, w_in_ref[...])
    q_lat = lat[:, :Q_LORA_RANK]
    c_kv = lat[:, Q_LORA_RANK:Q_LORA_RANK + KV_LORA_RANK]
    kr = lat[:, Q_LORA_RANK + KV_LORA_RANK:]
    cos = cos_ref[...]
    sin = sin_ref[...]
    k_rope = (kr[:, :QK_ROPE_DIM] * cos[:, :QK_ROPE_DIM]
              + kr[:, QK_ROPE_DIM:] * sin[:, :QK_ROPE_DIM])
    qn = _rms(q_lat, qn_ref[...]).astype(BF16)
    q_all = _dot(qn, w_uq_ref[...])
    n_nope = MLA_HEADS * QK_NOPE_DIM
    n_rope = MLA_HEADS * QK_ROPE_DIM
    reps = n_rope // LANES
    qscale = QK_DIM ** -0.5 * math.log2(math.e)
    q_rope = (q_all[:, n_nope:n_nope + n_rope] * jnp.tile(cos, (1, reps))
              + q_all[:, n_nope + n_rope:] * jnp.tile(sin, (1, reps))) * qscale
    kvn = _rms(c_kv, kvn_ref[...]).astype(BF16)
    kv = _dot(kvn, w_ukv_ref[...])
    tm = x_ref.shape[0]
    qn_t = (q_all[:, :n_nope] * qscale).T.astype(BF16)
    qr_t = q_rope.T.astype(BF16)
    v_t = kv[:, n_nope:].T.astype(BF16)
    zpad_t = jnp.zeros((QK_PAD - QK_DIM, tm), BF16)
    ones_t = jnp.ones((V_ROWS - V_HEAD_DIM, tm), BF16)
    zpad = jnp.zeros((tm, QK_PAD - QK_DIM), BF16)
    k_rope_b = k_rope.astype(BF16)
    for hd in range(MLA_HEADS):
        q_ref[hd, 0:QK_NOPE_DIM, :] = qn_t[hd * QK_NOPE_DIM:(hd + 1) * QK_NOPE_DIM]
        q_ref[hd, QK_NOPE_DIM:QK_DIM, :] = qr_t[hd * QK_ROPE_DIM:(hd + 1) * QK_ROPE_DIM]
        q_ref[hd, QK_DIM:QK_PAD, :] = zpad_t
        k_ref[hd, :, 0:QK_NOPE_DIM] = kv[:, hd * QK_NOPE_DIM:(hd + 1) * QK_NOPE_DIM].astype(BF16)
        k_ref[hd, :, QK_NOPE_DIM:QK_DIM] = k_rope_b
        k_ref[hd, :, QK_DIM:QK_PAD] = zpad
        v_ref[hd, 0, 0:V_HEAD_DIM, :] = v_t[hd * V_HEAD_DIM:(hd + 1) * V_HEAD_DIM]
        v_ref[hd, 0, V_HEAD_DIM:V_ROWS, :] = ones_t


def _mla_pre(x, g, w_in_e, q_norm, w_uq_e, kv_norm, w_ukv_p, cos_t, sin_t):
    s, d = x.shape
    tm = _row_tile(s, ATTN_TK)
    row = lambda w: pl.BlockSpec((tm, w), lambda i: (i, 0))
    return pl.pallas_call(
        _mla_pre_kernel,
        grid=(s // tm,),
        in_specs=[row(d), _full(g.shape), _full(w_in_e.shape), _full(q_norm.shape),
                  _full(w_uq_e.shape), _full(kv_norm.shape), _full(w_ukv_p.shape),
                  row(LANES), row(LANES)],
        out_specs=[pl.BlockSpec((MLA_HEADS, QK_PAD, tm), lambda i: (0, 0, i)),
                   pl.BlockSpec((MLA_HEADS, tm, QK_PAD), lambda i: (0, i, 0)),
                   pl.BlockSpec((MLA_HEADS, 1, V_ROWS, tm), lambda i: (0, i, 0, 0))],
        out_shape=[jax.ShapeDtypeStruct((MLA_HEADS, QK_PAD, s), BF16),
                   jax.ShapeDtypeStruct((MLA_HEADS, s, QK_PAD), BF16),
                   jax.ShapeDtypeStruct((MLA_HEADS, s // tm, V_ROWS, tm), BF16)],
        compiler_params=_params("parallel"),
        name="mla_pre",
    )(x, g, w_in_e, q_norm, w_uq_e, kv_norm, w_ukv_p, cos_t, sin_t)


def _attn_kernel(q_ref, k_ref, v_ref, o_ref, acc_a, acc_b, s_a0, s_a1, s_b0, s_b1, *, tk):
    qi = pl.program_id(1)
    accs = (acc_a, acc_b)
    for acc in accs:
        acc[...] = jnp.zeros(acc.shape, F32)

    def scores(half, j, s_ref):
        q_t = q_ref[:, half * tk:(half + 1) * tk]
        off = pl.multiple_of(j * tk, tk)
        s = _dot(k_ref[pl.ds(off, tk), :], q_t)
        s_ref[...] = s
        return jnp.max(s, axis=0, keepdims=True)

    def absorb(half, j, s_ref, m_blk, m_old, masked):
        if masked:
            s = s_ref[...]
            key = lax.broadcasted_iota(jnp.int32, s.shape, 0)
            qry = lax.broadcasted_iota(jnp.int32, s.shape, 1)
            s = jnp.where(key <= qry, s, jnp.finfo(F32).min)
            m_blk = jnp.max(s, axis=0, keepdims=True)
        else:
            s = s_ref[...]
        m_new = jnp.maximum(m_old, m_blk)
        p = jnp.exp2(s - m_new)
        alpha = jnp.exp2(m_old - m_new)
        acc = accs[half]
        acc[...] = alpha * acc[...] + _dot(v_ref[j], p.astype(BF16))
        return m_new

    m0 = jnp.full((1, tk), -1e30, F32)
    n_full = 2 * qi

    def body(jj, carry):
        ma, mb, blk_a = carry
        j = 2 * jj
        blk_b = scores(1, j, s_b0)
        ma = absorb(0, j, s_a0, blk_a, ma, False)
        blk_a = scores(0, j + 1, s_a1)
        mb = absorb(1, j, s_b0, blk_b, mb, False)
        blk_b = scores(1, j + 1, s_b1)
        ma = absorb(0, j + 1, s_a1, blk_a, ma, False)
        blk_a = scores(0, j + 2, s_a0)
        mb = absorb(1, j + 1, s_b1, blk_b, mb, False)
        return ma, mb, blk_a

    blk_a = scores(0, 0, s_a0)
    ma, mb, _ = lax.fori_loop(0, qi, body, (m0, m0, blk_a))
    blk_b = scores(1, n_full, s_b0)
    absorb(0, n_full, s_a0, None, ma, True)
    scores(1, n_full + 1, s_b1)
    mb = absorb(1, n_full, s_b0, blk_b, mb, False)
    absorb(1, n_full + 1, s_b1, None, mb, True)
    for half, acc in enumerate(accs):
        out = acc[0:V_HEAD_DIM, :] / acc[V_HEAD_DIM:V_HEAD_DIM + 1, :]
        o_ref[half * tk:(half + 1) * tk, :] = out.T.astype(o_ref.dtype)


def _attention(q_t, k_cat, v_t):
    nh, s, _ = k_cat.shape
    tk = v_t.shape[-1]
    tq = 2 * tk
    return pl.pallas_call(
        functools.partial(_attn_kernel, tk=tk),
        grid=(nh, s // tq),
        in_specs=[pl.BlockSpec((None, QK_PAD, tq), lambda h, i: (h, 0, i)),
                  pl.BlockSpec((None, s, QK_PAD), lambda h, i: (h, 0, 0)),
                  pl.BlockSpec((None, s // tk, V_ROWS, tk), lambda h, i: (h, 0, 0, 0))],
        out_specs=pl.BlockSpec((tq, V_HEAD_DIM), lambda h, i: (i, h)),
        out_shape=jax.ShapeDtypeStruct((s, nh * V_HEAD_DIM), BF16),
        scratch_shapes=[pltpu.VMEM((V_ROWS, tk), F32)] * 2 + [pltpu.VMEM((tk, tk), F32)] * 4,
        compiler_params=_params("parallel", "arbitrary"),
        name="mla_attention",
    )(q_t, k_cat, v_t)


FFN_CHUNKS = ((0, 1024), (1024, 2048), (2048, D_FF))


def _mix_ffn_kernel(*refs, gated, final):
    refs = list(refs)
    o_ref = refs.pop()
    fg_ref = refs.pop() if final else None
    x_ref, xp_ref, z_ref, zp_ref = refs[:4]
    gt_ref, gtp_ref = refs[4:6] if gated else (None, None)
    wo_ref, g_ref, wi_ref, cw_ref, cb_ref, wout_ref = refs[-6:]
    i = pl.program_id(0)

    def mixed(xr, zr, gr):
        z = zr[...]
        if gated:
            z = z * gr[...]
        return xr[...] + _dot(z.astype(BF16), wo_ref[...])

    x1 = mixed(x_ref, z_ref, gt_ref)
    x1p = mixed(xp_ref, zp_ref, gtp_ref)
    keep = (i > 0).astype(F32)
    hn = jnp.concatenate([(_rms(x1p, g_ref[...]) * keep).astype(BF16),
                          _rms(x1, g_ref[...]).astype(BF16)], axis=0)
    acc = x1
    for lo, hi in FFN_CHUNKS:
        gate = _dot(hn, wi_ref[:, lo:hi])
        up = _dot(hn[HALO:], wi_ref[:, D_FF + lo:D_FF + hi])
        conv = (gate[HALO:] * cw_ref[2:3, lo:hi]
                + pltpu.roll(gate, 1, 0)[HALO:] * cw_ref[1:2, lo:hi]
                + pltpu.roll(gate, 2, 0)[HALO:] * cw_ref[0:1, lo:hi] + cb_ref[:, lo:hi])
        act = (conv * jax.nn.sigmoid(conv) * up).astype(BF16)
        acc = acc + _dot(act, wout_ref[lo:hi, :])
    o_ref[...] = _rms(acc, fg_ref[...]) if final else acc


def _mix_ffn(x, z, gate, w_o, g, w_in, conv_w, conv_b, w_out, final_gain=None):
    s, d = x.shape
    tm = _row_tile(s, 512)
    row = pl.BlockSpec((tm, d), lambda i: (i, 0))
    prev = pl.BlockSpec((HALO, d), lambda i: (jnp.maximum(i * (tm // HALO) - 1, 0), 0))
    const = lambda a: pl.BlockSpec(a.shape, lambda i: (0,) * a.ndim, pipeline_mode=pl.Buffered(1))
    gated = gate is not None
    final = final_gain is not None
    consts = [w_o, g, w_in, conv_w, conv_b, w_out] + ([final_gain] if final else [])
    args = [x, x, z, z] + ([gate, gate] if gated else []) + consts
    specs = [row, prev, row, prev] + ([row, prev] if gated else []) + [const(a) for a in consts]
    return pl.pallas_call(
        functools.partial(_mix_ffn_kernel, gated=gated, final=final),
        grid=(s // tm,),
        in_specs=specs,
        out_specs=row,
        out_shape=jax.ShapeDtypeStruct((s, d), F32),
        compiler_params=_params("parallel"),
        name="mix_ffn",
    )(*args)


def _group_sum(x):
    w = 2 * LANES
    r = lax.broadcasted_iota(jnp.int32, (w, w), 0) // RWKV_HEAD
    c = lax.broadcasted_iota(jnp.int32, (w, w), 1) // RWKV_HEAD
    ones_bd = (r == c).astype(BF16)
    hi = x.astype(BF16)
    lo = (x - hi.astype(F32)).astype(BF16)
    cols = [_dot(hi[:, j:j + w], ones_bd) + _dot(lo[:, j:j + w], ones_bd)
            for j in range(0, x.shape[1], w)]
    return cols[0] if len(cols) == 1 else jnp.concatenate(cols, axis=1)


def _rwkv_pre_kernel(*refs, has_vres):
    (x_ref, xp_ref, g_ref, mix_ref, wrkv_ref, w0_ref, w1_ref, w2_ref, a0_ref, a1_ref, a2_ref,
     g1_ref, g2_ref, kk_ref, ka_ref, rk_ref) = refs[:16]
    if has_vres:
        vf_ref, v0_ref, v1_ref, v2_ref = refs[16:20]
        at_o, rt_o, bt_o, kt_o, bh_o, kh_o, vb_o, wc_o, bonus_o, g_o = refs[20:]
    else:
        at_o, rt_o, bt_o, kt_o, bh_o, kh_o, vb_o, wc_o, bonus_o, g_o, v_o = refs[16:]
    i = pl.program_id(0)
    tm = x_ref.shape[0]
    h = _rms(x_ref[...], g_ref[...])
    hp = _rms(xp_ref[...], g_ref[...])[HALO - 1:HALO] * (i > 0).astype(F32)
    row = lax.broadcasted_iota(jnp.int32, h.shape, 0)
    shifted = jnp.where(row == 0, jnp.broadcast_to(hp, h.shape), pltpu.roll(h, 1, 0))
    xx = shifted - h
    mix = mix_ref[...]
    xm = lambda n: (h + xx * mix[n:n + 1]).astype(BF16)
    xv = xm(2)
    r = _dot(xm(0), wrkv_ref[0])
    k = _dot(xm(1), wrkv_ref[1])
    v = _dot(xv, wrkv_ref[2])
    lora_w = _dot(jnp.tanh(_dot(xm(3), w1_ref[...])).astype(BF16), w2_ref[...])
    z = -(w0_ref[...] + lora_w)
    softplus = jnp.maximum(z, 0.0) + jnp.log(1.0 + jnp.exp(-jnp.abs(z)))
    lw = -jnp.exp(-softplus - 0.5)
    a = jax.nn.sigmoid(a0_ref[...] + _dot(_dot(xm(4), a1_ref[...]).astype(BF16), a2_ref[...]))
    g_o[...] = _dot(jax.nn.sigmoid(_dot(xm(5), g1_ref[...])).astype(BF16), g2_ref[...])
    if has_vres:
        gate_v = jax.nn.sigmoid(v0_ref[...] + _dot(_dot(xv, v1_ref[...]).astype(BF16), v2_ref[...]))
        v = v + (vf_ref[...] - v) * gate_v
    else:
        v_o[...] = v
    kk = k * kk_ref[...]
    kk = kk / jnp.maximum(jnp.sqrt(_group_sum(kk * kk)), 1e-12)
    k = k * (1.0 + (a - 1.0) * ka_ref[...])
    bonus_o[...] = _group_sum(r * k * rk_ref[...]) * v
    C = WKV_CHUNK
    rr = lax.broadcasted_iota(jnp.int32, (tm, tm), 0)
    cc = lax.broadcasted_iota(jnp.int32, (tm, tm), 1)
    tri = (((rr // C) == (cc // C)) & (cc <= rr)).astype(BF16)
    hi = lw.astype(BF16)
    rem = lw - hi.astype(F32)
    mid = rem.astype(BF16)
    lo = (rem - mid.astype(F32)).astype(BF16)
    cl = _dot(tri, hi) + _dot(tri, mid) + _dot(tri, lo)
    ends = [cl[c * C + C - 1:c * C + C] for c in range(tm // C)]
    cl_end = jnp.concatenate([jnp.broadcast_to(e, (C, e.shape[1])) for e in ends], axis=0)
    e_neg = jnp.exp(-cl)
    e_rem = jnp.exp(cl_end - cl)
    at_o[...] = (-kk * jnp.exp(cl - lw)).astype(BF16)
    rt_o[...] = (r * jnp.exp(cl)).astype(BF16)
    bt_o[...] = (kk * a * e_neg).astype(BF16)
    kt_o[...] = (k * e_neg).astype(BF16)
    bh_o[...] = (kk * a * e_rem).astype(BF16)
    kh_o[...] = (k * e_rem).astype(BF16)
    vb_o[...] = v.astype(BF16)
    for c, e in enumerate(ends):
        wc_o[c] = jnp.broadcast_to(jnp.exp(e), wc_o.shape[1:])


def _rwkv_pre(x, g, p, v_first):
    s, d = x.shape
    tm = _row_tile(s, WKV_TILE)
    row = pl.BlockSpec((tm, d), lambda i: (i, 0))
    prev = pl.BlockSpec((HALO, d), lambda i: (jnp.maximum(i * (tm // HALO) - 1, 0), 0))
    names = ["mix", "w_rkv", "w0", "w1", "w2", "a0", "a1", "a2", "g1", "g2", "k_k", "k_a", "r_k"]
    args = [x, x, g] + [p[n] for n in names]
    specs = [row, prev, _full(g.shape)] + [_full(p[n].shape) for n in names]
    has_vres = v_first is not None
    if has_vres:
        args += [v_first, p["v0"], p["v1"], p["v2"]]
        specs += [row] + [_full(p[n].shape) for n in ("v0", "v1", "v2")]
    n_ch = tm // WKV_CHUNK
    wc_spec = pl.BlockSpec((n_ch, SUBLANES, d), lambda i: (i, 0, 0))
    out_specs = [row] * 7 + [wc_spec, row, row]
    out_shape = ([jax.ShapeDtypeStruct((s, d), BF16)] * 7
                 + [jax.ShapeDtypeStruct((s // WKV_CHUNK, SUBLANES, d), F32)]
                 + [jax.ShapeDtypeStruct((s, d), F32)] * 2)
    if not has_vres:
        out_specs.append(row)
        out_shape.append(jax.ShapeDtypeStruct((s, d), F32))
    return pl.pallas_call(
        functools.partial(_rwkv_pre_kernel, has_vres=has_vres),
        grid=(s // tm,),
        in_specs=specs,
        out_specs=out_specs,
        out_shape=out_shape,
        compiler_params=_params("parallel"),
        name="rwkv_pre",
    )(*args)


def _wkv_kernel(at_ref, rt_ref, bt_ref, kt_ref, bh_ref, kh_ref, v_ref, wc_ref, bonus_ref,
                lnw_ref, lnb_ref, o_ref, s_ref, *, group, prec):
    C = WKV_CHUNK
    C2 = 2 * C
    n_pairs = o_ref.shape[1] // LANES
    n_groups = o_ref.shape[0] // (C * group)

    @pl.when(pl.program_id(0) == 0)
    def _():
        s_ref[...] = jnp.zeros(s_ref.shape, F32)

    lane = lax.broadcasted_iota(jnp.int32, (1, LANES), 1)
    head0 = lane < RWKV_HEAD
    m0 = head0.astype(F32)
    m1 = 1.0 - m0
    t2 = lax.broadcasted_iota(jnp.int32, (C2, C2), 0)
    s2 = lax.broadcasted_iota(jnp.int32, (C2, C2), 1)
    same_head = (t2 // C) == (s2 // C)
    strict = (same_head & (s2 < t2)).astype(F32)
    incl = (same_head & (s2 <= t2)).astype(F32)
    eye = (t2 == s2).astype(F32)

    def level_mask(b):
        return (((t2 // (2 * b)) == (s2 // (2 * b))) & ((t2 % (2 * b)) >= b)
                & ((s2 % (2 * b)) < b)).astype(F32)

    strict_m, incl_m = strict, incl
    levels = []
    b_sz = 1
    while b_sz < C:
        levels.append(level_mask(b_sz))
        b_sz *= 2

    def stack(x):
        zero = jnp.zeros_like(x)
        return jnp.concatenate([jnp.where(head0, x, zero), jnp.where(head0, zero, x)], axis=0)

    def twice(x):
        return jnp.concatenate([x, x], axis=0)

    def cast(x):
        return x.astype(prec)

    def group_body(g, carry):
        probs = [(ci, p) for ci in range(group) for p in range(n_pairs)]
        offs = [pl.multiple_of((g * group + ci) * C, C) for ci in range(group)]

        def load(ref, ci, p):
            return cast(ref[pl.ds(offs[ci], C), p * LANES:(p + 1) * LANES])

        a_s = [stack(load(at_ref, ci, p)) for ci, p in probs]
        r_s = [stack(load(rt_ref, ci, p)) for ci, p in probs]
        v_s = [stack(load(v_ref, ci, p)) for ci, p in probs]
        hat = [jnp.concatenate([stack(load(bh_ref, ci, p)), stack(load(kh_ref, ci, p))], axis=0)
               for ci, p in probs]
        til = [jnp.concatenate([twice(load(bt_ref, ci, p)), twice(load(kt_ref, ci, p))], axis=0)
               for ci, p in probs]
        lm = [_dot_nt(jnp.concatenate([a, r], axis=0), t) for a, r, t in zip(a_s, r_s, til)]
        l_ab = [x[:C2, :C2] * strict_m for x in lm]
        l_ak = [cast(x[:C2, C2:] * strict_m) for x in lm]
        m_cat = [cast(jnp.concatenate([x[C2:, :C2] * incl_m, x[C2:, C2:] * incl_m], axis=1))
                 for x in lm]
        t_inv = [eye + l * levels[0] for l in l_ab]
        for lvl in levels[1:]:
            t_b = [cast(t) for t in t_inv]
            z = [_dot(t, cast(l * lvl)) for t, l in zip(t_b, l_ab)]
            t_inv = [t + _dot(cast(zz), tb) for t, zz, tb in zip(t_inv, z, t_b)]
        t_b = [cast(t) for t in t_inv]
        p_loc = [_dot(l, v) for l, v in zip(l_ak, v_s)]
        for ci in range(group):
            rows = pl.ds(offs[ci], C)
            c_idx = g * group + ci
            sel = [ci * n_pairs + p for p in range(n_pairs)]
            st = [s_ref[p] for p in range(n_pairs)]
            st_b = [cast(s) for s in st]
            x = [_dot_nt(a_s[i], sb) + p_loc[i] for i, sb in zip(sel, st_b)]
            u = [cast(_dot(t_b[i], cast(xx))) for i, xx in zip(sel, x)]
            uv = [jnp.concatenate([uu, v_s[i]], axis=0) for i, uu in zip(sel, u)]
            y = [_dot_nt(r_s[i], sb) + _dot(m_cat[i], w) for i, sb, w in zip(sel, st_b, uv)]
            for p in range(n_pairs):
                cols = slice(p * LANES, (p + 1) * LANES)
                w_end = wc_ref[c_idx, 0:1, cols]
                s_ref[p] = st[p] * w_end + _dot_tn(uv[p], hat[sel[p]])
                yp = y[p][:C] + y[p][C:]
                mu = (jnp.sum(yp * m0, axis=-1, keepdims=True) * m0
                      + jnp.sum(yp * m1, axis=-1, keepdims=True) * m1) * (1.0 / RWKV_HEAD)
                yc = yp - mu
                yc2 = yc * yc
                var = (jnp.sum(yc2 * m0, axis=-1, keepdims=True) * m0
                       + jnp.sum(yc2 * m1, axis=-1, keepdims=True) * m1) * (1.0 / RWKV_HEAD)
                yn = yc * lax.rsqrt(var + GN_EPS)
                o_ref[rows, cols] = yn * lnw_ref[:, cols] + lnb_ref[:, cols] + bonus_ref[rows, cols]
        return carry

    lax.fori_loop(0, n_groups, group_body, 0)


def _wkv(at, rt, bt, kt, bh, kh, vb, wc, bonus, ln_w, ln_b, prec=BF16):
    s, d = at.shape
    tt = _row_tile(s, WKV_STEP)
    group = 2
    row = pl.BlockSpec((tt, d), lambda t: (t, 0))
    return pl.pallas_call(
        functools.partial(_wkv_kernel, group=group, prec=prec),
        grid=(s // tt,),
        in_specs=[row] * 7 + [pl.BlockSpec((tt // WKV_CHUNK, SUBLANES, d), lambda t: (t, 0, 0)),
                              row, _full(ln_w.shape), _full(ln_b.shape)],
        out_specs=row,
        out_shape=jax.ShapeDtypeStruct((s, d), F32),
        scratch_shapes=[pltpu.VMEM((d // LANES, LANES, LANES), F32)],
        compiler_params=_params("arbitrary"),
        name="wkv7_scan",
    )(at, rt, bt, kt, bh, kh, vb, wc, bonus, ln_w, ln_b)


def _rot_cols(w):
    half = QK_ROPE_DIM // 2
    return jnp.concatenate([-w[..., half:], w[..., :half]], axis=-1)


def _mla_weights(w_in, w_uq, w_ukv):
    d = w_in.shape[0]
    lat = Q_LORA_RANK + KV_LORA_RANK
    w_kr = w_in[:, lat:]
    w_in_e = jnp.concatenate([w_in[:, :lat], w_kr, _rot_cols(w_kr)], axis=1).astype(BF16)
    uq = w_uq.reshape(Q_LORA_RANK, MLA_HEADS, QK_DIM)
    uq_rope = uq[:, :, QK_NOPE_DIM:]
    w_uq_e = jnp.concatenate(
        [uq[:, :, :QK_NOPE_DIM].reshape(Q_LORA_RANK, -1),
         uq_rope.reshape(Q_LORA_RANK, -1),
         _rot_cols(uq_rope).reshape(Q_LORA_RANK, -1)], axis=1).astype(BF16)
    ukv = w_ukv.reshape(KV_LORA_RANK, MLA_HEADS, QK_NOPE_DIM + V_HEAD_DIM)
    w_ukv_p = jnp.concatenate(
        [ukv[:, :, :QK_NOPE_DIM].reshape(KV_LORA_RANK, -1),
         ukv[:, :, QK_NOPE_DIM:].reshape(KV_LORA_RANK, -1)], axis=1).astype(BF16)
    del d
    return w_in_e, w_uq_e, w_ukv_p


def _mla_layer(x, g, cos_t, sin_t, w_in, q_norm, w_uq, kv_norm, w_ukv):
    w_in_e, w_uq_e, w_ukv_p = _mla_weights(w_in, w_uq, w_ukv)
    q_cat, k_cat, v = _mla_pre(x, g[None], w_in_e, q_norm[None], w_uq_e, kv_norm[None],
                               w_ukv_p, cos_t, sin_t)
    return _attention(q_cat, k_cat, v)


def _rwkv_layer(x, g, v_first, p, wkv_prec=BF16):
    row = lambda t: t.reshape(1, -1)
    pp = {
        "mix": p["mix"], "w_rkv": p["w_rkv"].astype(BF16),
        "w0": row(p["w0"]), "w1": p["w1"].astype(BF16), "w2": p["w2"].astype(BF16),
        "a0": row(p["a0"]), "a1": p["a1"].astype(BF16), "a2": p["a2"].astype(BF16),
        "g1": p["g1"].astype(BF16), "g2": p["g2"].astype(BF16),
        "k_k": row(p["k_k"]), "k_a": row(p["k_a"]), "r_k": row(p["r_k"]),
    }
    if v_first is not None:
        pp.update(v0=row(p["v0"]), v1=p["v1"].astype(BF16), v2=p["v2"].astype(BF16))
    outs = _rwkv_pre(x, g[None], pp, v_first)
    scan_in, gate = outs[:9], outs[9]
    z = _wkv(*scan_in, row(p["ln_w"]), row(p["ln_b"]), prec=wkv_prec)
    return z, gate, (outs[10] if v_first is None else v_first)


def _mix_ffn_layer(x, z, gate, w_o, g, w_in, conv_w, conv_b, w_out, final_gain=None):
    return _mix_ffn(x, z, gate, w_o.astype(BF16), g[None], w_in.astype(BF16), conv_w, conv_b[None],
                    w_out.astype(BF16), None if final_gain is None else final_gain[None])


def kernel(x, positions, norm_mix, norm_ffn, norm_final, mla_w_in, mla_q_norm, mla_w_uq, mla_kv_norm, mla_w_ukv, mla_w_o, rwkv_mix, rwkv_w_rkv, rwkv_w0, rwkv_w1, rwkv_w2, rwkv_a0, rwkv_a1, rwkv_a2, rwkv_v0, rwkv_v1, rwkv_v2, rwkv_g1, rwkv_g2, rwkv_k_k, rwkv_k_a, rwkv_r_k, rwkv_ln_w, rwkv_ln_b, rwkv_w_o, ffn_w_in, ffn_conv_w, ffn_conv_b, ffn_w_out):
    batch, seq, d = x.shape
    assert batch == 1
    depth = norm_mix.shape[0]
    half = QK_ROPE_DIM // 2
    inv_freq = ROPE_THETA ** (-jnp.arange(half, dtype=F32) / half)
    freq_row = jnp.tile(inv_freq, LANES // half)[None]
    cos_t, sin_t = _rope_tables(positions.astype(F32).reshape(seq, 1), freq_row)
    xs = x.reshape(seq, d)
    v_first = None
    for i in range(depth):
        j = i // 2
        if i % 2 == 0:
            z = _mla_layer(xs, norm_mix[i], cos_t, sin_t, mla_w_in[j], mla_q_norm[j], mla_w_uq[j],
                           mla_kv_norm[j], mla_w_ukv[j])
            gate, w_o = None, mla_w_o[j]
        else:
            p = {"mix": rwkv_mix[j], "w_rkv": rwkv_w_rkv[j], "w0": rwkv_w0[j], "w1": rwkv_w1[j],
                 "w2": rwkv_w2[j], "a0": rwkv_a0[j], "a1": rwkv_a1[j], "a2": rwkv_a2[j],
                 "g1": rwkv_g1[j], "g2": rwkv_g2[j], "k_k": rwkv_k_k[j], "k_a": rwkv_k_a[j],
                 "r_k": rwkv_r_k[j], "ln_w": rwkv_ln_w[j], "ln_b": rwkv_ln_b[j]}
            if j > 0:
                p.update(v0=rwkv_v0[j - 1], v1=rwkv_v1[j - 1], v2=rwkv_v2[j - 1])
            z, gate, v_first = _rwkv_layer(xs, norm_mix[i], v_first, p)
            w_o = rwkv_w_o[j]
        xs = _mix_ffn_layer(xs, z, gate, w_o, norm_ffn[i], ffn_w_in[i], ffn_conv_w[i], ffn_conv_b[i],
                            ffn_w_out[i], norm_final if i == depth - 1 else None)
    return xs.reshape(batch, seq, d)
```

```python
import functools
import math

import jax
import jax.numpy as jnp
from jax import lax
from jax.experimental import pallas as pl
from jax.experimental.pallas import tpu as pltpu

NORM_EPS = 1e-6
GN_EPS = 64e-5
MLA_HEADS = 8
QK_NOPE_DIM = 128
QK_ROPE_DIM = 64
QK_DIM = QK_NOPE_DIM + QK_ROPE_DIM
QK_PAD = 256
V_HEAD_DIM = 128
V_ROWS = V_HEAD_DIM + 16
Q_LORA_RANK = 384
KV_LORA_RANK = 256
ROPE_THETA = 10000.0
RWKV_HEAD = 64
LANES = 128
D_FF = 2816
HALO = 16
WKV_CHUNK = 64
WKV_TILE = 256
WKV_STEP = 512
SUBLANES = 8
ATTN_TK = 512
ATTN_CHAINS = 4

BF16 = jnp.bfloat16
F32 = jnp.float32
VMEM_LIMIT = 56 * 1024 * 1024


def _params(*sem):
    return pltpu.CompilerParams(dimension_semantics=sem, vmem_limit_bytes=VMEM_LIMIT)


def _dot(a, b, precision=None):
    return jnp.dot(a, b, preferred_element_type=F32, precision=precision)


def _dot_nt(a, b, precision=None):
    return lax.dot_general(a, b, (((1,), (1,)), ((), ())),
                           preferred_element_type=F32, precision=precision)


def _dot_tn(a, b, precision=None):
    return lax.dot_general(a, b, (((0,), (0,)), ((), ())),
                           preferred_element_type=F32, precision=precision)


def _rms(x, g):
    return x * lax.rsqrt(jnp.mean(x * x, axis=-1, keepdims=True) + NORM_EPS) * g


def _row_tile(s, want):
    return min(want, s)


def _full(shape):
    return pl.BlockSpec(shape, lambda *_: (0,) * len(shape))


def _rope_table_kernel(pos_ref, freq_ref, cos_ref, sin_ref):
    ang = pos_ref[...] * freq_ref[...]
    cos_ref[...] = jnp.cos(ang)
    sin_ref[...] = jnp.sin(ang)


def _rope_tables(pos_col, freq_row):
    s = pos_col.shape[0]
    tm = _row_tile(s, 1024)
    return pl.pallas_call(
        _rope_table_kernel,
        grid=(s // tm,),
        in_specs=[pl.BlockSpec((tm, 1), lambda i: (i, 0)), _full((1, LANES))],
        out_specs=[pl.BlockSpec((tm, LANES), lambda i: (i, 0))] * 2,
        out_shape=[jax.ShapeDtypeStruct((s, LANES), F32)] * 2,
        compiler_params=_params("parallel"),
        name="rope_tables",
    )(pos_col, freq_row)


def _mla_pre_kernel(x_ref, g_ref, w_in_ref, qn_ref, w_uq_ref, kvn_ref, w_ukv_ref,
                    cos_ref, sin_ref, q_ref, k_ref, v_ref):
    h = _rms(x_ref[...], g_ref[...]).astype(BF16)
    lat = _dot(h, w_in_ref[...])
    q_lat = lat[:, :Q_LORA_RANK]
    c_kv = lat[:, Q_LORA_RANK:Q_LORA_RANK + KV_LORA_RANK]
    kr = lat[:, Q_LORA_RANK + KV_LORA_RANK:]
    cos = cos_ref[...]
    sin = sin_ref[...]
    k_rope = (kr[:, :QK_ROPE_DIM] * cos[:, :QK_ROPE_DIM]
              + kr[:, QK_ROPE_DIM:] * sin[:, :QK_ROPE_DIM])
    qn = _rms(q_lat, qn_ref[...]).astype(BF16)
    q_all = _dot(qn, w_uq_ref[...])
    n_nope = MLA_HEADS * QK_NOPE_DIM
    n_rope = MLA_HEADS * QK_ROPE_DIM
    reps = n_rope // LANES
    qscale = QK_DIM ** -0.5 * math.log2(math.e)
    q_rope = (q_all[:, n_nope:n_nope + n_rope] * jnp.tile(cos, (1, reps))
              + q_all[:, n_nope + n_rope:] * jnp.tile(sin, (1, reps))) * qscale
    kvn = _rms(c_kv, kvn_ref[...]).astype(BF16)
    kv = _dot(kvn, w_ukv_ref[...])
    tm = x_ref.shape[0]
    qn_t = (q_all[:, :n_nope] * qscale).T.astype(BF16)
    qr_t = q_rope.T.astype(BF16)
    v_t = kv[:, n_nope:].T.astype(BF16)
    zpad_t = jnp.zeros((QK_PAD - QK_DIM, tm), BF16)
    ones_t = jnp.ones((V_ROWS - V_HEAD_DIM, tm), BF16)
    zpad = jnp.zeros((tm, QK_PAD - QK_DIM), BF16)
    k_rope_b = k_rope.astype(BF16)
    for hd in range(MLA_HEADS):
        q_ref[hd, 0:QK_NOPE_DIM, :] = qn_t[hd * QK_NOPE_DIM:(hd + 1) * QK_NOPE_DIM]
        q_ref[hd, QK_NOPE_DIM:QK_DIM, :] = qr_t[hd * QK_ROPE_DIM:(hd + 1) * QK_ROPE_DIM]
        q_ref[hd, QK_DIM:QK_PAD, :] = zpad_t
        k_ref[hd, :, 0:QK_NOPE_DIM] = kv[:, hd * QK_NOPE_DIM:(hd + 1) * QK_NOPE_DIM].astype(BF16)
        k_ref[hd, :, QK_NOPE_DIM:QK_DIM] = k_rope_b
        k_ref[hd, :, QK_DIM:QK_PAD] = zpad
        v_ref[hd, 0, 0:V_HEAD_DIM, :] = v_t[hd * V_HEAD_DIM:(hd + 1) * V_HEAD_DIM]
        v_ref[hd, 0, V_HEAD_DIM:V_ROWS, :] = ones_t


def _mla_pre(x, g, w_in_e, q_norm, w_uq_e, kv_norm, w_ukv_p, cos_t, sin_t):
    s, d = x.shape
    tm = _row_tile(s, ATTN_TK)
    row = lambda w: pl.BlockSpec((tm, w), lambda i: (i, 0))
    return pl.pallas_call(
        _mla_pre_kernel,
        grid=(s // tm,),
        in_specs=[row(d), _full(g.shape), _full(w_in_e.shape), _full(q_norm.shape),
                  _full(w_uq_e.shape), _full(kv_norm.shape), _full(w_ukv_p.shape),
                  row(LANES), row(LANES)],
        out_specs=[pl.BlockSpec((MLA_HEADS, QK_PAD, tm), lambda i: (0, 0, i)),
                   pl.BlockSpec((MLA_HEADS, tm, QK_PAD), lambda i: (0, i, 0)),
                   pl.BlockSpec((MLA_HEADS, 1, V_ROWS, tm), lambda i: (0, i, 0, 0))],
        out_shape=[jax.ShapeDtypeStruct((MLA_HEADS, QK_PAD, s), BF16),
                   jax.ShapeDtypeStruct((MLA_HEADS, s, QK_PAD), BF16),
                   jax.ShapeDtypeStruct((MLA_HEADS, s // tm, V_ROWS, tm), BF16)],
        compiler_params=_params("parallel"),
        name="mla_pre",
    )(x, g, w_in_e, q_norm, w_uq_e, kv_norm, w_ukv_p, cos_t, sin_t)


def _attn_kernel(q_ref, k_ref, v_ref, o_ref, *scratch, tk, n_chains):
    qi = pl.program_id(1)
    accs = scratch[:n_chains]
    sbuf = [scratch[n_chains + 2 * c:n_chains + 2 * c + 2] for c in range(n_chains)]
    for acc in accs:
        acc[...] = jnp.zeros(acc.shape, F32)

    def scores(task):
        c, j, par, _ = task
        q_t = q_ref[:, c * tk:(c + 1) * tk]
        off = pl.multiple_of(j * tk, tk)
        s = _dot(k_ref[pl.ds(off, tk), :], q_t)
        sbuf[c][par][...] = s
        return jnp.max(s, axis=0, keepdims=True)

    def absorb(task, m_blk, m_old):
        c, j, par, masked = task
        s = sbuf[c][par][...]
        if masked:
            key = lax.broadcasted_iota(jnp.int32, s.shape, 0)
            qry = lax.broadcasted_iota(jnp.int32, s.shape, 1)
            s = jnp.where(key <= qry, s, jnp.finfo(F32).min)
            m_blk = jnp.max(s, axis=0, keepdims=True)
        m_new = jnp.maximum(m_old, m_blk)
        p = jnp.exp2(s - m_new)
        alpha = jnp.exp2(m_old - m_new)
        accs[c][...] = alpha * accs[c][...] + _dot(v_ref[j], p.astype(BF16))
        return m_new

    def run(tasks, lookahead, ms, pending):
        ms = list(ms)
        for i, task in enumerate(tasks):
            nxt = tasks[i + 1] if i + 1 < len(tasks) else lookahead
            staged = scores(nxt) if nxt is not None else None
            ms[task[0]] = absorb(task, pending, ms[task[0]])
            pending = staged
        return ms, pending

    chains = range(n_chains)
    n_full = n_chains * qi

    def body(jj, carry):
        j = 2 * jj
        tasks = [(c, j, 0, False) for c in chains] + [(c, j + 1, 1, False) for c in chains]
        ms, pending = run(tasks, (0, j + 2, 0, False), carry[:-1], carry[-1])
        return (*ms, pending)

    m0 = jnp.full((1, tk), -1e30, F32)
    carry = lax.fori_loop(0, n_full // 2, body, (*[m0] * n_chains, scores((0, 0, 0, False))))
    tail = [(c, n_full + kb, kb % 2, c == kb) for kb in chains for c in range(kb, n_chains)]
    run(tail, None, carry[:-1], carry[-1])
    for c, acc in enumerate(accs):
        out = acc[0:V_HEAD_DIM, :] / acc[V_HEAD_DIM:V_HEAD_DIM + 1, :]
        o_ref[c * tk:(c + 1) * tk, :] = out.T.astype(o_ref.dtype)


def _attention(q_t, k_cat, v_t):
    nh, s, _ = k_cat.shape
    tk = v_t.shape[-1]
    n_chains = min(ATTN_CHAINS, s // tk)
    assert n_chains % 2 == 0
    tq = n_chains * tk
    return pl.pallas_call(
        functools.partial(_attn_kernel, tk=tk, n_chains=n_chains),
        grid=(nh, s // tq),
        in_specs=[pl.BlockSpec((None, QK_PAD, tq), lambda h, i: (h, 0, i)),
                  pl.BlockSpec((None, s, QK_PAD), lambda h, i: (h, 0, 0)),
                  pl.BlockSpec((None, s // tk, V_ROWS, tk), lambda h, i: (h, 0, 0, 0))],
        out_specs=pl.BlockSpec((tq, V_HEAD_DIM), lambda h, i: (i, h)),
        out_shape=jax.ShapeDtypeStruct((s, nh * V_HEAD_DIM), BF16),
        scratch_shapes=([pltpu.VMEM((V_ROWS, tk), F32)] * n_chains
                        + [pltpu.VMEM((tk, tk), F32)] * (2 * n_chains)),
        compiler_params=_params("parallel", "arbitrary"),
        name="mla_attention",
    )(q_t, k_cat, v_t)


FFN_CHUNKS = ((0, 1024), (1024, 2048), (2048, D_FF))


def _mix_ffn_kernel(*refs, gated, final):
    refs = list(refs)
    o_ref = refs.pop()
    fg_ref = refs.pop() if final else None
    x_ref, xp_ref, z_ref, zp_ref = refs[:4]
    gt_ref, gtp_ref = refs[4:6] if gated else (None, None)
    wo_ref, g_ref, wi_ref, cw_ref, cb_ref, wout_ref = refs[-6:]
    i = pl.program_id(0)

    def mixed(xr, zr, gr):
        z = zr[...]
        if gated:
            z = z * gr[...]
        return xr[...] + _dot(z.astype(BF16), wo_ref[...])

    x1 = mixed(x_ref, z_ref, gt_ref)
    x1p = mixed(xp_ref, zp_ref, gtp_ref)
    keep = (i > 0).astype(F32)
    hn = jnp.concatenate([(_rms(x1p, g_ref[...]) * keep).astype(BF16),
                          _rms(x1, g_ref[...]).astype(BF16)], axis=0)
    acc = x1
    for lo, hi in FFN_CHUNKS:
        gate = _dot(hn, wi_ref[:, lo:hi])
        up = _dot(hn[HALO:], wi_ref[:, D_FF + lo:D_FF + hi])
        conv = (gate[HALO:] * cw_ref[2:3, lo:hi]
                + pltpu.roll(gate, 1, 0)[HALO:] * cw_ref[1:2, lo:hi]
                + pltpu.roll(gate, 2, 0)[HALO:] * cw_ref[0:1, lo:hi] + cb_ref[:, lo:hi])
        act = (conv * jax.nn.sigmoid(conv) * up).astype(BF16)
        acc = acc + _dot(act, wout_ref[lo:hi, :])
    o_ref[...] = _rms(acc, fg_ref[...]) if final else acc


def _mix_ffn(x, z, gate, w_o, g, w_in, conv_w, conv_b, w_out, final_gain=None):
    s, d = x.shape
    tm = _row_tile(s, 512)
    row = pl.BlockSpec((tm, d), lambda i: (i, 0))
    prev = pl.BlockSpec((HALO, d), lambda i: (jnp.maximum(i * (tm // HALO) - 1, 0), 0))
    const = lambda a: pl.BlockSpec(a.shape, lambda i: (0,) * a.ndim, pipeline_mode=pl.Buffered(1))
    gated = gate is not None
    final = final_gain is not None
    consts = [w_o, g, w_in, conv_w, conv_b, w_out] + ([final_gain] if final else [])
    args = [x, x, z, z] + ([gate, gate] if gated else []) + consts
    specs = [row, prev, row, prev] + ([row, prev] if gated else []) + [const(a) for a in consts]
    return pl.pallas_call(
        functools.partial(_mix_ffn_kernel, gated=gated, final=final),
        grid=(s // tm,),
        in_specs=specs,
        out_specs=row,
        out_shape=jax.ShapeDtypeStruct((s, d), F32),
        compiler_params=_params("parallel"),
        name="mix_ffn",
    )(*args)


def _group_sum(x):
    w = 2 * LANES
    r = lax.broadcasted_iota(jnp.int32, (w, w), 0) // RWKV_HEAD
    c = lax.broadcasted_iota(jnp.int32, (w, w), 1) // RWKV_HEAD
    ones_bd = (r == c).astype(BF16)
    hi = x.astype(BF16)
    lo = (x - hi.astype(F32)).astype(BF16)
    cols = [_dot(hi[:, j:j + w], ones_bd) + _dot(lo[:, j:j + w], ones_bd)
            for j in range(0, x.shape[1], w)]
    return cols[0] if len(cols) == 1 else jnp.concatenate(cols, axis=1)


def _rwkv_pre_kernel(*refs, has_vres):
    (x_ref, xp_ref, g_ref, mix_ref, wrkv_ref, w0_ref, w1_ref, w2_ref, a0_ref, a1_ref, a2_ref,
     g1_ref, g2_ref, kk_ref, ka_ref, rk_ref) = refs[:16]
    if has_vres:
        vf_ref, v0_ref, v1_ref, v2_ref = refs[16:20]
        at_o, rt_o, bt_o, kt_o, bh_o, kh_o, vb_o, wc_o, bonus_o, g_o = refs[20:]
    else:
        at_o, rt_o, bt_o, kt_o, bh_o, kh_o, vb_o, wc_o, bonus_o, g_o, v_o = refs[16:]
    i = pl.program_id(0)
    tm = x_ref.shape[0]
    h = _rms(x_ref[...], g_ref[...])
    hp = _rms(xp_ref[...], g_ref[...])[HALO - 1:HALO] * (i > 0).astype(F32)
    row = lax.broadcasted_iota(jnp.int32, h.shape, 0)
    shifted = jnp.where(row == 0, jnp.broadcast_to(hp, h.shape), pltpu.roll(h, 1, 0))
    xx = shifted - h
    mix = mix_ref[...]
    xm = lambda n: (h + xx * mix[n:n + 1]).astype(BF16)
    xv = xm(2)
    r = _dot(xm(0), wrkv_ref[0])
    k = _dot(xm(1), wrkv_ref[1])
    v = _dot(xv, wrkv_ref[2])
    lora_w = _dot(jnp.tanh(_dot(xm(3), w1_ref[...])).astype(BF16), w2_ref[...])
    z = -(w0_ref[...] + lora_w)
    softplus = jnp.maximum(z, 0.0) + jnp.log(1.0 + jnp.exp(-jnp.abs(z)))
    lw = -jnp.exp(-softplus - 0.5)
    a = jax.nn.sigmoid(a0_ref[...] + _dot(_dot(xm(4), a1_ref[...]).astype(BF16), a2_ref[...]))
    g_o[...] = _dot(jax.nn.sigmoid(_dot(xm(5), g1_ref[...])).astype(BF16), g2_ref[...])
    if has_vres:
        gate_v = jax.nn.sigmoid(v0_ref[...] + _dot(_dot(xv, v1_ref[...]).astype(BF16), v2_ref[...]))
        v = v + (vf_ref[...] - v) * gate_v
    else:
        v_o[...] = v
    kk = k * kk_ref[...]
    kk = kk / jnp.maximum(jnp.sqrt(_group_sum(kk * kk)), 1e-12)
    k = k * (1.0 + (a - 1.0) * ka_ref[...])
    bonus_o[...] = _group_sum(r * k * rk_ref[...]) * v
    C = WKV_CHUNK
    rr = lax.broadcasted_iota(jnp.int32, (tm, tm), 0)
    cc = lax.broadcasted_iota(jnp.int32, (tm, tm), 1)
    tri = (((rr // C) == (cc // C)) & (cc <= rr)).astype(BF16)
    hi = lw.astype(BF16)
    rem = lw - hi.astype(F32)
    mid = rem.astype(BF16)
    lo = (rem - mid.astype(F32)).astype(BF16)
    cl = _dot(tri, hi) + _dot(tri, mid) + _dot(tri, lo)
    ends = [cl[c * C + C - 1:c * C + C] for c in range(tm // C)]
    cl_end = jnp.concatenate([jnp.broadcast_to(e, (C, e.shape[1])) for e in ends], axis=0)
    e_neg = jnp.exp(-cl)
    e_rem = jnp.exp(cl_end - cl)
    at_o[...] = (-kk * jnp.exp(cl - lw)).astype(BF16)
    rt_o[...] = (r * jnp.exp(cl)).astype(BF16)
    bt_o[...] = (kk * a * e_neg).astype(BF16)
    kt_o[...] = (k * e_neg).astype(BF16)
    bh_o[...] = (kk * a * e_rem).astype(BF16)
    kh_o[...] = (k * e_rem).astype(BF16)
    vb_o[...] = v.astype(BF16)
    for c, e in enumerate(ends):
        wc_o[c] = jnp.broadcast_to(jnp.exp(e), wc_o.shape[1:])


def _rwkv_pre(x, g, p, v_first):
    s, d = x.shape
    tm = _row_tile(s, WKV_TILE)
    row = pl.BlockSpec((tm, d), lambda i: (i, 0))
    prev = pl.BlockSpec((HALO, d), lambda i: (jnp.maximum(i * (tm // HALO) - 1, 0), 0))
    names = ["mix", "w_rkv", "w0", "w1", "w2", "a0", "a1", "a2", "g1", "g2", "k_k", "k_a", "r_k"]
    args = [x, x, g] + [p[n] for n in names]
    specs = [row, prev, _full(g.shape)] + [_full(p[n].shape) for n in names]
    has_vres = v_first is not None
    if has_vres:
        args += [v_first, p["v0"], p["v1"], p["v2"]]
        specs += [row] + [_full(p[n].shape) for n in ("v0", "v1", "v2")]
    n_ch = tm // WKV_CHUNK
    wc_spec = pl.BlockSpec((n_ch, SUBLANES, d), lambda i: (i, 0, 0))
    out_specs = [row] * 7 + [wc_spec, row, row]
    out_shape = ([jax.ShapeDtypeStruct((s, d), BF16)] * 7
                 + [jax.ShapeDtypeStruct((s // WKV_CHUNK, SUBLANES, d), F32)]
                 + [jax.ShapeDtypeStruct((s, d), F32)] * 2)
    if not has_vres:
        out_specs.append(row)
        out_shape.append(jax.ShapeDtypeStruct((s, d), F32))
    return pl.pallas_call(
        functools.partial(_rwkv_pre_kernel, has_vres=has_vres),
        grid=(s // tm,),
        in_specs=specs,
        out_specs=out_specs,
        out_shape=out_shape,
        compiler_params=_params("parallel"),
        name="rwkv_pre",
    )(*args)


def _wkv_kernel(at_ref, rt_ref, bt_ref, kt_ref, bh_ref, kh_ref, v_ref, wc_ref, bonus_ref,
                lnw_ref, lnb_ref, o_ref, s_ref, *, group, prec):
    C = WKV_CHUNK
    C2 = 2 * C
    n_pairs = o_ref.shape[1] // LANES
    n_groups = o_ref.shape[0] // (C * group)

    @pl.when(pl.program_id(0) == 0)
    def _():
        s_ref[...] = jnp.zeros(s_ref.shape, F32)

    lane = lax.broadcasted_iota(jnp.int32, (1, LANES), 1)
    head0 = lane < RWKV_HEAD
    m0 = head0.astype(F32)
    m1 = 1.0 - m0
    t2 = lax.broadcasted_iota(jnp.int32, (C2, C2), 0)
    s2 = lax.broadcasted_iota(jnp.int32, (C2, C2), 1)
    same_head = (t2 // C) == (s2 // C)
    strict = (same_head & (s2 < t2)).astype(F32)
    incl = (same_head & (s2 <= t2)).astype(F32)
    eye = (t2 == s2).astype(F32)

    def level_mask(b):
        return (((t2 // (2 * b)) == (s2 // (2 * b))) & ((t2 % (2 * b)) >= b)
                & ((s2 % (2 * b)) < b)).astype(F32)

    strict_m, incl_m = strict, incl
    levels = []
    b_sz = 1
    while b_sz < C:
        levels.append(level_mask(b_sz))
        b_sz *= 2

    def stack(x):
        zero = jnp.zeros_like(x)
        return jnp.concatenate([jnp.where(head0, x, zero), jnp.where(head0, zero, x)], axis=0)

    def twice(x):
        return jnp.concatenate([x, x], axis=0)

    def cast(x):
        return x.astype(prec)

    def group_body(g, carry):
        probs = [(ci, p) for ci in range(group) for p in range(n_pairs)]
        offs = [pl.multiple_of((g * group + ci) * C, C) for ci in range(group)]

        def load(ref, ci, p):
            return cast(ref[pl.ds(offs[ci], C), p * LANES:(p + 1) * LANES])

        a_s = [stack(load(at_ref, ci, p)) for ci, p in probs]
        r_s = [stack(load(rt_ref, ci, p)) for ci, p in probs]
        v_s = [stack(load(v_ref, ci, p)) for ci, p in probs]
        hat = [jnp.concatenate([stack(load(bh_ref, ci, p)), stack(load(kh_ref, ci, p))], axis=0)
               for ci, p in probs]
        til = [jnp.concatenate([twice(load(bt_ref, ci, p)), twice(load(kt_ref, ci, p))], axis=0)
               for ci, p in probs]
        lm = [_dot_nt(jnp.concatenate([a, r], axis=0), t) for a, r, t in zip(a_s, r_s, til)]
        l_ab = [x[:C2, :C2] * strict_m for x in lm]
        l_ak = [cast(x[:C2, C2:] * strict_m) for x in lm]
        m_cat = [cast(jnp.concatenate([x[C2:, :C2] * incl_m, x[C2:, C2:] * incl_m], axis=1))
                 for x in lm]
        t_inv = [eye + l * levels[0] for l in l_ab]
        for lvl in levels[1:]:
            t_b = [cast(t) for t in t_inv]
            z = [_dot(t, cast(l * lvl)) for t, l in zip(t_b, l_ab)]
            t_inv = [t + _dot(cast(zz), tb) for t, zz, tb in zip(t_inv, z, t_b)]
        t_b = [cast(t) for t in t_inv]
        p_loc = [_dot(l, v) for l, v in zip(l_ak, v_s)]
        for ci in range(group):
            rows = pl.ds(offs[ci], C)
            c_idx = g * group + ci
            sel = [ci * n_pairs + p for p in range(n_pairs)]
            st = [s_ref[p] for p in range(n_pairs)]
            st_b = [cast(s) for s in st]
            x = [_dot_nt(a_s[i], sb) + p_loc[i] for i, sb in zip(sel, st_b)]
            u = [cast(_dot(t_b[i], cast(xx))) for i, xx in zip(sel, x)]
            uv = [jnp.concatenate([uu, v_s[i]], axis=0) for i, uu in zip(sel, u)]
            y = [_dot_nt(r_s[i], sb) + _dot(m_cat[i], w) for i, sb, w in zip(sel, st_b, uv)]
            for p in range(n_pairs):
                cols = slice(p * LANES, (p + 1) * LANES)
                w_end = wc_ref[c_idx, 0:1, cols]
                s_ref[p] = st[p] * w_end + _dot_tn(uv[p], hat[sel[p]])
                yp = y[p][:C] + y[p][C:]
                mu = (jnp.sum(yp * m0, axis=-1, keepdims=True) * m0
                      + jnp.sum(yp * m1, axis=-1, keepdims=True) * m1) * (1.0 / RWKV_HEAD)
                yc = yp - mu
                yc2 = yc * yc
                var = (jnp.sum(yc2 * m0, axis=-1, keepdims=True) * m0
                       + jnp.sum(yc2 * m1, axis=-1, keepdims=True) * m1) * (1.0 / RWKV_HEAD)
                yn = yc * lax.rsqrt(var + GN_EPS)
                o_ref[rows, cols] = yn * lnw_ref[:, cols] + lnb_ref[:, cols] + bonus_ref[rows, cols]
        return carry

    lax.fori_loop(0, n_groups, group_body, 0)


def _wkv(at, rt, bt, kt, bh, kh, vb, wc, bonus, ln_w, ln_b, prec=BF16):
    s, d = at.shape
    tt = _row_tile(s, WKV_STEP)
    group = 2
    row = pl.BlockSpec((tt, d), lambda t: (t, 0))
    return pl.pallas_call(
        functools.partial(_wkv_kernel, group=group, prec=prec),
        grid=(s // tt,),
        in_specs=[row] * 7 + [pl.BlockSpec((tt // WKV_CHUNK, SUBLANES, d), lambda t: (t, 0, 0)),
                              row, _full(ln_w.shape), _full(ln_b.shape)],
        out_specs=row,
        out_shape=jax.ShapeDtypeStruct((s, d), F32),
        scratch_shapes=[pltpu.VMEM((d // LANES, LANES, LANES), F32)],
        compiler_params=_params("arbitrary"),
        name="wkv7_scan",
    )(at, rt, bt, kt, bh, kh, vb, wc, bonus, ln_w, ln_b)


def _rot_cols(w):
    half = QK_ROPE_DIM // 2
    return jnp.concatenate([-w[..., half:], w[..., :half]], axis=-1)


def _mla_weights(w_in, w_uq, w_ukv):
    d = w_in.shape[0]
    lat = Q_LORA_RANK + KV_LORA_RANK
    w_kr = w_in[:, lat:]
    w_in_e = jnp.concatenate([w_in[:, :lat], w_kr, _rot_cols(w_kr)], axis=1).astype(BF16)
    uq = w_uq.reshape(Q_LORA_RANK, MLA_HEADS, QK_DIM)
    uq_rope = uq[:, :, QK_NOPE_DIM:]
    w_uq_e = jnp.concatenate(
        [uq[:, :, :QK_NOPE_DIM].reshape(Q_LORA_RANK, -1),
         uq_rope.reshape(Q_LORA_RANK, -1),
         _rot_cols(uq_rope).reshape(Q_LORA_RANK, -1)], axis=1).astype(BF16)
    ukv = w_ukv.reshape(KV_LORA_RANK, MLA_HEADS, QK_NOPE_DIM + V_HEAD_DIM)
    w_ukv_p = jnp.concatenate(
        [ukv[:, :, :QK_NOPE_DIM].reshape(KV_LORA_RANK, -1),
         ukv[:, :, QK_NOPE_DIM:].reshape(KV_LORA_RANK, -1)], axis=1).astype(BF16)
    del d
    return w_in_e, w_uq_e, w_ukv_p


def _mla_layer(x, g, cos_t, sin_t, w_in, q_norm, w_uq, kv_norm, w_ukv):
    w_in_e, w_uq_e, w_ukv_p = _mla_weights(w_in, w_uq, w_ukv)
    q_cat, k_cat, v = _mla_pre(x, g[None], w_in_e, q_norm[None], w_uq_e, kv_norm[None],
                               w_ukv_p, cos_t, sin_t)
    return _attention(q_cat, k_cat, v)


def _rwkv_layer(x, g, v_first, p, wkv_prec=BF16):
    row = lambda t: t.reshape(1, -1)
    pp = {
        "mix": p["mix"], "w_rkv": p["w_rkv"].astype(BF16),
        "w0": row(p["w0"]), "w1": p["w1"].astype(BF16), "w2": p["w2"].astype(BF16),
        "a0": row(p["a0"]), "a1": p["a1"].astype(BF16), "a2": p["a2"].astype(BF16),
        "g1": p["g1"].astype(BF16), "g2": p["g2"].astype(BF16),
        "k_k": row(p["k_k"]), "k_a": row(p["k_a"]), "r_k": row(p["r_k"]),
    }
    if v_first is not None:
        pp.update(v0=row(p["v0"]), v1=p["v1"].astype(BF16), v2=p["v2"].astype(BF16))
    outs = _rwkv_pre(x, g[None], pp, v_first)
    scan_in, gate = outs[:9], outs[9]
    z = _wkv(*scan_in, row(p["ln_w"]), row(p["ln_b"]), prec=wkv_prec)
    return z, gate, (outs[10] if v_first is None else v_first)


def _mix_ffn_layer(x, z, gate, w_o, g, w_in, conv_w, conv_b, w_out, final_gain=None):
    return _mix_ffn(x, z, gate, w_o.astype(BF16), g[None], w_in.astype(BF16), conv_w, conv_b[None],
                    w_out.astype(BF16), None if final_gain is None else final_gain[None])


def kernel(x, positions, norm_mix, norm_ffn, norm_final, mla_w_in, mla_q_norm, mla_w_uq, mla_kv_norm, mla_w_ukv, mla_w_o, rwkv_mix, rwkv_w_rkv, rwkv_w0, rwkv_w1, rwkv_w2, rwkv_a0, rwkv_a1, rwkv_a2, rwkv_v0, rwkv_v1, rwkv_v2, rwkv_g1, rwkv_g2, rwkv_k_k, rwkv_k_a, rwkv_r_k, rwkv_ln_w, rwkv_ln_b, rwkv_w_o, ffn_w_in, ffn_conv_w, ffn_conv_b, ffn_w_out):
    batch, seq, d = x.shape
    assert batch == 1
    depth = norm_mix.shape[0]
    half = QK_ROPE_DIM // 2
    inv_freq = ROPE_THETA ** (-jnp.arange(half, dtype=F32) / half)
    freq_row = jnp.tile(inv_freq, LANES // half)[None]
    cos_t, sin_t = _rope_tables(positions.astype(F32).reshape(seq, 1), freq_row)
    xs = x.reshape(seq, d)
    v_first = None
    for i in range(depth):
        j = i // 2
        if i % 2 == 0:
            z = _mla_layer(xs, norm_mix[i], cos_t, sin_t, mla_w_in[j], mla_q_norm[j], mla_w_uq[j],
                           mla_kv_norm[j], mla_w_ukv[j])
            gate, w_o = None, mla_w_o[j]
        else:
            p = {"mix": rwkv_mix[j], "w_rkv": rwkv_w_rkv[j], "w0": rwkv_w0[j], "w1": rwkv_w1[j],
                 "w2": rwkv_w2[j], "a0": rwkv_a0[j], "a1": rwkv_a1[j], "a2": rwkv_a2[j],
                 "g1": rwkv_g1[j], "g2": rwkv_g2[j], "k_k": rwkv_k_k[j], "k_a": rwkv_k_a[j],
                 "r_k": rwkv_r_k[j], "ln_w": rwkv_ln_w[j], "ln_b": rwkv_ln_b[j]}
            if j > 0:
                p.update(v0=rwkv_v0[j - 1], v1=rwkv_v1[j - 1], v2=rwkv_v2[j - 1])
            z, gate, v_first = _rwkv_layer(xs, norm_mix[i], v_first, p)
            w_o = rwkv_w_o[j]
        xs = _mix_ffn_layer(xs, z, gate, w_o, norm_ffn[i], ffn_w_in[i], ffn_conv_w[i], ffn_conv_b[i],
                            ffn_w_out[i], norm_final if i == depth - 1 else None)
    return xs.reshape(batch, seq, d)
```

```python
import functools
import math

import jax
import jax.numpy as jnp
from jax import lax
from jax.experimental import pallas as pl
from jax.experimental.pallas import tpu as pltpu

NORM_EPS = 1e-6
GN_EPS = 64e-5
MLA_HEADS = 8
QK_NOPE_DIM = 128
QK_ROPE_DIM = 64
QK_DIM = QK_NOPE_DIM + QK_ROPE_DIM
QK_PAD = 256
V_HEAD_DIM = 128
V_ROWS = V_HEAD_DIM + 16
Q_LORA_RANK = 384
KV_LORA_RANK = 256
ROPE_THETA = 10000.0
RWKV_HEAD = 64
LANES = 128
D_FF = 2816
HALO = 16
WKV_CHUNK = 64
WKV_TILE = 256
WKV_STEP = 512
SUBLANES = 8
ATTN_TK = 512
ATTN_CHAINS = 8

BF16 = jnp.bfloat16
F32 = jnp.float32
VMEM_LIMIT = 56 * 1024 * 1024


def _params(*sem):
    return pltpu.CompilerParams(dimension_semantics=sem, vmem_limit_bytes=VMEM_LIMIT)


def _dot(a, b, precision=None):
    return jnp.dot(a, b, preferred_element_type=F32, precision=precision)


def _dot_nt(a, b, precision=None):
    return lax.dot_general(a, b, (((1,), (1,)), ((), ())),
                           preferred_element_type=F32, precision=precision)


def _dot_tn(a, b, precision=None):
    return lax.dot_general(a, b, (((0,), (0,)), ((), ())),
                           preferred_element_type=F32, precision=precision)


def _rms(x, g):
    return x * lax.rsqrt(jnp.mean(x * x, axis=-1, keepdims=True) + NORM_EPS) * g


def _row_tile(s, want):
    return min(want, s)


def _full(shape):
    return pl.BlockSpec(shape, lambda *_: (0,) * len(shape))


def _rope_table_kernel(pos_ref, freq_ref, cos_ref, sin_ref):
    ang = pos_ref[...] * freq_ref[...]
    cos_ref[...] = jnp.cos(ang)
    sin_ref[...] = jnp.sin(ang)


def _rope_tables(pos_col, freq_row):
    s = pos_col.shape[0]
    tm = _row_tile(s, 1024)
    return pl.pallas_call(
        _rope_table_kernel,
        grid=(s // tm,),
        in_specs=[pl.BlockSpec((tm, 1), lambda i: (i, 0)), _full((1, LANES))],
        out_specs=[pl.BlockSpec((tm, LANES), lambda i: (i, 0))] * 2,
        out_shape=[jax.ShapeDtypeStruct((s, LANES), F32)] * 2,
        compiler_params=_params("parallel"),
        name="rope_tables",
    )(pos_col, freq_row)


def _mla_pre_kernel(x_ref, g_ref, w_in_ref, qn_ref, w_uq_ref, kvn_ref, w_ukv_ref,
                    cos_ref, sin_ref, q_ref, k_ref, v_ref):
    h = _rms(x_ref[...], g_ref[...]).astype(BF16)
    lat = _dot(h, w_in_ref[...])
    q_lat = lat[:, :Q_LORA_RANK]
    c_kv = lat[:, Q_LORA_RANK:Q_LORA_RANK + KV_LORA_RANK]
    kr = lat[:, Q_LORA_RANK + KV_LORA_RANK:]
    cos = cos_ref[...]
    sin = sin_ref[...]
    k_rope = (kr[:, :QK_ROPE_DIM] * cos[:, :QK_ROPE_DIM]
              + kr[:, QK_ROPE_DIM:] * sin[:, :QK_ROPE_DIM])
    qn = _rms(q_lat, qn_ref[...]).astype(BF16)
    q_all = _dot(qn, w_uq_ref[...])
    n_nope = MLA_HEADS * QK_NOPE_DIM
    n_rope = MLA_HEADS * QK_ROPE_DIM
    reps = n_rope // LANES
    qscale = QK_DIM ** -0.5 * math.log2(math.e)
    q_rope = (q_all[:, n_nope:n_nope + n_rope] * jnp.tile(cos, (1, reps))
              + q_all[:, n_nope + n_rope:] * jnp.tile(sin, (1, reps))) * qscale
    kvn = _rms(c_kv, kvn_ref[...]).astype(BF16)
    kv = _dot(kvn, w_ukv_ref[...])
    tm = x_ref.shape[0]
    qn_t = (q_all[:, :n_nope] * qscale).T.astype(BF16)
    qr_t = q_rope.T.astype(BF16)
    v_t = kv[:, n_nope:].T.astype(BF16)
    zpad_t = jnp.zeros((QK_PAD - QK_DIM, tm), BF16)
    ones_t = jnp.ones((V_ROWS - V_HEAD_DIM, tm), BF16)
    zpad = jnp.zeros((tm, QK_PAD - QK_DIM), BF16)
    k_rope_b = k_rope.astype(BF16)
    for hd in range(MLA_HEADS):
        q_ref[hd, 0:QK_NOPE_DIM, :] = qn_t[hd * QK_NOPE_DIM:(hd + 1) * QK_NOPE_DIM]
        q_ref[hd, QK_NOPE_DIM:QK_DIM, :] = qr_t[hd * QK_ROPE_DIM:(hd + 1) * QK_ROPE_DIM]
        q_ref[hd, QK_DIM:QK_PAD, :] = zpad_t
        k_ref[hd, :, 0:QK_NOPE_DIM] = kv[:, hd * QK_NOPE_DIM:(hd + 1) * QK_NOPE_DIM].astype(BF16)
        k_ref[hd, :, QK_NOPE_DIM:QK_DIM] = k_rope_b
        k_ref[hd, :, QK_DIM:QK_PAD] = zpad
        v_ref[hd, 0, 0:V_HEAD_DIM, :] = v_t[hd * V_HEAD_DIM:(hd + 1) * V_HEAD_DIM]
        v_ref[hd, 0, V_HEAD_DIM:V_ROWS, :] = ones_t


def _mla_pre(x, g, w_in_e, q_norm, w_uq_e, kv_norm, w_ukv_p, cos_t, sin_t):
    s, d = x.shape
    tm = _row_tile(s, ATTN_TK)
    row = lambda w: pl.BlockSpec((tm, w), lambda i: (i, 0))
    return pl.pallas_call(
        _mla_pre_kernel,
        grid=(s // tm,),
        in_specs=[row(d), _full(g.shape), _full(w_in_e.shape), _full(q_norm.shape),
                  _full(w_uq_e.shape), _full(kv_norm.shape), _full(w_ukv_p.shape),
                  row(LANES), row(LANES)],
        out_specs=[pl.BlockSpec((MLA_HEADS, QK_PAD, tm), lambda i: (0, 0, i)),
                   pl.BlockSpec((MLA_HEADS, tm, QK_PAD), lambda i: (0, i, 0)),
                   pl.BlockSpec((MLA_HEADS, 1, V_ROWS, tm), lambda i: (0, i, 0, 0))],
        out_shape=[jax.ShapeDtypeStruct((MLA_HEADS, QK_PAD, s), BF16),
                   jax.ShapeDtypeStruct((MLA_HEADS, s, QK_PAD), BF16),
                   jax.ShapeDtypeStruct((MLA_HEADS, s // tm, V_ROWS, tm), BF16)],
        compiler_params=_params("parallel"),
        name="mla_pre",
    )(x, g, w_in_e, q_norm, w_uq_e, kv_norm, w_ukv_p, cos_t, sin_t)


def _attn_kernel(q_ref, k_ref, v_ref, o_ref, *scratch, tk, n_chains):
    qi = pl.program_id(1)
    accs = scratch[:n_chains]
    sbuf = [scratch[n_chains + 2 * c:n_chains + 2 * c + 2] for c in range(n_chains)]
    for acc in accs:
        acc[...] = jnp.zeros(acc.shape, F32)

    def scores(task):
        c, j, par, _ = task
        q_t = q_ref[:, c * tk:(c + 1) * tk]
        off = pl.multiple_of(j * tk, tk)
        s = _dot(k_ref[pl.ds(off, tk), :], q_t)
        sbuf[c][par][...] = s
        return jnp.max(s, axis=0, keepdims=True)

    def absorb(task, m_blk, m_old):
        c, j, par, masked = task
        s = sbuf[c][par][...]
        if masked:
            key = lax.broadcasted_iota(jnp.int32, s.shape, 0)
            qry = lax.broadcasted_iota(jnp.int32, s.shape, 1)
            s = jnp.where(key <= qry, s, jnp.finfo(F32).min)
            m_blk = jnp.max(s, axis=0, keepdims=True)
        m_new = jnp.maximum(m_old, m_blk)
        p = jnp.exp2(s - m_new)
        alpha = jnp.exp2(m_old - m_new)
        accs[c][...] = alpha * accs[c][...] + _dot(v_ref[j], p.astype(BF16))
        return m_new

    def run(tasks, lookahead, ms, pending):
        ms = list(ms)
        for i, task in enumerate(tasks):
            nxt = tasks[i + 1] if i + 1 < len(tasks) else lookahead
            staged = scores(nxt) if nxt is not None else None
            ms[task[0]] = absorb(task, pending, ms[task[0]])
            pending = staged
        return ms, pending

    chains = range(n_chains)
    n_full = n_chains * qi

    def body(jj, carry):
        j = 2 * jj
        tasks = [(c, j, 0, False) for c in chains] + [(c, j + 1, 1, False) for c in chains]
        ms, pending = run(tasks, (0, j + 2, 0, False), carry[:-1], carry[-1])
        return (*ms, pending)

    m0 = jnp.full((1, tk), -1e30, F32)
    carry = lax.fori_loop(0, n_full // 2, body, (*[m0] * n_chains, scores((0, 0, 0, False))))
    tail = [(c, n_full + kb, kb % 2, c == kb) for kb in chains for c in range(kb, n_chains)]
    run(tail, None, carry[:-1], carry[-1])
    for c, acc in enumerate(accs):
        out = acc[0:V_HEAD_DIM, :] / acc[V_HEAD_DIM:V_HEAD_DIM + 1, :]
        o_ref[c * tk:(c + 1) * tk, :] = out.T.astype(o_ref.dtype)


def _attention(q_t, k_cat, v_t):
    nh, s, _ = k_cat.shape
    tk = v_t.shape[-1]
    n_chains = min(ATTN_CHAINS, s // tk)
    assert n_chains % 2 == 0
    tq = n_chains * tk
    return pl.pallas_call(
        functools.partial(_attn_kernel, tk=tk, n_chains=n_chains),
        grid=(nh, s // tq),
        in_specs=[pl.BlockSpec((None, QK_PAD, tq), lambda h, i: (h, 0, i)),
                  pl.BlockSpec((None, s, QK_PAD), lambda h, i: (h, 0, 0)),
                  pl.BlockSpec((None, s // tk, V_ROWS, tk), lambda h, i: (h, 0, 0, 0))],
        out_specs=pl.BlockSpec((tq, V_HEAD_DIM), lambda h, i: (i, h)),
        out_shape=jax.ShapeDtypeStruct((s, nh * V_HEAD_DIM), BF16),
        scratch_shapes=([pltpu.VMEM((V_ROWS, tk), F32)] * n_chains
                        + [pltpu.VMEM((tk, tk), F32)] * (2 * n_chains)),
        compiler_params=_params("parallel", "arbitrary"),
        name="mla_attention",
    )(q_t, k_cat, v_t)


FFN_CHUNKS = ((0, 1024), (1024, 2048), (2048, D_FF))


def _mix_ffn_kernel(*refs, gated, final):
    refs = list(refs)
    o_ref = refs.pop()
    fg_ref = refs.pop() if final else None
    x_ref, xp_ref, z_ref, zp_ref = refs[:4]
    gt_ref, gtp_ref = refs[4:6] if gated else (None, None)
    wo_ref, g_ref, wi_ref, cw_ref, cb_ref, wout_ref = refs[-6:]
    i = pl.program_id(0)

    def mixed(xr, zr, gr):
        z = zr[...]
        if gated:
            z = z * gr[...]
        return xr[...] + _dot(z.astype(BF16), wo_ref[...])

    x1 = mixed(x_ref, z_ref, gt_ref)
    x1p = mixed(xp_ref, zp_ref, gtp_ref)
    keep = (i > 0).astype(F32)
    hn = jnp.concatenate([(_rms(x1p, g_ref[...]) * keep).astype(BF16),
                          _rms(x1, g_ref[...]).astype(BF16)], axis=0)
    acc = x1
    for lo, hi in FFN_CHUNKS:
        gate = _dot(hn, wi_ref[:, lo:hi])
        up = _dot(hn[HALO:], wi_ref[:, D_FF + lo:D_FF + hi])
        conv = (gate[HALO:] * cw_ref[2:3, lo:hi]
                + pltpu.roll(gate, 1, 0)[HALO:] * cw_ref[1:2, lo:hi]
                + pltpu.roll(gate, 2, 0)[HALO:] * cw_ref[0:1, lo:hi] + cb_ref[:, lo:hi])
        act = (conv * jax.nn.sigmoid(conv) * up).astype(BF16)
        acc = acc + _dot(act, wout_ref[lo:hi, :])
    o_ref[...] = _rms(acc, fg_ref[...]) if final else acc


def _mix_ffn(x, z, gate, w_o, g, w_in, conv_w, conv_b, w_out, final_gain=None):
    s, d = x.shape
    tm = _row_tile(s, 512)
    row = pl.BlockSpec((tm, d), lambda i: (i, 0))
    prev = pl.BlockSpec((HALO, d), lambda i: (jnp.maximum(i * (tm // HALO) - 1, 0), 0))
    const = lambda a: pl.BlockSpec(a.shape, lambda i: (0,) * a.ndim, pipeline_mode=pl.Buffered(1))
    gated = gate is not None
    final = final_gain is not None
    consts = [w_o, g, w_in, conv_w, conv_b, w_out] + ([final_gain] if final else [])
    args = [x, x, z, z] + ([gate, gate] if gated else []) + consts
    specs = [row, prev, row, prev] + ([row, prev] if gated else []) + [const(a) for a in consts]
    return pl.pallas_call(
        functools.partial(_mix_ffn_kernel, gated=gated, final=final),
        grid=(s // tm,),
        in_specs=specs,
        out_specs=row,
        out_shape=jax.ShapeDtypeStruct((s, d), F32),
        compiler_params=_params("parallel"),
        name="mix_ffn",
    )(*args)


def _group_sum(x):
    w = 2 * LANES
    r = lax.broadcasted_iota(jnp.int32, (w, w), 0) // RWKV_HEAD
    c = lax.broadcasted_iota(jnp.int32, (w, w), 1) // RWKV_HEAD
    ones_bd = (r == c).astype(BF16)
    hi = x.astype(BF16)
    lo = (x - hi.astype(F32)).astype(BF16)
    cols = [_dot(hi[:, j:j + w], ones_bd) + _dot(lo[:, j:j + w], ones_bd)
            for j in range(0, x.shape[1], w)]
    return cols[0] if len(cols) == 1 else jnp.concatenate(cols, axis=1)


def _rwkv_pre_kernel(*refs, has_vres):
    (x_ref, xp_ref, g_ref, mix_ref, wrkv_ref, w0_ref, w1_ref, w2_ref, a0_ref, a1_ref, a2_ref,
     g1_ref, g2_ref, kk_ref, ka_ref, rk_ref) = refs[:16]
    if has_vres:
        vf_ref, v0_ref, v1_ref, v2_ref = refs[16:20]
        at_o, rt_o, bt_o, kt_o, bh_o, kh_o, vb_o, wc_o, bonus_o, g_o = refs[20:]
    else:
        at_o, rt_o, bt_o, kt_o, bh_o, kh_o, vb_o, wc_o, bonus_o, g_o, v_o = refs[16:]
    i = pl.program_id(0)
    tm = x_ref.shape[0]
    h = _rms(x_ref[...], g_ref[...])
    hp = _rms(xp_ref[...], g_ref[...])[HALO - 1:HALO] * (i > 0).astype(F32)
    row = lax.broadcasted_iota(jnp.int32, h.shape, 0)
    shifted = jnp.where(row == 0, jnp.broadcast_to(hp, h.shape), pltpu.roll(h, 1, 0))
    xx = shifted - h
    mix = mix_ref[...]
    xm = lambda n: (h + xx * mix[n:n + 1]).astype(BF16)
    xv = xm(2)
    r = _dot(xm(0), wrkv_ref[0])
    k = _dot(xm(1), wrkv_ref[1])
    v = _dot(xv, wrkv_ref[2])
    lora_w = _dot(jnp.tanh(_dot(xm(3), w1_ref[...])).astype(BF16), w2_ref[...])
    z = -(w0_ref[...] + lora_w)
    softplus = jnp.maximum(z, 0.0) + jnp.log(1.0 + jnp.exp(-jnp.abs(z)))
    lw = -jnp.exp(-softplus - 0.5)
    a = jax.nn.sigmoid(a0_ref[...] + _dot(_dot(xm(4), a1_ref[...]).astype(BF16), a2_ref[...]))
    g_o[...] = _dot(jax.nn.sigmoid(_dot(xm(5), g1_ref[...])).astype(BF16), g2_ref[...])
    if has_vres:
        gate_v = jax.nn.sigmoid(v0_ref[...] + _dot(_dot(xv, v1_ref[...]).astype(BF16), v2_ref[...]))
        v = v + (vf_ref[...] - v) * gate_v
    else:
        v_o[...] = v
    kk = k * kk_ref[...]
    kk = kk / jnp.maximum(jnp.sqrt(_group_sum(kk * kk)), 1e-12)
    k = k * (1.0 + (a - 1.0) * ka_ref[...])
    bonus_o[...] = _group_sum(r * k * rk_ref[...]) * v
    C = WKV_CHUNK
    rr = lax.broadcasted_iota(jnp.int32, (tm, tm), 0)
    cc = lax.broadcasted_iota(jnp.int32, (tm, tm), 1)
    tri = (((rr // C) == (cc // C)) & (cc <= rr)).astype(BF16)
    hi = lw.astype(BF16)
    rem = lw - hi.astype(F32)
    mid = rem.astype(BF16)
    lo = (rem - mid.astype(F32)).astype(BF16)
    cl = _dot(tri, hi) + _dot(tri, mid) + _dot(tri, lo)
    ends = [cl[c * C + C - 1:c * C + C] for c in range(tm // C)]
    cl_end = jnp.concatenate([jnp.broadcast_to(e, (C, e.shape[1])) for e in ends], axis=0)
    e_neg = jnp.exp(-cl)
    e_rem = jnp.exp(cl_end - cl)
    at_o[...] = (-kk * jnp.exp(cl - lw)).astype(BF16)
    rt_o[...] = (r * jnp.exp(cl)).astype(BF16)
    bt_o[...] = (kk * a * e_neg).astype(BF16)
    kt_o[...] = (k * e_neg).astype(BF16)
    bh_o[...] = (kk * a * e_rem).astype(BF16)
    kh_o[...] = (k * e_rem).astype(BF16)
    vb_o[...] = v.astype(BF16)
    for c, e in enumerate(ends):
        wc_o[c] = jnp.broadcast_to(jnp.exp(e), wc_o.shape[1:])


def _rwkv_pre(x, g, p, v_first):
    s, d = x.shape
    tm = _row_tile(s, WKV_TILE)
    row = pl.BlockSpec((tm, d), lambda i: (i, 0))
    prev = pl.BlockSpec((HALO, d), lambda i: (jnp.maximum(i * (tm // HALO) - 1, 0), 0))
    names = ["mix", "w_rkv", "w0", "w1", "w2", "a0", "a1", "a2", "g1", "g2", "k_k", "k_a", "r_k"]
    args = [x, x, g] + [p[n] for n in names]
    specs = [row, prev, _full(g.shape)] + [_full(p[n].shape) for n in names]
    has_vres = v_first is not None
    if has_vres:
        args += [v_first, p["v0"], p["v1"], p["v2"]]
        specs += [row] + [_full(p[n].shape) for n in ("v0", "v1", "v2")]
    n_ch = tm // WKV_CHUNK
    wc_spec = pl.BlockSpec((n_ch, SUBLANES, d), lambda i: (i, 0, 0))
    out_specs = [row] * 7 + [wc_spec, row, row]
    out_shape = ([jax.ShapeDtypeStruct((s, d), BF16)] * 7
                 + [jax.ShapeDtypeStruct((s // WKV_CHUNK, SUBLANES, d), F32)]
                 + [jax.ShapeDtypeStruct((s, d), F32)] * 2)
    if not has_vres:
        out_specs.append(row)
        out_shape.append(jax.ShapeDtypeStruct((s, d), F32))
    return pl.pallas_call(
        functools.partial(_rwkv_pre_kernel, has_vres=has_vres),
        grid=(s // tm,),
        in_specs=specs,
        out_specs=out_specs,
        out_shape=out_shape,
        compiler_params=_params("parallel"),
        name="rwkv_pre",
    )(*args)


def _wkv_kernel(at_ref, rt_ref, bt_ref, kt_ref, bh_ref, kh_ref, v_ref, wc_ref, bonus_ref,
                lnw_ref, lnb_ref, o_ref, s_ref, *, group, prec):
    C = WKV_CHUNK
    C2 = 2 * C
    n_pairs = o_ref.shape[1] // LANES
    n_groups = o_ref.shape[0] // (C * group)

    @pl.when(pl.program_id(0) == 0)
    def _():
        s_ref[...] = jnp.zeros(s_ref.shape, F32)

    lane = lax.broadcasted_iota(jnp.int32, (1, LANES), 1)
    head0 = lane < RWKV_HEAD
    m0 = head0.astype(F32)
    m1 = 1.0 - m0
    t2 = lax.broadcasted_iota(jnp.int32, (C2, C2), 0)
    s2 = lax.broadcasted_iota(jnp.int32, (C2, C2), 1)
    same_head = (t2 // C) == (s2 // C)
    strict = (same_head & (s2 < t2)).astype(F32)
    incl = (same_head & (s2 <= t2)).astype(F32)
    eye = (t2 == s2).astype(F32)

    def level_mask(b):
        return (((t2 // (2 * b)) == (s2 // (2 * b))) & ((t2 % (2 * b)) >= b)
                & ((s2 % (2 * b)) < b)).astype(F32)

    strict_m, incl_m = strict, incl
    levels = []
    b_sz = 1
    while b_sz < C:
        levels.append(level_mask(b_sz))
        b_sz *= 2

    def stack(x):
        zero = jnp.zeros_like(x)
        return jnp.concatenate([jnp.where(head0, x, zero), jnp.where(head0, zero, x)], axis=0)

    def twice(x):
        return jnp.concatenate([x, x], axis=0)

    def cast(x):
        return x.astype(prec)

    def group_body(g, carry):
        probs = [(ci, p) for ci in range(group) for p in range(n_pairs)]
        offs = [pl.multiple_of((g * group + ci) * C, C) for ci in range(group)]

        def load(ref, ci, p):
            return cast(ref[pl.ds(offs[ci], C), p * LANES:(p + 1) * LANES])

        a_s = [stack(load(at_ref, ci, p)) for ci, p in probs]
        r_s = [stack(load(rt_ref, ci, p)) for ci, p in probs]
        v_s = [stack(load(v_ref, ci, p)) for ci, p in probs]
        hat = [jnp.concatenate([stack(load(bh_ref, ci, p)), stack(load(kh_ref, ci, p))], axis=0)
               for ci, p in probs]
        til = [jnp.concatenate([twice(load(bt_ref, ci, p)), twice(load(kt_ref, ci, p))], axis=0)
               for ci, p in probs]
        lm = [_dot_nt(jnp.concatenate([a, r], axis=0), t) for a, r, t in zip(a_s, r_s, til)]
        l_ab = [x[:C2, :C2] * strict_m for x in lm]
        l_ak = [cast(x[:C2, C2:] * strict_m) for x in lm]
        m_cat = [cast(jnp.concatenate([x[C2:, :C2] * incl_m, x[C2:, C2:] * incl_m], axis=1))
                 for x in lm]
        t_inv = [eye + l * levels[0] for l in l_ab]
        for lvl in levels[1:]:
            t_b = [cast(t) for t in t_inv]
            z = [_dot(t, cast(l * lvl)) for t, l in zip(t_b, l_ab)]
            t_inv = [t + _dot(cast(zz), tb) for t, zz, tb in zip(t_inv, z, t_b)]
        t_b = [cast(t) for t in t_inv]
        p_loc = [_dot(l, v) for l, v in zip(l_ak, v_s)]
        for ci in range(group):
            rows = pl.ds(offs[ci], C)
            c_idx = g * group + ci
            sel = [ci * n_pairs + p for p in range(n_pairs)]
            st = [s_ref[p] for p in range(n_pairs)]
            st_b = [cast(s) for s in st]
            x = [_dot_nt(a_s[i], sb) + p_loc[i] for i, sb in zip(sel, st_b)]
            u = [cast(_dot(t_b[i], cast(xx))) for i, xx in zip(sel, x)]
            uv = [jnp.concatenate([uu, v_s[i]], axis=0) for i, uu in zip(sel, u)]
            y = [_dot_nt(r_s[i], sb) + _dot(m_cat[i], w) for i, sb, w in zip(sel, st_b, uv)]
            for p in range(n_pairs):
                cols = slice(p * LANES, (p + 1) * LANES)
                w_end = wc_ref[c_idx, 0:1, cols]
                s_ref[p] = st[p] * w_end + _dot_tn(uv[p], hat[sel[p]])
                yp = y[p][:C] + y[p][C:]
                mu = (jnp.sum(yp * m0, axis=-1, keepdims=True) * m0
                      + jnp.sum(yp * m1, axis=-1, keepdims=True) * m1) * (1.0 / RWKV_HEAD)
                yc = yp - mu
                yc2 = yc * yc
                var = (jnp.sum(yc2 * m0, axis=-1, keepdims=True) * m0
                       + jnp.sum(yc2 * m1, axis=-1, keepdims=True) * m1) * (1.0 / RWKV_HEAD)
                yn = yc * lax.rsqrt(var + GN_EPS)
                o_ref[rows, cols] = yn * lnw_ref[:, cols] + lnb_ref[:, cols] + bonus_ref[rows, cols]
        return carry

    lax.fori_loop(0, n_groups, group_body, 0)


def _wkv(at, rt, bt, kt, bh, kh, vb, wc, bonus, ln_w, ln_b, prec=BF16):
    s, d = at.shape
    tt = _row_tile(s, WKV_STEP)
    group = 2
    row = pl.BlockSpec((tt, d), lambda t: (t, 0))
    return pl.pallas_call(
        functools.partial(_wkv_kernel, group=group, prec=prec),
        grid=(s // tt,),
        in_specs=[row] * 7 + [pl.BlockSpec((tt // WKV_CHUNK, SUBLANES, d), lambda t: (t, 0, 0)),
                              row, _full(ln_w.shape), _full(ln_b.shape)],
        out_specs=row,
        out_shape=jax.ShapeDtypeStruct((s, d), F32),
        scratch_shapes=[pltpu.VMEM((d // LANES, LANES, LANES), F32)],
        compiler_params=_params("arbitrary"),
        name="wkv7_scan",
    )(at, rt, bt, kt, bh, kh, vb, wc, bonus, ln_w, ln_b)


def _rot_cols(w):
    half = QK_ROPE_DIM // 2
    return jnp.concatenate([-w[..., half:], w[..., :half]], axis=-1)


def _mla_weights(w_in, w_uq, w_ukv):
    d = w_in.shape[0]
    lat = Q_LORA_RANK + KV_LORA_RANK
    w_kr = w_in[:, lat:]
    w_in_e = jnp.concatenate([w_in[:, :lat], w_kr, _rot_cols(w_kr)], axis=1).astype(BF16)
    uq = w_uq.reshape(Q_LORA_RANK, MLA_HEADS, QK_DIM)
    uq_rope = uq[:, :, QK_NOPE_DIM:]
    w_uq_e = jnp.concatenate(
        [uq[:, :, :QK_NOPE_DIM].reshape(Q_LORA_RANK, -1),
         uq_rope.reshape(Q_LORA_RANK, -1),
         _rot_cols(uq_rope).reshape(Q_LORA_RANK, -1)], axis=1).astype(BF16)
    ukv = w_ukv.reshape(KV_LORA_RANK, MLA_HEADS, QK_NOPE_DIM + V_HEAD_DIM)
    w_ukv_p = jnp.concatenate(
        [ukv[:, :, :QK_NOPE_DIM].reshape(KV_LORA_RANK, -1),
         ukv[:, :, QK_NOPE_DIM:].reshape(KV_LORA_RANK, -1)], axis=1).astype(BF16)
    del d
    return w_in_e, w_uq_e, w_ukv_p


def _mla_layer(x, g, cos_t, sin_t, w_in, q_norm, w_uq, kv_norm, w_ukv):
    w_in_e, w_uq_e, w_ukv_p = _mla_weights(w_in, w_uq, w_ukv)
    q_cat, k_cat, v = _mla_pre(x, g[None], w_in_e, q_norm[None], w_uq_e, kv_norm[None],
                               w_ukv_p, cos_t, sin_t)
    return _attention(q_cat, k_cat, v)


def _rwkv_layer(x, g, v_first, p, wkv_prec=BF16):
    row = lambda t: t.reshape(1, -1)
    pp = {
        "mix": p["mix"], "w_rkv": p["w_rkv"].astype(BF16),
        "w0": row(p["w0"]), "w1": p["w1"].astype(BF16), "w2": p["w2"].astype(BF16),
        "a0": row(p["a0"]), "a1": p["a1"].astype(BF16), "a2": p["a2"].astype(BF16),
        "g1": p["g1"].astype(BF16), "g2": p["g2"].astype(BF16),
        "k_k": row(p["k_k"]), "k_a": row(p["k_a"]), "r_k": row(p["r_k"]),
    }
    if v_first is not None:
        pp.update(v0=row(p["v0"]), v1=p["v1"].astype(BF16), v2=p["v2"].astype(BF16))
    outs = _rwkv_pre(x, g[None], pp, v_first)
    scan_in, gate = outs[:9], outs[9]
    z = _wkv(*scan_in, row(p["ln_w"]), row(p["ln_b"]), prec=wkv_prec)
    return z, gate, (outs[10] if v_first is None else v_first)


def _mix_ffn_layer(x, z, gate, w_o, g, w_in, conv_w, conv_b, w_out, final_gain=None):
    return _mix_ffn(x, z, gate, w_o.astype(BF16), g[None], w_in.astype(BF16), conv_w, conv_b[None],
                    w_out.astype(BF16), None if final_gain is None else final_gain[None])


def kernel(x, positions, norm_mix, norm_ffn, norm_final, mla_w_in, mla_q_norm, mla_w_uq, mla_kv_norm, mla_w_ukv, mla_w_o, rwkv_mix, rwkv_w_rkv, rwkv_w0, rwkv_w1, rwkv_w2, rwkv_a0, rwkv_a1, rwkv_a2, rwkv_v0, rwkv_v1, rwkv_v2, rwkv_g1, rwkv_g2, rwkv_k_k, rwkv_k_a, rwkv_r_k, rwkv_ln_w, rwkv_ln_b, rwkv_w_o, ffn_w_in, ffn_conv_w, ffn_conv_b, ffn_w_out):
    batch, seq, d = x.shape
    assert batch == 1
    depth = norm_mix.shape[0]
    half = QK_ROPE_DIM // 2
    inv_freq = ROPE_THETA ** (-jnp.arange(half, dtype=F32) / half)
    freq_row = jnp.tile(inv_freq, LANES // half)[None]
    cos_t, sin_t = _rope_tables(positions.astype(F32).reshape(seq, 1), freq_row)
    xs = x.reshape(seq, d)
    v_first = None
    for i in range(depth):
        j = i // 2
        if i % 2 == 0:
            z = _mla_layer(xs, norm_mix[i], cos_t, sin_t, mla_w_in[j], mla_q_norm[j], mla_w_uq[j],
                           mla_kv_norm[j], mla_w_ukv[j])
            gate, w_o = None, mla_w_o[j]
        else:
            p = {"mix": rwkv_mix[j], "w_rkv": rwkv_w_rkv[j], "w0": rwkv_w0[j], "w1": rwkv_w1[j],
                 "w2": rwkv_w2[j], "a0": rwkv_a0[j], "a1": rwkv_a1[j], "a2": rwkv_a2[j],
                 "g1": rwkv_g1[j], "g2": rwkv_g2[j], "k_k": rwkv_k_k[j], "k_a": rwkv_k_a[j],
                 "r_k": rwkv_r_k[j], "ln_w": rwkv_ln_w[j], "ln_b": rwkv_ln_b[j]}
            if j > 0:
                p.update(v0=rwkv_v0[j - 1], v1=rwkv_v1[j - 1], v2=rwkv_v2[j - 1])
            z, gate, v_first = _rwkv_layer(xs, norm_mix[i], v_first, p)
            w_o = rwkv_w_o[j]
        xs = _mix_ffn_layer(xs, z, gate, w_o, norm_ffn[i], ffn_w_in[i], ffn_conv_w[i], ffn_conv_b[i],
                            ffn_w_out[i], norm_final if i == depth - 1 else None)
    return xs.reshape(batch, seq, d)
```

```python
import functools
import math

import jax
import jax.numpy as jnp
from jax import lax
from jax.experimental import pallas as pl
from jax.experimental.pallas import tpu as pltpu

NORM_EPS = 1e-6
GN_EPS = 64e-5
MLA_HEADS = 8
QK_NOPE_DIM = 128
QK_ROPE_DIM = 64
QK_DIM = QK_NOPE_DIM + QK_ROPE_DIM
QK_PAD = 256
V_HEAD_DIM = 128
V_ROWS = V_HEAD_DIM + 16
Q_LORA_RANK = 384
KV_LORA_RANK = 256
ROPE_THETA = 10000.0
RWKV_HEAD = 64
LANES = 128
D_FF = 2816
HALO = 16
WKV_CHUNK = 64
WKV_TILE = 256
WKV_STEP = 512
SUBLANES = 8
ATTN_TK = 512
ATTN_CHAINS = 8

BF16 = jnp.bfloat16
F32 = jnp.float32
VMEM_LIMIT = 56 * 1024 * 1024


def _params(*sem):
    return pltpu.CompilerParams(dimension_semantics=sem, vmem_limit_bytes=VMEM_LIMIT)


def _dot(a, b, precision=None):
    return jnp.dot(a, b, preferred_element_type=F32, precision=precision)


def _dot_nt(a, b, precision=None):
    return lax.dot_general(a, b, (((1,), (1,)), ((), ())),
                           preferred_element_type=F32, precision=precision)


def _dot_tn(a, b, precision=None):
    return lax.dot_general(a, b, (((0,), (0,)), ((), ())),
                           preferred_element_type=F32, precision=precision)


def _rms(x, g):
    return x * lax.rsqrt(jnp.mean(x * x, axis=-1, keepdims=True) + NORM_EPS) * g


def _row_tile(s, want):
    return min(want, s)


def _full(shape):
    return pl.BlockSpec(shape, lambda *_: (0,) * len(shape))


def _rope_table_kernel(pos_ref, freq_ref, cos_ref, sin_ref):
    ang = pos_ref[...] * freq_ref[...]
    cos_ref[...] = jnp.cos(ang)
    sin_ref[...] = jnp.sin(ang)


def _rope_tables(pos_col, freq_row):
    s = pos_col.shape[0]
    tm = _row_tile(s, 1024)
    return pl.pallas_call(
        _rope_table_kernel,
        grid=(s // tm,),
        in_specs=[pl.BlockSpec((tm, 1), lambda i: (i, 0)), _full((1, LANES))],
        out_specs=[pl.BlockSpec((tm, LANES), lambda i: (i, 0))] * 2,
        out_shape=[jax.ShapeDtypeStruct((s, LANES), F32)] * 2,
        compiler_params=_params("parallel"),
        name="rope_tables",
    )(pos_col, freq_row)


def _mla_pre_kernel(x_ref, g_ref, w_in_ref, qn_ref, w_uq_ref, kvn_ref, w_ukv_ref,
                    cos_ref, sin_ref, q_ref, k_ref, v_ref):
    h = _rms(x_ref[...], g_ref[...]).astype(BF16)
    lat = _dot(h, w_in_ref[...])
    q_lat = lat[:, :Q_LORA_RANK]
    c_kv = lat[:, Q_LORA_RANK:Q_LORA_RANK + KV_LORA_RANK]
    kr = lat[:, Q_LORA_RANK + KV_LORA_RANK:]
    cos = cos_ref[...]
    sin = sin_ref[...]
    k_rope = (kr[:, :QK_ROPE_DIM] * cos[:, :QK_ROPE_DIM]
              + kr[:, QK_ROPE_DIM:] * sin[:, :QK_ROPE_DIM])
    qn = _rms(q_lat, qn_ref[...]).astype(BF16)
    q_all = _dot(qn, w_uq_ref[...])
    n_nope = MLA_HEADS * QK_NOPE_DIM
    n_rope = MLA_HEADS * QK_ROPE_DIM
    reps = n_rope // LANES
    qscale = QK_DIM ** -0.5 * math.log2(math.e)
    q_rope = (q_all[:, n_nope:n_nope + n_rope] * jnp.tile(cos, (1, reps))
              + q_all[:, n_nope + n_rope:] * jnp.tile(sin, (1, reps))) * qscale
    kvn = _rms(c_kv, kvn_ref[...]).astype(BF16)
    kv = _dot(kvn, w_ukv_ref[...])
    tm = x_ref.shape[0]
    qn_t = (q_all[:, :n_nope] * qscale).T.astype(BF16)
    qr_t = q_rope.T.astype(BF16)
    v_t = kv[:, n_nope:].T.astype(BF16)
    zpad_t = jnp.zeros((QK_PAD - QK_DIM, tm), BF16)
    ones_t = jnp.ones((V_ROWS - V_HEAD_DIM, tm), BF16)
    zpad = jnp.zeros((tm, QK_PAD - QK_DIM), BF16)
    k_rope_b = k_rope.astype(BF16)
    for hd in range(MLA_HEADS):
        q_ref[hd, 0:QK_NOPE_DIM, :] = qn_t[hd * QK_NOPE_DIM:(hd + 1) * QK_NOPE_DIM]
        q_ref[hd, QK_NOPE_DIM:QK_DIM, :] = qr_t[hd * QK_ROPE_DIM:(hd + 1) * QK_ROPE_DIM]
        q_ref[hd, QK_DIM:QK_PAD, :] = zpad_t
        k_ref[hd, :, 0:QK_NOPE_DIM] = kv[:, hd * QK_NOPE_DIM:(hd + 1) * QK_NOPE_DIM].astype(BF16)
        k_ref[hd, :, QK_NOPE_DIM:QK_DIM] = k_rope_b
        k_ref[hd, :, QK_DIM:QK_PAD] = zpad
        v_ref[hd, 0, 0:V_HEAD_DIM, :] = v_t[hd * V_HEAD_DIM:(hd + 1) * V_HEAD_DIM]
        v_ref[hd, 0, V_HEAD_DIM:V_ROWS, :] = ones_t


def _mla_pre(x, g, w_in_e, q_norm, w_uq_e, kv_norm, w_ukv_p, cos_t, sin_t):
    s, d = x.shape
    tm = _row_tile(s, ATTN_TK)
    row = lambda w: pl.BlockSpec((tm, w), lambda i: (i, 0))
    return pl.pallas_call(
        _mla_pre_kernel,
        grid=(s // tm,),
        in_specs=[row(d), _full(g.shape), _full(w_in_e.shape), _full(q_norm.shape),
                  _full(w_uq_e.shape), _full(kv_norm.shape), _full(w_ukv_p.shape),
                  row(LANES), row(LANES)],
        out_specs=[pl.BlockSpec((MLA_HEADS, QK_PAD, tm), lambda i: (0, 0, i)),
                   pl.BlockSpec((MLA_HEADS, tm, QK_PAD), lambda i: (0, i, 0)),
                   pl.BlockSpec((MLA_HEADS, 1, V_ROWS, tm), lambda i: (0, i, 0, 0))],
        out_shape=[jax.ShapeDtypeStruct((MLA_HEADS, QK_PAD, s), BF16),
                   jax.ShapeDtypeStruct((MLA_HEADS, s, QK_PAD), BF16),
                   jax.ShapeDtypeStruct((MLA_HEADS, s // tm, V_ROWS, tm), BF16)],
        compiler_params=_params("parallel"),
        name="mla_pre",
    )(x, g, w_in_e, q_norm, w_uq_e, kv_norm, w_ukv_p, cos_t, sin_t)


def _attn_kernel(q_ref, k_ref, v_ref, o_ref, *scratch, tk, n_chains):
    qi = pl.program_id(1)
    accs = scratch[:n_chains]
    sbuf = [scratch[n_chains + 2 * c:n_chains + 2 * c + 2] for c in range(n_chains)]
    for acc in accs:
        acc[...] = jnp.zeros(acc.shape, F32)

    def scores(task):
        c, j, par, _ = task
        q_t = q_ref[:, c * tk:(c + 1) * tk]
        off = pl.multiple_of(j * tk, tk)
        s = _dot(k_ref[pl.ds(off, tk), :], q_t)
        sbuf[c][par][...] = s
        return jnp.max(s, axis=0, keepdims=True)

    def absorb(task, m_blk, m_old):
        c, j, par, masked = task
        s = sbuf[c][par][...]
        if masked:
            key = lax.broadcasted_iota(jnp.int32, s.shape, 0)
            qry = lax.broadcasted_iota(jnp.int32, s.shape, 1)
            s = jnp.where(key <= qry, s, jnp.finfo(F32).min)
            m_blk = jnp.max(s, axis=0, keepdims=True)
        m_new = jnp.maximum(m_old, m_blk)
        p = jnp.exp2(s - m_new)
        alpha = jnp.exp2(m_old - m_new)
        accs[c][...] = alpha * accs[c][...] + _dot(v_ref[j], p.astype(BF16))
        return m_new

    def run(tasks, lookahead, ms, pending):
        ms = list(ms)
        for i, task in enumerate(tasks):
            nxt = tasks[i + 1] if i + 1 < len(tasks) else lookahead
            staged = scores(nxt) if nxt is not None else None
            ms[task[0]] = absorb(task, pending, ms[task[0]])
            pending = staged
        return ms, pending

    chains = range(n_chains)
    n_full = n_chains * qi

    def body(jj, carry):
        j = 2 * jj
        tasks = [(c, j, 0, False) for c in chains] + [(c, j + 1, 1, False) for c in chains]
        ms, pending = run(tasks, (0, j + 2, 0, False), carry[:-1], carry[-1])
        return (*ms, pending)

    m0 = jnp.full((1, tk), -1e30, F32)
    carry = lax.fori_loop(0, n_full // 2, body, (*[m0] * n_chains, scores((0, 0, 0, False))))
    tail = [(c, n_full + kb, kb % 2, c == kb) for kb in chains for c in range(kb, n_chains)]
    run(tail, None, carry[:-1], carry[-1])
    for c, acc in enumerate(accs):
        out = acc[0:V_HEAD_DIM, :] / acc[V_HEAD_DIM:V_HEAD_DIM + 1, :]
        o_ref[c * tk:(c + 1) * tk, :] = out.T.astype(o_ref.dtype)


def _attention(q_t, k_cat, v_t):
    nh, s, _ = k_cat.shape
    tk = v_t.shape[-1]
    n_chains = min(ATTN_CHAINS, s // tk)
    assert n_chains % 2 == 0
    tq = n_chains * tk
    return pl.pallas_call(
        functools.partial(_attn_kernel, tk=tk, n_chains=n_chains),
        grid=(nh, s // tq),
        in_specs=[pl.BlockSpec((None, QK_PAD, tq), lambda h, i: (h, 0, i)),
                  pl.BlockSpec((None, s, QK_PAD), lambda h, i: (h, 0, 0)),
                  pl.BlockSpec((None, s // tk, V_ROWS, tk), lambda h, i: (h, 0, 0, 0))],
        out_specs=pl.BlockSpec((tq, V_HEAD_DIM), lambda h, i: (i, h)),
        out_shape=jax.ShapeDtypeStruct((s, nh * V_HEAD_DIM), BF16),
        scratch_shapes=([pltpu.VMEM((V_ROWS, tk), F32)] * n_chains
                        + [pltpu.VMEM((tk, tk), F32)] * (2 * n_chains)),
        compiler_params=_params("parallel", "arbitrary"),
        name="mla_attention",
    )(q_t, k_cat, v_t)


FFN_CHUNKS = ((0, 1024), (1024, 2048), (2048, D_FF))


def _mix_ffn_kernel(*refs, gated, final):
    refs = list(refs)
    o_ref = refs.pop()
    fg_ref = refs.pop() if final else None
    x_ref, xp_ref, z_ref, zp_ref = refs[:4]
    gt_ref, gtp_ref = refs[4:6] if gated else (None, None)
    wo_ref, g_ref, wi_ref, cw_ref, cb_ref, wout_ref = refs[-6:]
    i = pl.program_id(0)

    def mixed(xr, zr, gr):
        z = zr[...]
        if gated:
            z = z * gr[...]
        return xr[...] + _dot(z.astype(BF16), wo_ref[...])

    x1 = mixed(x_ref, z_ref, gt_ref)
    x1p = mixed(xp_ref, zp_ref, gtp_ref)
    keep = (i > 0).astype(F32)
    hn = jnp.concatenate([(_rms(x1p, g_ref[...]) * keep).astype(BF16),
                          _rms(x1, g_ref[...]).astype(BF16)], axis=0)
    acc = x1
    for lo, hi in FFN_CHUNKS:
        gate = _dot(hn, wi_ref[:, lo:hi])
        up = _dot(hn[HALO:], wi_ref[:, D_FF + lo:D_FF + hi])
        conv = (gate[HALO:] * cw_ref[2:3, lo:hi]
                + pltpu.roll(gate, 1, 0)[HALO:] * cw_ref[1:2, lo:hi]
                + pltpu.roll(gate, 2, 0)[HALO:] * cw_ref[0:1, lo:hi] + cb_ref[:, lo:hi])
        act = (conv * jax.nn.sigmoid(conv) * up).astype(BF16)
        acc = acc + _dot(act, wout_ref[lo:hi, :])
    o_ref[...] = _rms(acc, fg_ref[...]) if final else acc


def _mix_ffn(x, z, gate, w_o, g, w_in, conv_w, conv_b, w_out, final_gain=None):
    s, d = x.shape
    tm = _row_tile(s, 512)
    row = pl.BlockSpec((tm, d), lambda i: (i, 0))
    prev = pl.BlockSpec((HALO, d), lambda i: (jnp.maximum(i * (tm // HALO) - 1, 0), 0))
    const = lambda a: pl.BlockSpec(a.shape, lambda i: (0,) * a.ndim, pipeline_mode=pl.Buffered(1))
    gated = gate is not None
    final = final_gain is not None
    consts = [w_o, g, w_in, conv_w, conv_b, w_out] + ([final_gain] if final else [])
    args = [x, x, z, z] + ([gate, gate] if gated else []) + consts
    specs = [row, prev, row, prev] + ([row, prev] if gated else []) + [const(a) for a in consts]
    return pl.pallas_call(
        functools.partial(_mix_ffn_kernel, gated=gated, final=final),
        grid=(s // tm,),
        in_specs=specs,
        out_specs=row,
        out_shape=jax.ShapeDtypeStruct((s, d), F32),
        compiler_params=_params("parallel"),
        name="mix_ffn",
    )(*args)


def _group_sum(x):
    w = 2 * LANES
    r = lax.broadcasted_iota(jnp.int32, (w, w), 0) // RWKV_HEAD
    c = lax.broadcasted_iota(jnp.int32, (w, w), 1) // RWKV_HEAD
    ones_bd = (r == c).astype(BF16)
    hi = x.astype(BF16)
    lo = (x - hi.astype(F32)).astype(BF16)
    cols = [_dot(hi[:, j:j + w], ones_bd) + _dot(lo[:, j:j + w], ones_bd)
            for j in range(0, x.shape[1], w)]
    return cols[0] if len(cols) == 1 else jnp.concatenate(cols, axis=1)


def _rwkv_pre_kernel(*refs, has_vres):
    (x_ref, xp_ref, g_ref, mix_ref, wrkv_ref, w0_ref, w1_ref, w2_ref, a0_ref, a1_ref, a2_ref,
     g1_ref, g2_ref, kk_ref, ka_ref, rk_ref) = refs[:16]
    if has_vres:
        vf_ref, v0_ref, v1_ref, v2_ref = refs[16:20]
        at_o, rt_o, bt_o, kt_o, bh_o, kh_o, vb_o, wc_o, bonus_o, g_o = refs[20:]
    else:
        at_o, rt_o, bt_o, kt_o, bh_o, kh_o, vb_o, wc_o, bonus_o, g_o, v_o = refs[16:]
    i = pl.program_id(0)
    tm = x_ref.shape[0]
    h = _rms(x_ref[...], g_ref[...])
    hp = _rms(xp_ref[...], g_ref[...])[HALO - 1:HALO] * (i > 0).astype(F32)
    row = lax.broadcasted_iota(jnp.int32, h.shape, 0)
    shifted = jnp.where(row == 0, jnp.broadcast_to(hp, h.shape), pltpu.roll(h, 1, 0))
    xx = shifted - h
    mix = mix_ref[...]
    xm = lambda n: (h + xx * mix[n:n + 1]).astype(BF16)
    xv = xm(2)
    r = _dot(xm(0), wrkv_ref[0])
    k = _dot(xm(1), wrkv_ref[1])
    v = _dot(xv, wrkv_ref[2])
    lora_w = _dot(jnp.tanh(_dot(xm(3), w1_ref[...])).astype(BF16), w2_ref[...])
    lw = -math.exp(-0.5) * jax.nn.sigmoid(w0_ref[...] + lora_w)
    a = jax.nn.sigmoid(a0_ref[...] + _dot(_dot(xm(4), a1_ref[...]).astype(BF16), a2_ref[...]))
    g_o[...] = _dot(jax.nn.sigmoid(_dot(xm(5), g1_ref[...])).astype(BF16), g2_ref[...])
    if has_vres:
        gate_v = jax.nn.sigmoid(v0_ref[...] + _dot(_dot(xv, v1_ref[...]).astype(BF16), v2_ref[...]))
        v = v + (vf_ref[...] - v) * gate_v
    else:
        v_o[...] = v
    kk = k * kk_ref[...]
    kk = kk * lax.rsqrt(jnp.maximum(_group_sum(kk * kk), 1e-24))
    k = k * (1.0 + (a - 1.0) * ka_ref[...])
    bonus_o[...] = _group_sum(r * k * rk_ref[...]) * v
    C = WKV_CHUNK
    rr = lax.broadcasted_iota(jnp.int32, (tm, tm), 0)
    cc = lax.broadcasted_iota(jnp.int32, (tm, tm), 1)
    tri = (((rr // C) == (cc // C)) & (cc <= rr)).astype(BF16)
    hi = lw.astype(BF16)
    rem = lw - hi.astype(F32)
    mid = rem.astype(BF16)
    lo = (rem - mid.astype(F32)).astype(BF16)
    cl = _dot(tri, hi) + _dot(tri, mid) + _dot(tri, lo)
    ends = [cl[c * C + C - 1:c * C + C] for c in range(tm // C)]
    cl_end = jnp.concatenate([jnp.broadcast_to(e, (C, e.shape[1])) for e in ends], axis=0)
    e_neg = jnp.exp(-cl)
    e_rem = jnp.exp(cl_end - cl)
    at_o[...] = (-kk * jnp.exp(cl - lw)).astype(BF16)
    rt_o[...] = (r * jnp.exp(cl)).astype(BF16)
    bt_o[...] = (kk * a * e_neg).astype(BF16)
    kt_o[...] = (k * e_neg).astype(BF16)
    bh_o[...] = (kk * a * e_rem).astype(BF16)
    kh_o[...] = (k * e_rem).astype(BF16)
    vb_o[...] = v.astype(BF16)
    for c, e in enumerate(ends):
        wc_o[c] = jnp.broadcast_to(jnp.exp(e), wc_o.shape[1:])


def _rwkv_pre(x, g, p, v_first):
    s, d = x.shape
    tm = _row_tile(s, WKV_TILE)
    row = pl.BlockSpec((tm, d), lambda i: (i, 0))
    prev = pl.BlockSpec((HALO, d), lambda i: (jnp.maximum(i * (tm // HALO) - 1, 0), 0))
    names = ["mix", "w_rkv", "w0", "w1", "w2", "a0", "a1", "a2", "g1", "g2", "k_k", "k_a", "r_k"]
    args = [x, x, g] + [p[n] for n in names]
    specs = [row, prev, _full(g.shape)] + [_full(p[n].shape) for n in names]
    has_vres = v_first is not None
    if has_vres:
        args += [v_first, p["v0"], p["v1"], p["v2"]]
        specs += [row] + [_full(p[n].shape) for n in ("v0", "v1", "v2")]
    n_ch = tm // WKV_CHUNK
    wc_spec = pl.BlockSpec((n_ch, SUBLANES, d), lambda i: (i, 0, 0))
    out_specs = [row] * 7 + [wc_spec, row, row]
    out_shape = ([jax.ShapeDtypeStruct((s, d), BF16)] * 7
                 + [jax.ShapeDtypeStruct((s // WKV_CHUNK, SUBLANES, d), F32)]
                 + [jax.ShapeDtypeStruct((s, d), F32)] * 2)
    if not has_vres:
        out_specs.append(row)
        out_shape.append(jax.ShapeDtypeStruct((s, d), F32))
    return pl.pallas_call(
        functools.partial(_rwkv_pre_kernel, has_vres=has_vres),
        grid=(s // tm,),
        in_specs=specs,
        out_specs=out_specs,
        out_shape=out_shape,
        compiler_params=_params("parallel"),
        name="rwkv_pre",
    )(*args)


def _wkv_kernel(at_ref, rt_ref, bt_ref, kt_ref, bh_ref, kh_ref, v_ref, wc_ref, bonus_ref,
                lnw_ref, lnb_ref, o_ref, s_ref, *, group, prec):
    C = WKV_CHUNK
    C2 = 2 * C
    n_pairs = o_ref.shape[1] // LANES
    n_groups = o_ref.shape[0] // (C * group)

    @pl.when(pl.program_id(0) == 0)
    def _():
        s_ref[...] = jnp.zeros(s_ref.shape, F32)

    lane = lax.broadcasted_iota(jnp.int32, (1, LANES), 1)
    head0 = lane < RWKV_HEAD
    m0 = head0.astype(F32)
    m1 = 1.0 - m0
    t2 = lax.broadcasted_iota(jnp.int32, (C2, C2), 0)
    s2 = lax.broadcasted_iota(jnp.int32, (C2, C2), 1)
    same_head = (t2 // C) == (s2 // C)
    strict = (same_head & (s2 < t2)).astype(F32)
    incl = (same_head & (s2 <= t2)).astype(F32)
    eye = (t2 == s2).astype(F32)

    def level_mask(b):
        return (((t2 // (2 * b)) == (s2 // (2 * b))) & ((t2 % (2 * b)) >= b)
                & ((s2 % (2 * b)) < b)).astype(F32)

    strict_m, incl_m = strict, incl
    levels = []
    b_sz = 1
    while b_sz < C:
        levels.append(level_mask(b_sz))
        b_sz *= 2

    def stack(x):
        zero = jnp.zeros_like(x)
        return jnp.concatenate([jnp.where(head0, x, zero), jnp.where(head0, zero, x)], axis=0)

    def twice(x):
        return jnp.concatenate([x, x], axis=0)

    def cast(x):
        return x.astype(prec)

    def group_body(g, carry):
        probs = [(ci, p) for ci in range(group) for p in range(n_pairs)]
        offs = [pl.multiple_of((g * group + ci) * C, C) for ci in range(group)]

        def load(ref, ci, p):
            return cast(ref[pl.ds(offs[ci], C), p * LANES:(p + 1) * LANES])

        a_s = [stack(load(at_ref, ci, p)) for ci, p in probs]
        r_s = [stack(load(rt_ref, ci, p)) for ci, p in probs]
        v_s = [stack(load(v_ref, ci, p)) for ci, p in probs]
        hat = [jnp.concatenate([stack(load(bh_ref, ci, p)), stack(load(kh_ref, ci, p))], axis=0)
               for ci, p in probs]
        til = [jnp.concatenate([twice(load(bt_ref, ci, p)), twice(load(kt_ref, ci, p))], axis=0)
               for ci, p in probs]
        lm = [_dot_nt(jnp.concatenate([a, r], axis=0), t) for a, r, t in zip(a_s, r_s, til)]
        l_ab = [x[:C2, :C2] * strict_m for x in lm]
        l_ak = [cast(x[:C2, C2:] * strict_m) for x in lm]
        m_cat = [cast(jnp.concatenate([x[C2:, :C2] * incl_m, x[C2:, C2:] * incl_m], axis=1))
                 for x in lm]
        t_inv = [eye + l * levels[0] for l in l_ab]
        for lvl in levels[1:]:
            t_b = [cast(t) for t in t_inv]
            z = [_dot(t, cast(l * lvl)) for t, l in zip(t_b, l_ab)]
            t_inv = [t + _dot(cast(zz), tb) for t, zz, tb in zip(t_inv, z, t_b)]
        t_b = [cast(t) for t in t_inv]
        p_loc = [_dot(l, v) for l, v in zip(l_ak, v_s)]
        for ci in range(group):
            rows = pl.ds(offs[ci], C)
            c_idx = g * group + ci
            sel = [ci * n_pairs + p for p in range(n_pairs)]
            st = [s_ref[p] for p in range(n_pairs)]
            st_b = [cast(s) for s in st]
            x = [_dot_nt(a_s[i], sb) + p_loc[i] for i, sb in zip(sel, st_b)]
            u = [cast(_dot(t_b[i], cast(xx))) for i, xx in zip(sel, x)]
            uv = [jnp.concatenate([uu, v_s[i]], axis=0) for i, uu in zip(sel, u)]
            y = [_dot_nt(r_s[i], sb) + _dot(m_cat[i], w) for i, sb, w in zip(sel, st_b, uv)]
            for p in range(n_pairs):
                cols = slice(p * LANES, (p + 1) * LANES)
                w_end = wc_ref[c_idx, 0:1, cols]
                s_ref[p] = st[p] * w_end + _dot_tn(uv[p], hat[sel[p]])
                yp = y[p][:C] + y[p][C:]
                mu = (jnp.sum(yp * m0, axis=-1, keepdims=True) * m0
                      + jnp.sum(yp * m1, axis=-1, keepdims=True) * m1) * (1.0 / RWKV_HEAD)
                yc = yp - mu
                yc2 = yc * yc
                var = (jnp.sum(yc2 * m0, axis=-1, keepdims=True) * m0
                       + jnp.sum(yc2 * m1, axis=-1, keepdims=True) * m1) * (1.0 / RWKV_HEAD)
                yn = yc * lax.rsqrt(var + GN_EPS)
                o_ref[rows, cols] = yn * lnw_ref[:, cols] + lnb_ref[:, cols] + bonus_ref[rows, cols]
        return carry

    lax.fori_loop(0, n_groups, group_body, 0)


def _wkv(at, rt, bt, kt, bh, kh, vb, wc, bonus, ln_w, ln_b, prec=BF16):
    s, d = at.shape
    tt = _row_tile(s, WKV_STEP)
    group = 2
    row = pl.BlockSpec((tt, d), lambda t: (t, 0))
    return pl.pallas_call(
        functools.partial(_wkv_kernel, group=group, prec=prec),
        grid=(s // tt,),
        in_specs=[row] * 7 + [pl.BlockSpec((tt // WKV_CHUNK, SUBLANES, d), lambda t: (t, 0, 0)),
                              row, _full(ln_w.shape), _full(ln_b.shape)],
        out_specs=row,
        out_shape=jax.ShapeDtypeStruct((s, d), F32),
        scratch_shapes=[pltpu.VMEM((d // LANES, LANES, LANES), F32)],
        compiler_params=_params("arbitrary"),
        name="wkv7_scan",
    )(at, rt, bt, kt, bh, kh, vb, wc, bonus, ln_w, ln_b)


def _rot_cols(w):
    half = QK_ROPE_DIM // 2
    return jnp.concatenate([-w[..., half:], w[..., :half]], axis=-1)


def _mla_weights(w_in, w_uq, w_ukv):
    d = w_in.shape[0]
    lat = Q_LORA_RANK + KV_LORA_RANK
    w_kr = w_in[:, lat:]
    w_in_e = jnp.concatenate([w_in[:, :lat], w_kr, _rot_cols(w_kr)], axis=1).astype(BF16)
    uq = w_uq.reshape(Q_LORA_RANK, MLA_HEADS, QK_DIM)
    uq_rope = uq[:, :, QK_NOPE_DIM:]
    w_uq_e = jnp.concatenate(
        [uq[:, :, :QK_NOPE_DIM].reshape(Q_LORA_RANK, -1),
         uq_rope.reshape(Q_LORA_RANK, -1),
         _rot_cols(uq_rope).reshape(Q_LORA_RANK, -1)], axis=1).astype(BF16)
    ukv = w_ukv.reshape(KV_LORA_RANK, MLA_HEADS, QK_NOPE_DIM + V_HEAD_DIM)
    w_ukv_p = jnp.concatenate(
        [ukv[:, :, :QK_NOPE_DIM].reshape(KV_LORA_RANK, -1),
         ukv[:, :, QK_NOPE_DIM:].reshape(KV_LORA_RANK, -1)], axis=1).astype(BF16)
    del d
    return w_in_e, w_uq_e, w_ukv_p


def _mla_layer(x, g, cos_t, sin_t, w_in, q_norm, w_uq, kv_norm, w_ukv):
    w_in_e, w_uq_e, w_ukv_p = _mla_weights(w_in, w_uq, w_ukv)
    q_cat, k_cat, v = _mla_pre(x, g[None], w_in_e, q_norm[None], w_uq_e, kv_norm[None],
                               w_ukv_p, cos_t, sin_t)
    return _attention(q_cat, k_cat, v)


def _rwkv_layer(x, g, v_first, p, wkv_prec=BF16):
    row = lambda t: t.reshape(1, -1)
    pp = {
        "mix": p["mix"], "w_rkv": p["w_rkv"].astype(BF16),
        "w0": row(p["w0"]), "w1": p["w1"].astype(BF16), "w2": p["w2"].astype(BF16),
        "a0": row(p["a0"]), "a1": p["a1"].astype(BF16), "a2": p["a2"].astype(BF16),
        "g1": p["g1"].astype(BF16), "g2": p["g2"].astype(BF16),
        "k_k": row(p["k_k"]), "k_a": row(p["k_a"]), "r_k": row(p["r_k"]),
    }
    if v_first is not None:
        pp.update(v0=row(p["v0"]), v1=p["v1"].astype(BF16), v2=p["v2"].astype(BF16))
    outs = _rwkv_pre(x, g[None], pp, v_first)
    scan_in, gate = outs[:9], outs[9]
    z = _wkv(*scan_in, row(p["ln_w"]), row(p["ln_b"]), prec=wkv_prec)
    return z, gate, (outs[10] if v_first is None else v_first)


def _mix_ffn_layer(x, z, gate, w_o, g, w_in, conv_w, conv_b, w_out, final_gain=None):
    return _mix_ffn(x, z, gate, w_o.astype(BF16), g[None], w_in.astype(BF16), conv_w, conv_b[None],
                    w_out.astype(BF16), None if final_gain is None else final_gain[None])


def kernel(x, positions, norm_mix, norm_ffn, norm_final, mla_w_in, mla_q_norm, mla_w_uq, mla_kv_norm, mla_w_ukv, mla_w_o, rwkv_mix, rwkv_w_rkv, rwkv_w0, rwkv_w1, rwkv_w2, rwkv_a0, rwkv_a1, rwkv_a2, rwkv_v0, rwkv_v1, rwkv_v2, rwkv_g1, rwkv_g2, rwkv_k_k, rwkv_k_a, rwkv_r_k, rwkv_ln_w, rwkv_ln_b, rwkv_w_o, ffn_w_in, ffn_conv_w, ffn_conv_b, ffn_w_out):
    batch, seq, d = x.shape
    assert batch == 1
    depth = norm_mix.shape[0]
    half = QK_ROPE_DIM // 2
    inv_freq = ROPE_THETA ** (-jnp.arange(half, dtype=F32) / half)
    freq_row = jnp.tile(inv_freq, LANES // half)[None]
    cos_t, sin_t = _rope_tables(positions.astype(F32).reshape(seq, 1), freq_row)
    xs = x.reshape(seq, d)
    v_first = None
    for i in range(depth):
        j = i // 2
        if i % 2 == 0:
            z = _mla_layer(xs, norm_mix[i], cos_t, sin_t, mla_w_in[j], mla_q_norm[j], mla_w_uq[j],
                           mla_kv_norm[j], mla_w_ukv[j])
            gate, w_o = None, mla_w_o[j]
        else:
            p = {"mix": rwkv_mix[j], "w_rkv": rwkv_w_rkv[j], "w0": rwkv_w0[j], "w1": rwkv_w1[j],
                 "w2": rwkv_w2[j], "a0": rwkv_a0[j], "a1": rwkv_a1[j], "a2": rwkv_a2[j],
                 "g1": rwkv_g1[j], "g2": rwkv_g2[j], "k_k": rwkv_k_k[j], "k_a": rwkv_k_a[j],
                 "r_k": rwkv_r_k[j], "ln_w": rwkv_ln_w[j], "ln_b": rwkv_ln_b[j]}
            if j > 0:
                p.update(v0=rwkv_v0[j - 1], v1=rwkv_v1[j - 1], v2=rwkv_v2[j - 1])
            z, gate, v_first = _rwkv_layer(xs, norm_mix[i], v_first, p)
            w_o = rwkv_w_o[j]
        xs = _mix_ffn_layer(xs, z, gate, w_o, norm_ffn[i], ffn_w_in[i], ffn_conv_w[i], ffn_conv_b[i],
                            ffn_w_out[i], norm_final if i == depth - 1 else None)
    return xs.reshape(batch, seq, d)
```

```python
import functools
import math

import jax
import jax.numpy as jnp
from jax import lax
from jax.experimental import pallas as pl
from jax.experimental.pallas import tpu as pltpu

NORM_EPS = 1e-6
GN_EPS = 64e-5
MLA_HEADS = 8
QK_NOPE_DIM = 128
QK_ROPE_DIM = 64
QK_DIM = QK_NOPE_DIM + QK_ROPE_DIM
QK_PAD = 256
V_HEAD_DIM = 128
V_ROWS = V_HEAD_DIM + 16
Q_LORA_RANK = 384
KV_LORA_RANK = 256
ROPE_THETA = 10000.0
RWKV_HEAD = 64
LANES = 128
D_FF = 2816
HALO = 16
WKV_CHUNK = 64
WKV_TILE = 256
WKV_STEP = 512
SUBLANES = 8
ATTN_TK = 512
ATTN_CHAINS = 8

BF16 = jnp.bfloat16
F32 = jnp.float32
VMEM_LIMIT = 56 * 1024 * 1024


def _params(*sem):
    return pltpu.CompilerParams(dimension_semantics=sem, vmem_limit_bytes=VMEM_LIMIT)


def _dot(a, b, precision=None):
    return jnp.dot(a, b, preferred_element_type=F32, precision=precision)


def _dot_nt(a, b, precision=None):
    return lax.dot_general(a, b, (((1,), (1,)), ((), ())),
                           preferred_element_type=F32, precision=precision)


def _dot_tn(a, b, precision=None):
    return lax.dot_general(a, b, (((0,), (0,)), ((), ())),
                           preferred_element_type=F32, precision=precision)


def _rms(x, g):
    return x * lax.rsqrt(jnp.mean(x * x, axis=-1, keepdims=True) + NORM_EPS) * g


def _row_tile(s, want):
    return min(want, s)


def _full(shape):
    return pl.BlockSpec(shape, lambda *_: (0,) * len(shape))


def _rope_table_kernel(pos_ref, freq_ref, cos_ref, sin_ref):
    ang = pos_ref[...] * freq_ref[...]
    cos_ref[...] = jnp.cos(ang)
    sin_ref[...] = jnp.sin(ang)


def _rope_tables(pos_col, freq_row):
    s = pos_col.shape[0]
    tm = _row_tile(s, 1024)
    return pl.pallas_call(
        _rope_table_kernel,
        grid=(s // tm,),
        in_specs=[pl.BlockSpec((tm, 1), lambda i: (i, 0)), _full((1, LANES))],
        out_specs=[pl.BlockSpec((tm, LANES), lambda i: (i, 0))] * 2,
        out_shape=[jax.ShapeDtypeStruct((s, LANES), F32)] * 2,
        compiler_params=_params("parallel"),
        name="rope_tables",
    )(pos_col, freq_row)


def _mla_pre_kernel(x_ref, g_ref, w_in_ref, qn_ref, w_uq_ref, kvn_ref, w_ukv_ref,
                    cos_ref, sin_ref, q_ref, k_ref, v_ref):
    h = _rms(x_ref[...], g_ref[...]).astype(BF16)
    lat = _dot(h, w_in_ref[...])
    q_lat = lat[:, :Q_LORA_RANK]
    c_kv = lat[:, Q_LORA_RANK:Q_LORA_RANK + KV_LORA_RANK]
    kr = lat[:, Q_LORA_RANK + KV_LORA_RANK:]
    cos = cos_ref[...]
    sin = sin_ref[...]
    k_rope = (kr[:, :QK_ROPE_DIM] * cos[:, :QK_ROPE_DIM]
              + kr[:, QK_ROPE_DIM:] * sin[:, :QK_ROPE_DIM])
    qn = _rms(q_lat, qn_ref[...]).astype(BF16)
    q_all = _dot(qn, w_uq_ref[...])
    n_nope = MLA_HEADS * QK_NOPE_DIM
    n_rope = MLA_HEADS * QK_ROPE_DIM
    reps = n_rope // LANES
    qscale = QK_DIM ** -0.5 * math.log2(math.e)
    q_rope = (q_all[:, n_nope:n_nope + n_rope] * jnp.tile(cos, (1, reps))
              + q_all[:, n_nope + n_rope:] * jnp.tile(sin, (1, reps))) * qscale
    kvn = _rms(c_kv, kvn_ref[...]).astype(BF16)
    kv = _dot(kvn, w_ukv_ref[...])
    tm = x_ref.shape[0]
    qn_t = (q_all[:, :n_nope] * qscale).T.astype(BF16)
    qr_t = q_rope.T.astype(BF16)
    v_t = kv[:, n_nope:].T.astype(BF16)
    zpad_t = jnp.zeros((QK_PAD - QK_DIM, tm), BF16)
    ones_t = jnp.ones((V_ROWS - V_HEAD_DIM, tm), BF16)
    zpad = jnp.zeros((tm, QK_PAD - QK_DIM), BF16)
    k_rope_b = k_rope.astype(BF16)
    for hd in range(MLA_HEADS):
        q_ref[hd, 0:QK_NOPE_DIM, :] = qn_t[hd * QK_NOPE_DIM:(hd + 1) * QK_NOPE_DIM]
        q_ref[hd, QK_NOPE_DIM:QK_DIM, :] = qr_t[hd * QK_ROPE_DIM:(hd + 1) * QK_ROPE_DIM]
        q_ref[hd, QK_DIM:QK_PAD, :] = zpad_t
        k_ref[hd, :, 0:QK_NOPE_DIM] = kv[:, hd * QK_NOPE_DIM:(hd + 1) * QK_NOPE_DIM].astype(BF16)
        k_ref[hd, :, QK_NOPE_DIM:QK_DIM] = k_rope_b
        k_ref[hd, :, QK_DIM:QK_PAD] = zpad
        v_ref[hd, 0, 0:V_HEAD_DIM, :] = v_t[hd * V_HEAD_DIM:(hd + 1) * V_HEAD_DIM]
        v_ref[hd, 0, V_HEAD_DIM:V_ROWS, :] = ones_t


def _mla_pre(x, g, w_in_e, q_norm, w_uq_e, kv_norm, w_ukv_p, cos_t, sin_t):
    s, d = x.shape
    tm = _row_tile(s, ATTN_TK)
    row = lambda w: pl.BlockSpec((tm, w), lambda i: (i, 0))
    return pl.pallas_call(
        _mla_pre_kernel,
        grid=(s // tm,),
        in_specs=[row(d), _full(g.shape), _full(w_in_e.shape), _full(q_norm.shape),
                  _full(w_uq_e.shape), _full(kv_norm.shape), _full(w_ukv_p.shape),
                  row(LANES), row(LANES)],
        out_specs=[pl.BlockSpec((MLA_HEADS, QK_PAD, tm), lambda i: (0, 0, i)),
                   pl.BlockSpec((MLA_HEADS, tm, QK_PAD), lambda i: (0, i, 0)),
                   pl.BlockSpec((MLA_HEADS, 1, V_ROWS, tm), lambda i: (0, i, 0, 0))],
        out_shape=[jax.ShapeDtypeStruct((MLA_HEADS, QK_PAD, s), BF16),
                   jax.ShapeDtypeStruct((MLA_HEADS, s, QK_PAD), BF16),
                   jax.ShapeDtypeStruct((MLA_HEADS, s // tm, V_ROWS, tm), BF16)],
        compiler_params=_params("parallel"),
        name="mla_pre",
    )(x, g, w_in_e, q_norm, w_uq_e, kv_norm, w_ukv_p, cos_t, sin_t)


def _attn_kernel(q_ref, k_ref, v_ref, o_ref, *scratch, tk, n_chains):
    qi = pl.program_id(1)
    accs = scratch[:n_chains]
    sbuf = [scratch[n_chains + 2 * c:n_chains + 2 * c + 2] for c in range(n_chains)]
    for acc in accs:
        acc[...] = jnp.zeros(acc.shape, F32)

    def scores(task):
        c, j, par, _ = task
        q_t = q_ref[:, c * tk:(c + 1) * tk]
        off = pl.multiple_of(j * tk, tk)
        s = _dot(k_ref[pl.ds(off, tk), :], q_t)
        sbuf[c][par][...] = s
        return jnp.max(s, axis=0, keepdims=True)

    def absorb(task, m_blk, m_old):
        c, j, par, masked = task
        s = sbuf[c][par][...]
        if masked:
            key = lax.broadcasted_iota(jnp.int32, s.shape, 0)
            qry = lax.broadcasted_iota(jnp.int32, s.shape, 1)
            s = jnp.where(key <= qry, s, jnp.finfo(F32).min)
            m_blk = jnp.max(s, axis=0, keepdims=True)
        m_new = jnp.maximum(m_old, m_blk)
        p = jnp.exp2(s - m_new)
        alpha = jnp.exp2(m_old - m_new)
        accs[c][...] = alpha * accs[c][...] + _dot(v_ref[j], p.astype(BF16))
        return m_new

    def run(tasks, lookahead, ms, pending):
        ms = list(ms)
        for i, task in enumerate(tasks):
            nxt = tasks[i + 1] if i + 1 < len(tasks) else lookahead
            staged = scores(nxt) if nxt is not None else None
            ms[task[0]] = absorb(task, pending, ms[task[0]])
            pending = staged
        return ms, pending

    chains = range(n_chains)
    n_full = n_chains * qi

    def body(jj, carry):
        j = 2 * jj
        tasks = [(c, j, 0, False) for c in chains] + [(c, j + 1, 1, False) for c in chains]
        ms, pending = run(tasks, (0, j + 2, 0, False), carry[:-1], carry[-1])
        return (*ms, pending)

    m0 = jnp.full((1, tk), -1e30, F32)
    carry = lax.fori_loop(0, n_full // 2, body, (*[m0] * n_chains, scores((0, 0, 0, False))))
    tail = [(c, n_full + kb, kb % 2, c == kb) for kb in chains for c in range(kb, n_chains)]
    run(tail, None, carry[:-1], carry[-1])
    for c, acc in enumerate(accs):
        out = acc[0:V_HEAD_DIM, :] / acc[V_HEAD_DIM:V_HEAD_DIM + 1, :]
        o_ref[c * tk:(c + 1) * tk, :] = out.T.astype(o_ref.dtype)


def _attention(q_t, k_cat, v_t):
    nh, s, _ = k_cat.shape
    tk = v_t.shape[-1]
    n_chains = min(ATTN_CHAINS, s // tk)
    assert n_chains % 2 == 0
    tq = n_chains * tk
    return pl.pallas_call(
        functools.partial(_attn_kernel, tk=tk, n_chains=n_chains),
        grid=(nh, s // tq),
        in_specs=[pl.BlockSpec((None, QK_PAD, tq), lambda h, i: (h, 0, i)),
                  pl.BlockSpec((None, s, QK_PAD), lambda h, i: (h, 0, 0)),
                  pl.BlockSpec((None, s // tk, V_ROWS, tk), lambda h, i: (h, 0, 0, 0))],
        out_specs=pl.BlockSpec((tq, V_HEAD_DIM), lambda h, i: (i, h)),
        out_shape=jax.ShapeDtypeStruct((s, nh * V_HEAD_DIM), BF16),
        scratch_shapes=([pltpu.VMEM((V_ROWS, tk), F32)] * n_chains
                        + [pltpu.VMEM((tk, tk), F32)] * (2 * n_chains)),
        compiler_params=_params("parallel", "arbitrary"),
        name="mla_attention",
    )(q_t, k_cat, v_t)


FFN_CHUNKS = ((0, 1024), (1024, 2048), (2048, D_FF))


def _mix_ffn_kernel(*refs, gated, final):
    refs = list(refs)
    o_ref = refs.pop()
    fg_ref = refs.pop() if final else None
    x_ref, xp_ref, z_ref, zp_ref = refs[:4]
    gt_ref, gtp_ref = refs[4:6] if gated else (None, None)
    wo_ref, g_ref, wi_ref, cw_ref, cb_ref, wout_ref = refs[-6:]
    i = pl.program_id(0)

    def mixed(xr, zr, gr):
        z = zr[...]
        if gated:
            z = z * gr[...]
        return xr[...] + _dot(z.astype(BF16), wo_ref[...])

    x1 = mixed(x_ref, z_ref, gt_ref)
    x1p = mixed(xp_ref, zp_ref, gtp_ref)
    keep = (i > 0).astype(F32)
    hn = jnp.concatenate([(_rms(x1p, g_ref[...]) * keep).astype(BF16),
                          _rms(x1, g_ref[...]).astype(BF16)], axis=0)
    def gate_up(lo, hi):
        return _dot(hn, wi_ref[:, lo:hi]), _dot(hn[HALO:], wi_ref[:, D_FF + lo:D_FF + hi])

    acc = x1
    nxt = gate_up(*FFN_CHUNKS[0])
    for n, (lo, hi) in enumerate(FFN_CHUNKS):
        gate, up = nxt
        if n + 1 < len(FFN_CHUNKS):
            nxt = gate_up(*FFN_CHUNKS[n + 1])
        conv = (gate[HALO:] * cw_ref[2:3, lo:hi]
                + pltpu.roll(gate, 1, 0)[HALO:] * cw_ref[1:2, lo:hi]
                + pltpu.roll(gate, 2, 0)[HALO:] * cw_ref[0:1, lo:hi] + cb_ref[:, lo:hi])
        act = (conv * jax.nn.sigmoid(conv) * up).astype(BF16)
        acc = acc + _dot(act, wout_ref[lo:hi, :])
    o_ref[...] = _rms(acc, fg_ref[...]) if final else acc


def _mix_ffn(x, z, gate, w_o, g, w_in, conv_w, conv_b, w_out, final_gain=None):
    s, d = x.shape
    tm = _row_tile(s, 512)
    row = pl.BlockSpec((tm, d), lambda i: (i, 0))
    prev = pl.BlockSpec((HALO, d), lambda i: (jnp.maximum(i * (tm // HALO) - 1, 0), 0))
    const = lambda a: pl.BlockSpec(a.shape, lambda i: (0,) * a.ndim, pipeline_mode=pl.Buffered(1))
    gated = gate is not None
    final = final_gain is not None
    consts = [w_o, g, w_in, conv_w, conv_b, w_out] + ([final_gain] if final else [])
    args = [x, x, z, z] + ([gate, gate] if gated else []) + consts
    specs = [row, prev, row, prev] + ([row, prev] if gated else []) + [const(a) for a in consts]
    return pl.pallas_call(
        functools.partial(_mix_ffn_kernel, gated=gated, final=final),
        grid=(s // tm,),
        in_specs=specs,
        out_specs=row,
        out_shape=jax.ShapeDtypeStruct((s, d), F32),
        compiler_params=_params("parallel"),
        name="mix_ffn",
    )(*args)


def _group_sum(x):
    w = 2 * LANES
    r = lax.broadcasted_iota(jnp.int32, (w, w), 0) // RWKV_HEAD
    c = lax.broadcasted_iota(jnp.int32, (w, w), 1) // RWKV_HEAD
    ones_bd = (r == c).astype(BF16)
    hi = x.astype(BF16)
    lo = (x - hi.astype(F32)).astype(BF16)
    cols = [_dot(hi[:, j:j + w], ones_bd) + _dot(lo[:, j:j + w], ones_bd)
            for j in range(0, x.shape[1], w)]
    return cols[0] if len(cols) == 1 else jnp.concatenate(cols, axis=1)


def _rwkv_pre_kernel(*refs, has_vres):
    (x_ref, xp_ref, g_ref, mix_ref, wrkv_ref, w0_ref, w1_ref, w2_ref, a0_ref, a1_ref, a2_ref,
     g1_ref, g2_ref, kk_ref, ka_ref, rk_ref) = refs[:16]
    if has_vres:
        vf_ref, v0_ref, v1_ref, v2_ref = refs[16:20]
        at_o, rt_o, bt_o, kt_o, bh_o, kh_o, vb_o, wc_o, bonus_o, g_o = refs[20:]
    else:
        at_o, rt_o, bt_o, kt_o, bh_o, kh_o, vb_o, wc_o, bonus_o, g_o, v_o = refs[16:]
    i = pl.program_id(0)
    tm = x_ref.shape[0]
    h = _rms(x_ref[...], g_ref[...])
    hp = _rms(xp_ref[...], g_ref[...])[HALO - 1:HALO] * (i > 0).astype(F32)
    row = lax.broadcasted_iota(jnp.int32, h.shape, 0)
    shifted = jnp.where(row == 0, jnp.broadcast_to(hp, h.shape), pltpu.roll(h, 1, 0))
    xx = shifted - h
    mix = mix_ref[...]
    xm = lambda n: (h + xx * mix[n:n + 1]).astype(BF16)
    xv = xm(2)
    r = _dot(xm(0), wrkv_ref[0])
    k = _dot(xm(1), wrkv_ref[1])
    v = _dot(xv, wrkv_ref[2])
    lora_w = _dot(jnp.tanh(_dot(xm(3), w1_ref[...])).astype(BF16), w2_ref[...])
    lw = -math.exp(-0.5) * jax.nn.sigmoid(w0_ref[...] + lora_w)
    a = jax.nn.sigmoid(a0_ref[...] + _dot(_dot(xm(4), a1_ref[...]).astype(BF16), a2_ref[...]))
    g_o[...] = _dot(jax.nn.sigmoid(_dot(xm(5), g1_ref[...])).astype(BF16), g2_ref[...])
    if has_vres:
        gate_v = jax.nn.sigmoid(v0_ref[...] + _dot(_dot(xv, v1_ref[...]).astype(BF16), v2_ref[...]))
        v = v + (vf_ref[...] - v) * gate_v
    else:
        v_o[...] = v
    kk = k * kk_ref[...]
    kk = kk * lax.rsqrt(jnp.maximum(_group_sum(kk * kk), 1e-24))
    k = k * (1.0 + (a - 1.0) * ka_ref[...])
    bonus_o[...] = _group_sum(r * k * rk_ref[...]) * v
    C = WKV_CHUNK
    rr = lax.broadcasted_iota(jnp.int32, (tm, tm), 0)
    cc = lax.broadcasted_iota(jnp.int32, (tm, tm), 1)
    tri = (((rr // C) == (cc // C)) & (cc <= rr)).astype(BF16)
    hi = lw.astype(BF16)
    rem = lw - hi.astype(F32)
    mid = rem.astype(BF16)
    lo = (rem - mid.astype(F32)).astype(BF16)
    cl = _dot(tri, hi) + _dot(tri, mid) + _dot(tri, lo)
    ends = [cl[c * C + C - 1:c * C + C] for c in range(tm // C)]
    cl_end = jnp.concatenate([jnp.broadcast_to(e, (C, e.shape[1])) for e in ends], axis=0)
    e_neg = jnp.exp(-cl)
    e_rem = jnp.exp(cl_end - cl)
    at_o[...] = (-kk * jnp.exp(cl - lw)).astype(BF16)
    rt_o[...] = (r * jnp.exp(cl)).astype(BF16)
    bt_o[...] = (kk * a * e_neg).astype(BF16)
    kt_o[...] = (k * e_neg).astype(BF16)
    bh_o[...] = (kk * a * e_rem).astype(BF16)
    kh_o[...] = (k * e_rem).astype(BF16)
    vb_o[...] = v.astype(BF16)
    for c, e in enumerate(ends):
        wc_o[c] = jnp.broadcast_to(jnp.exp(e), wc_o.shape[1:])


def _rwkv_pre(x, g, p, v_first):
    s, d = x.shape
    tm = _row_tile(s, WKV_TILE)
    row = pl.BlockSpec((tm, d), lambda i: (i, 0))
    prev = pl.BlockSpec((HALO, d), lambda i: (jnp.maximum(i * (tm // HALO) - 1, 0), 0))
    names = ["mix", "w_rkv", "w0", "w1", "w2", "a0", "a1", "a2", "g1", "g2", "k_k", "k_a", "r_k"]
    args = [x, x, g] + [p[n] for n in names]
    specs = [row, prev, _full(g.shape)] + [_full(p[n].shape) for n in names]
    has_vres = v_first is not None
    if has_vres:
        args += [v_first, p["v0"], p["v1"], p["v2"]]
        specs += [row] + [_full(p[n].shape) for n in ("v0", "v1", "v2")]
    n_ch = tm // WKV_CHUNK
    wc_spec = pl.BlockSpec((n_ch, SUBLANES, d), lambda i: (i, 0, 0))
    out_specs = [row] * 7 + [wc_spec, row, row]
    out_shape = ([jax.ShapeDtypeStruct((s, d), BF16)] * 7
                 + [jax.ShapeDtypeStruct((s // WKV_CHUNK, SUBLANES, d), F32)]
                 + [jax.ShapeDtypeStruct((s, d), F32)] * 2)
    if not has_vres:
        out_specs.append(row)
        out_shape.append(jax.ShapeDtypeStruct((s, d), F32))
    return pl.pallas_call(
        functools.partial(_rwkv_pre_kernel, has_vres=has_vres),
        grid=(s // tm,),
        in_specs=specs,
        out_specs=out_specs,
        out_shape=out_shape,
        compiler_params=_params("parallel"),
        name="rwkv_pre",
    )(*args)


def _wkv_kernel(at_ref, rt_ref, bt_ref, kt_ref, bh_ref, kh_ref, v_ref, wc_ref, bonus_ref,
                lnw_ref, lnb_ref, o_ref, s_ref, *, group, prec):
    C = WKV_CHUNK
    C2 = 2 * C
    n_pairs = o_ref.shape[1] // LANES
    n_groups = o_ref.shape[0] // (C * group)

    @pl.when(pl.program_id(0) == 0)
    def _():
        s_ref[...] = jnp.zeros(s_ref.shape, F32)

    lane = lax.broadcasted_iota(jnp.int32, (1, LANES), 1)
    head0 = lane < RWKV_HEAD
    m0 = head0.astype(F32)
    m1 = 1.0 - m0
    t2 = lax.broadcasted_iota(jnp.int32, (C2, C2), 0)
    s2 = lax.broadcasted_iota(jnp.int32, (C2, C2), 1)
    same_head = (t2 // C) == (s2 // C)
    strict = (same_head & (s2 < t2)).astype(F32)
    incl = (same_head & (s2 <= t2)).astype(F32)
    eye = (t2 == s2).astype(F32)

    def level_mask(b):
        return (((t2 // (2 * b)) == (s2 // (2 * b))) & ((t2 % (2 * b)) >= b)
                & ((s2 % (2 * b)) < b)).astype(F32)

    levels = []
    b_sz = 1
    while b_sz < C:
        levels.append(level_mask(b_sz))
        b_sz *= 2

    def stack(x):
        zero = jnp.zeros_like(x)
        return jnp.concatenate([jnp.where(head0, x, zero), jnp.where(head0, zero, x)], axis=0)

    def twice(x):
        return jnp.concatenate([x, x], axis=0)

    def cast(x):
        return x.astype(prec)

    def group_body(g, carry):
        probs = [(ci, p) for ci in range(group) for p in range(n_pairs)]
        offs = [pl.multiple_of((g * group + ci) * C, C) for ci in range(group)]

        def load(ref, ci, p):
            return cast(ref[pl.ds(offs[ci], C), p * LANES:(p + 1) * LANES])

        a_s = [stack(load(at_ref, ci, p)) for ci, p in probs]
        r_s = [stack(load(rt_ref, ci, p)) for ci, p in probs]
        v_s = [stack(load(v_ref, ci, p)) for ci, p in probs]
        hat = [jnp.concatenate([stack(load(bh_ref, ci, p)), stack(load(kh_ref, ci, p))], axis=0)
               for ci, p in probs]
        til = [jnp.concatenate([twice(load(bt_ref, ci, p)), twice(load(kt_ref, ci, p))], axis=0)
               for ci, p in probs]
        lm = [_dot_nt(jnp.concatenate([a, r], axis=0), t) for a, r, t in zip(a_s, r_s, til)]
        l_ab = [x[:C2, :C2] * strict for x in lm]
        l_ak = [cast(x[:C2, C2:] * strict) for x in lm]
        m_cat = [cast(jnp.concatenate([x[C2:, :C2] * incl, x[C2:, C2:] * incl], axis=1))
                 for x in lm]
        t_inv = [eye + l * levels[0] for l in l_ab]
        for lvl in levels[1:]:
            t_b = [cast(t) for t in t_inv]
            z = [_dot(t, cast(l * lvl)) for t, l in zip(t_b, l_ab)]
            t_inv = [t + _dot(cast(zz), tb) for t, zz, tb in zip(t_inv, z, t_b)]
        t_b = [cast(t) for t in t_inv]
        p_loc = [_dot(l, v) for l, v in zip(l_ak, v_s)]
        for ci in range(group):
            rows = pl.ds(offs[ci], C)
            c_idx = g * group + ci
            sel = [ci * n_pairs + p for p in range(n_pairs)]
            st = [s_ref[p] for p in range(n_pairs)]
            st_b = [cast(s) for s in st]
            x = [_dot_nt(a_s[i], sb) + p_loc[i] for i, sb in zip(sel, st_b)]
            u = [cast(_dot(t_b[i], cast(xx))) for i, xx in zip(sel, x)]
            uv = [jnp.concatenate([uu, v_s[i]], axis=0) for i, uu in zip(sel, u)]
            y = [_dot_nt(r_s[i], sb) + _dot(m_cat[i], w) for i, sb, w in zip(sel, st_b, uv)]
            for p in range(n_pairs):
                cols = slice(p * LANES, (p + 1) * LANES)
                w_end = wc_ref[c_idx, 0:1, cols]
                s_ref[p] = st[p] * w_end + _dot_tn(uv[p], hat[sel[p]])
                yp = y[p][:C] + y[p][C:]
                mu = (jnp.sum(yp * m0, axis=-1, keepdims=True) * m0
                      + jnp.sum(yp * m1, axis=-1, keepdims=True) * m1) * (1.0 / RWKV_HEAD)
                yc = yp - mu
                yc2 = yc * yc
                var = (jnp.sum(yc2 * m0, axis=-1, keepdims=True) * m0
                       + jnp.sum(yc2 * m1, axis=-1, keepdims=True) * m1) * (1.0 / RWKV_HEAD)
                yn = yc * lax.rsqrt(var + GN_EPS)
                o_ref[rows, cols] = yn * lnw_ref[:, cols] + lnb_ref[:, cols] + bonus_ref[rows, cols]
        return carry

    lax.fori_loop(0, n_groups, group_body, 0)


def _wkv(at, rt, bt, kt, bh, kh, vb, wc, bonus, ln_w, ln_b, prec=BF16):
    s, d = at.shape
    tt = _row_tile(s, WKV_STEP)
    group = 2
    row = pl.BlockSpec((tt, d), lambda t: (t, 0))
    return pl.pallas_call(
        functools.partial(_wkv_kernel, group=group, prec=prec),
        grid=(s // tt,),
        in_specs=[row] * 7 + [pl.BlockSpec((tt // WKV_CHUNK, SUBLANES, d), lambda t: (t, 0, 0)),
                              row, _full(ln_w.shape), _full(ln_b.shape)],
        out_specs=row,
        out_shape=jax.ShapeDtypeStruct((s, d), F32),
        scratch_shapes=[pltpu.VMEM((d // LANES, LANES, LANES), F32)],
        compiler_params=_params("arbitrary"),
        name="wkv7_scan",
    )(at, rt, bt, kt, bh, kh, vb, wc, bonus, ln_w, ln_b)


def _rot_cols(w):
    half = QK_ROPE_DIM // 2
    return jnp.concatenate([-w[..., half:], w[..., :half]], axis=-1)


def _mla_weights(w_in, w_uq, w_ukv):
    d = w_in.shape[0]
    lat = Q_LORA_RANK + KV_LORA_RANK
    w_kr = w_in[:, lat:]
    w_in_e = jnp.concatenate([w_in[:, :lat], w_kr, _rot_cols(w_kr)], axis=1).astype(BF16)
    uq = w_uq.reshape(Q_LORA_RANK, MLA_HEADS, QK_DIM)
    uq_rope = uq[:, :, QK_NOPE_DIM:]
    w_uq_e = jnp.concatenate(
        [uq[:, :, :QK_NOPE_DIM].reshape(Q_LORA_RANK, -1),
         uq_rope.reshape(Q_LORA_RANK, -1),
         _rot_cols(uq_rope).reshape(Q_LORA_RANK, -1)], axis=1).astype(BF16)
    ukv = w_ukv.reshape(KV_LORA_RANK, MLA_HEADS, QK_NOPE_DIM + V_HEAD_DIM)
    w_ukv_p = jnp.concatenate(
        [ukv[:, :, :QK_NOPE_DIM].reshape(KV_LORA_RANK, -1),
         ukv[:, :, QK_NOPE_DIM:].reshape(KV_LORA_RANK, -1)], axis=1).astype(BF16)
    del d
    return w_in_e, w_uq_e, w_ukv_p


def _mla_layer(x, g, cos_t, sin_t, w_in, q_norm, w_uq, kv_norm, w_ukv):
    w_in_e, w_uq_e, w_ukv_p = _mla_weights(w_in, w_uq, w_ukv)
    q_cat, k_cat, v = _mla_pre(x, g[None], w_in_e, q_norm[None], w_uq_e, kv_norm[None],
                               w_ukv_p, cos_t, sin_t)
    return _attention(q_cat, k_cat, v)


def _rwkv_layer(x, g, v_first, p, wkv_prec=BF16):
    row = lambda t: t.reshape(1, -1)
    pp = {
        "mix": p["mix"], "w_rkv": p["w_rkv"].astype(BF16),
        "w0": row(p["w0"]), "w1": p["w1"].astype(BF16), "w2": p["w2"].astype(BF16),
        "a0": row(p["a0"]), "a1": p["a1"].astype(BF16), "a2": p["a2"].astype(BF16),
        "g1": p["g1"].astype(BF16), "g2": p["g2"].astype(BF16),
        "k_k": row(p["k_k"]), "k_a": row(p["k_a"]), "r_k": row(p["r_k"]),
    }
    if v_first is not None:
        pp.update(v0=row(p["v0"]), v1=p["v1"].astype(BF16), v2=p["v2"].astype(BF16))
    outs = _rwkv_pre(x, g[None], pp, v_first)
    scan_in, gate = outs[:9], outs[9]
    z = _wkv(*scan_in, row(p["ln_w"]), row(p["ln_b"]), prec=wkv_prec)
    return z, gate, (outs[10] if v_first is None else v_first)


def _mix_ffn_layer(x, z, gate, w_o, g, w_in, conv_w, conv_b, w_out, final_gain=None):
    return _mix_ffn(x, z, gate, w_o.astype(BF16), g[None], w_in.astype(BF16), conv_w, conv_b[None],
                    w_out.astype(BF16), None if final_gain is None else final_gain[None])


def kernel(x, positions, norm_mix, norm_ffn, norm_final, mla_w_in, mla_q_norm, mla_w_uq, mla_kv_norm, mla_w_ukv, mla_w_o, rwkv_mix, rwkv_w_rkv, rwkv_w0, rwkv_w1, rwkv_w2, rwkv_a0, rwkv_a1, rwkv_a2, rwkv_v0, rwkv_v1, rwkv_v2, rwkv_g1, rwkv_g2, rwkv_k_k, rwkv_k_a, rwkv_r_k, rwkv_ln_w, rwkv_ln_b, rwkv_w_o, ffn_w_in, ffn_conv_w, ffn_conv_b, ffn_w_out):
    batch, seq, d = x.shape
    assert batch == 1
    depth = norm_mix.shape[0]
    half = QK_ROPE_DIM // 2
    inv_freq = ROPE_THETA ** (-jnp.arange(half, dtype=F32) / half)
    freq_row = jnp.tile(inv_freq, LANES // half)[None]
    cos_t, sin_t = _rope_tables(positions.astype(F32).reshape(seq, 1), freq_row)
    xs = x.reshape(seq, d)
    v_first = None
    for i in range(depth):
        j = i // 2
        if i % 2 == 0:
            z = _mla_layer(xs, norm_mix[i], cos_t, sin_t, mla_w_in[j], mla_q_norm[j], mla_w_uq[j],
                           mla_kv_norm[j], mla_w_ukv[j])
            gate, w_o = None, mla_w_o[j]
        else:
            p = {"mix": rwkv_mix[j], "w_rkv": rwkv_w_rkv[j], "w0": rwkv_w0[j], "w1": rwkv_w1[j],
                 "w2": rwkv_w2[j], "a0": rwkv_a0[j], "a1": rwkv_a1[j], "a2": rwkv_a2[j],
                 "g1": rwkv_g1[j], "g2": rwkv_g2[j], "k_k": rwkv_k_k[j], "k_a": rwkv_k_a[j],
                 "r_k": rwkv_r_k[j], "ln_w": rwkv_ln_w[j], "ln_b": rwkv_ln_b[j]}
            if j > 0:
                p.update(v0=rwkv_v0[j - 1], v1=rwkv_v1[j - 1], v2=rwkv_v2[j - 1])
            z, gate, v_first = _rwkv_layer(xs, norm_mix[i], v_first, p)
            w_o = rwkv_w_o[j]
        xs = _mix_ffn_layer(xs, z, gate, w_o, norm_ffn[i], ffn_w_in[i], ffn_conv_w[i], ffn_conv_b[i],
                            ffn_w_out[i], norm_final if i == depth - 1 else None)
    return xs.reshape(batch, seq, d)
```

```python
import functools
import math

import jax
import jax.numpy as jnp
from jax import lax
from jax.experimental import pallas as pl
from jax.experimental.pallas import tpu as pltpu

NORM_EPS = 1e-6
GN_EPS = 64e-5
MLA_HEADS = 8
QK_NOPE_DIM = 128
QK_ROPE_DIM = 64
QK_DIM = QK_NOPE_DIM + QK_ROPE_DIM
QK_PAD = 256
V_HEAD_DIM = 128
V_ROWS = V_HEAD_DIM + 16
Q_LORA_RANK = 384
KV_LORA_RANK = 256
ROPE_THETA = 10000.0
RWKV_HEAD = 64
LANES = 128
D_FF = 2816
HALO = 16
WKV_CHUNK = 64
WKV_TILE = 256
WKV_STEP = 512
SUBLANES = 8
ATTN_TK = 512
ATTN_CHAINS = 8

BF16 = jnp.bfloat16
F32 = jnp.float32
VMEM_LIMIT = 56 * 1024 * 1024


def _params(*sem):
    return pltpu.CompilerParams(dimension_semantics=sem, vmem_limit_bytes=VMEM_LIMIT)


def _dot(a, b, precision=None):
    return jnp.dot(a, b, preferred_element_type=F32, precision=precision)


def _dot_nt(a, b, precision=None):
    return lax.dot_general(a, b, (((1,), (1,)), ((), ())),
                           preferred_element_type=F32, precision=precision)


def _dot_tn(a, b, precision=None):
    return lax.dot_general(a, b, (((0,), (0,)), ((), ())),
                           preferred_element_type=F32, precision=precision)


def _rms(x, g):
    return x * lax.rsqrt(jnp.mean(x * x, axis=-1, keepdims=True) + NORM_EPS) * g


def _row_tile(s, want):
    return min(want, s)


def _full(shape):
    return pl.BlockSpec(shape, lambda *_: (0,) * len(shape))


def _rope_table_kernel(pos_ref, freq_ref, cos_ref, sin_ref):
    ang = pos_ref[...] * freq_ref[...]
    cos_ref[...] = jnp.cos(ang)
    sin_ref[...] = jnp.sin(ang)


def _rope_tables(pos_col, freq_row):
    s = pos_col.shape[0]
    tm = _row_tile(s, 1024)
    return pl.pallas_call(
        _rope_table_kernel,
        grid=(s // tm,),
        in_specs=[pl.BlockSpec((tm, 1), lambda i: (i, 0)), _full((1, LANES))],
        out_specs=[pl.BlockSpec((tm, LANES), lambda i: (i, 0))] * 2,
        out_shape=[jax.ShapeDtypeStruct((s, LANES), F32)] * 2,
        compiler_params=_params("parallel"),
        name="rope_tables",
    )(pos_col, freq_row)


def _mla_pre_kernel(x_ref, g_ref, w_in_ref, qn_ref, w_uq_ref, kvn_ref, w_ukv_ref,
                    cos_ref, sin_ref, q_ref, k_ref, v_ref):
    h = _rms(x_ref[...], g_ref[...]).astype(BF16)
    lat = _dot(h, w_in_ref[...])
    q_lat = lat[:, :Q_LORA_RANK]
    c_kv = lat[:, Q_LORA_RANK:Q_LORA_RANK + KV_LORA_RANK]
    kr = lat[:, Q_LORA_RANK + KV_LORA_RANK:]
    cos = cos_ref[...]
    sin = sin_ref[...]
    k_rope = (kr[:, :QK_ROPE_DIM] * cos[:, :QK_ROPE_DIM]
              + kr[:, QK_ROPE_DIM:] * sin[:, :QK_ROPE_DIM])
    qn = _rms(q_lat, qn_ref[...]).astype(BF16)
    q_all = _dot(qn, w_uq_ref[...])
    n_nope = MLA_HEADS * QK_NOPE_DIM
    n_rope = MLA_HEADS * QK_ROPE_DIM
    reps = n_rope // LANES
    qscale = QK_DIM ** -0.5 * math.log2(math.e)
    q_rope = (q_all[:, n_nope:n_nope + n_rope] * jnp.tile(cos, (1, reps))
              + q_all[:, n_nope + n_rope:] * jnp.tile(sin, (1, reps))) * qscale
    kvn = _rms(c_kv, kvn_ref[...]).astype(BF16)
    kv = _dot(kvn, w_ukv_ref[...])
    tm = x_ref.shape[0]
    qn_t = (q_all[:, :n_nope] * qscale).T.astype(BF16)
    qr_t = q_rope.T.astype(BF16)
    v_t = kv[:, n_nope:].T.astype(BF16)
    zpad_t = jnp.zeros((QK_PAD - QK_DIM, tm), BF16)
    ones_t = jnp.ones((V_ROWS - V_HEAD_DIM, tm), BF16)
    zpad = jnp.zeros((tm, QK_PAD - QK_DIM), BF16)
    k_rope_b = k_rope.astype(BF16)
    for hd in range(MLA_HEADS):
        q_ref[hd, 0:QK_NOPE_DIM, :] = qn_t[hd * QK_NOPE_DIM:(hd + 1) * QK_NOPE_DIM]
        q_ref[hd, QK_NOPE_DIM:QK_DIM, :] = qr_t[hd * QK_ROPE_DIM:(hd + 1) * QK_ROPE_DIM]
        q_ref[hd, QK_DIM:QK_PAD, :] = zpad_t
        k_ref[hd, :, 0:QK_NOPE_DIM] = kv[:, hd * QK_NOPE_DIM:(hd + 1) * QK_NOPE_DIM].astype(BF16)
        k_ref[hd, :, QK_NOPE_DIM:QK_DIM] = k_rope_b
        k_ref[hd, :, QK_DIM:QK_PAD] = zpad
        v_ref[hd, 0, 0:V_HEAD_DIM, :] = v_t[hd * V_HEAD_DIM:(hd + 1) * V_HEAD_DIM]
        v_ref[hd, 0, V_HEAD_DIM:V_ROWS, :] = ones_t


def _mla_pre(x, g, w_in_e, q_norm, w_uq_e, kv_norm, w_ukv_p, cos_t, sin_t):
    s, d = x.shape
    tm = _row_tile(s, ATTN_TK)
    row = lambda w: pl.BlockSpec((tm, w), lambda i: (i, 0))
    return pl.pallas_call(
        _mla_pre_kernel,
        grid=(s // tm,),
        in_specs=[row(d), _full(g.shape), _full(w_in_e.shape), _full(q_norm.shape),
                  _full(w_uq_e.shape), _full(kv_norm.shape), _full(w_ukv_p.shape),
                  row(LANES), row(LANES)],
        out_specs=[pl.BlockSpec((MLA_HEADS, QK_PAD, tm), lambda i: (0, 0, i)),
                   pl.BlockSpec((MLA_HEADS, tm, QK_PAD), lambda i: (0, i, 0)),
                   pl.BlockSpec((MLA_HEADS, 1, V_ROWS, tm), lambda i: (0, i, 0, 0))],
        out_shape=[jax.ShapeDtypeStruct((MLA_HEADS, QK_PAD, s), BF16),
                   jax.ShapeDtypeStruct((MLA_HEADS, s, QK_PAD), BF16),
                   jax.ShapeDtypeStruct((MLA_HEADS, s // tm, V_ROWS, tm), BF16)],
        compiler_params=_params("parallel"),
        name="mla_pre",
    )(x, g, w_in_e, q_norm, w_uq_e, kv_norm, w_ukv_p, cos_t, sin_t)


def _attn_kernel(q_ref, k_ref, v_ref, o_ref, *scratch, tk, n_chains):
    qi = pl.program_id(1)
    accs = scratch[:n_chains]
    sbuf = [scratch[n_chains + 2 * c:n_chains + 2 * c + 2] for c in range(n_chains)]
    for acc in accs:
        acc[...] = jnp.zeros(acc.shape, F32)

    def scores(task):
        c, j, par, _ = task
        q_t = q_ref[:, c * tk:(c + 1) * tk]
        off = pl.multiple_of(j * tk, tk)
        s = _dot(k_ref[pl.ds(off, tk), :], q_t)
        sbuf[c][par][...] = s
        return jnp.max(s, axis=0, keepdims=True)

    def absorb(task, m_blk, m_old):
        c, j, par, masked = task
        s = sbuf[c][par][...]
        if masked:
            key = lax.broadcasted_iota(jnp.int32, s.shape, 0)
            qry = lax.broadcasted_iota(jnp.int32, s.shape, 1)
            s = jnp.where(key <= qry, s, jnp.finfo(F32).min)
            m_blk = jnp.max(s, axis=0, keepdims=True)
        m_new = jnp.maximum(m_old, m_blk)
        p = jnp.exp2(s - m_new)
        alpha = jnp.exp2(m_old - m_new)
        accs[c][...] = alpha * accs[c][...] + _dot(v_ref[j], p.astype(BF16))
        return m_new

    def run(tasks, lookahead, ms, pending):
        ms = list(ms)
        for i, task in enumerate(tasks):
            nxt = tasks[i + 1] if i + 1 < len(tasks) else lookahead
            staged = scores(nxt) if nxt is not None else None
            ms[task[0]] = absorb(task, pending, ms[task[0]])
            pending = staged
        return ms, pending

    chains = range(n_chains)
    n_full = n_chains * qi

    def body(jj, carry):
        j = 2 * jj
        tasks = [(c, j, 0, False) for c in chains] + [(c, j + 1, 1, False) for c in chains]
        ms, pending = run(tasks, (0, j + 2, 0, False), carry[:-1], carry[-1])
        return (*ms, pending)

    m0 = jnp.full((1, tk), -1e30, F32)
    carry = lax.fori_loop(0, n_full // 2, body, (*[m0] * n_chains, scores((0, 0, 0, False))))
    tail = [(c, n_full + kb, kb % 2, c == kb) for kb in chains for c in range(kb, n_chains)]
    run(tail, None, carry[:-1], carry[-1])
    for c, acc in enumerate(accs):
        out = acc[0:V_HEAD_DIM, :] / acc[V_HEAD_DIM:V_HEAD_DIM + 1, :]
        o_ref[c * tk:(c + 1) * tk, :] = out.T.astype(o_ref.dtype)


def _attention(q_t, k_cat, v_t):
    nh, s, _ = k_cat.shape
    tk = v_t.shape[-1]
    n_chains = min(ATTN_CHAINS, s // tk)
    assert n_chains % 2 == 0
    tq = n_chains * tk
    return pl.pallas_call(
        functools.partial(_attn_kernel, tk=tk, n_chains=n_chains),
        grid=(nh, s // tq),
        in_specs=[pl.BlockSpec((None, QK_PAD, tq), lambda h, i: (h, 0, i)),
                  pl.BlockSpec((None, s, QK_PAD), lambda h, i: (h, 0, 0)),
                  pl.BlockSpec((None, s // tk, V_ROWS, tk), lambda h, i: (h, 0, 0, 0))],
        out_specs=pl.BlockSpec((tq, V_HEAD_DIM), lambda h, i: (i, h)),
        out_shape=jax.ShapeDtypeStruct((s, nh * V_HEAD_DIM), BF16),
        scratch_shapes=([pltpu.VMEM((V_ROWS, tk), F32)] * n_chains
                        + [pltpu.VMEM((tk, tk), F32)] * (2 * n_chains)),
        compiler_params=_params("parallel", "arbitrary"),
        name="mla_attention",
    )(q_t, k_cat, v_t)


FFN_CHUNKS = ((0, 1024), (1024, 2048), (2048, D_FF))


def _mix_ffn_kernel(*refs, gated, final):
    refs = list(refs)
    o_ref = refs.pop()
    fg_ref = refs.pop() if final else None
    x_ref, xp_ref, z_ref, zp_ref = refs[:4]
    gt_ref, gtp_ref = refs[4:6] if gated else (None, None)
    wo_ref, g_ref, wi_ref, cw_ref, cb_ref, wout_ref = refs[-6:]
    i = pl.program_id(0)

    def mixed(xr, zr, gr):
        z = zr[...]
        if gated:
            z = z * gr[...]
        return xr[...] + _dot(z.astype(BF16), wo_ref[...])

    x1 = mixed(x_ref, z_ref, gt_ref)
    x1p = mixed(xp_ref, zp_ref, gtp_ref)
    keep = (i > 0).astype(F32)
    hn = jnp.concatenate([(_rms(x1p, g_ref[...]) * keep).astype(BF16),
                          _rms(x1, g_ref[...]).astype(BF16)], axis=0)
    def gate_up(lo, hi):
        return _dot(hn, wi_ref[:, lo:hi]), _dot(hn[HALO:], wi_ref[:, D_FF + lo:D_FF + hi])

    acc = x1
    nxt = gate_up(*FFN_CHUNKS[0])
    for n, (lo, hi) in enumerate(FFN_CHUNKS):
        gate, up = nxt
        if n + 1 < len(FFN_CHUNKS):
            nxt = gate_up(*FFN_CHUNKS[n + 1])
        conv = (gate[HALO:] * cw_ref[2:3, lo:hi]
                + pltpu.roll(gate, 1, 0)[HALO:] * cw_ref[1:2, lo:hi]
                + pltpu.roll(gate, 2, 0)[HALO:] * cw_ref[0:1, lo:hi] + cb_ref[:, lo:hi])
        act = (conv * jax.nn.sigmoid(conv) * up).astype(BF16)
        acc = acc + _dot(act, wout_ref[lo:hi, :])
    o_ref[...] = _rms(acc, fg_ref[...]) if final else acc


def _mix_ffn(x, z, gate, w_o, g, layer, w_in_all, conv_w, conv_b, w_out_all, final_gain=None):
    s, d = x.shape
    tm = _row_tile(s, 512)
    row = pl.BlockSpec((tm, d), lambda i: (i, 0))
    prev = pl.BlockSpec((HALO, d), lambda i: (jnp.maximum(i * (tm // HALO) - 1, 0), 0))

    def const(a, stacked=False):
        if stacked:
            return pl.BlockSpec((None,) + a.shape[1:], lambda i: (layer,) + (0,) * (a.ndim - 1),
                                pipeline_mode=pl.Buffered(1))
        return pl.BlockSpec(a.shape, lambda i: (0,) * a.ndim, pipeline_mode=pl.Buffered(1))

    gated = gate is not None
    final = final_gain is not None
    consts = [w_o, g, w_in_all, conv_w, conv_b, w_out_all] + ([final_gain] if final else [])
    args = [x, x, z, z] + ([gate, gate] if gated else []) + consts
    specs = ([row, prev, row, prev] + ([row, prev] if gated else [])
             + [const(a, stacked=a is w_in_all or a is w_out_all) for a in consts])
    return pl.pallas_call(
        functools.partial(_mix_ffn_kernel, gated=gated, final=final),
        grid=(s // tm,),
        in_specs=specs,
        out_specs=row,
        out_shape=jax.ShapeDtypeStruct((s, d), F32),
        compiler_params=_params("parallel"),
        name="mix_ffn",
    )(*args)


def _group_sum(x):
    w = 2 * LANES
    r = lax.broadcasted_iota(jnp.int32, (w, w), 0) // RWKV_HEAD
    c = lax.broadcasted_iota(jnp.int32, (w, w), 1) // RWKV_HEAD
    ones_bd = (r == c).astype(BF16)
    hi = x.astype(BF16)
    lo = (x - hi.astype(F32)).astype(BF16)
    cols = [_dot(hi[:, j:j + w], ones_bd) + _dot(lo[:, j:j + w], ones_bd)
            for j in range(0, x.shape[1], w)]
    return cols[0] if len(cols) == 1 else jnp.concatenate(cols, axis=1)


def _rwkv_pre_kernel(*refs, has_vres):
    (x_ref, xp_ref, g_ref, mix_ref, wrkv_ref, w0_ref, w1_ref, w2_ref, a0_ref, a1_ref, a2_ref,
     g1_ref, g2_ref, kk_ref, ka_ref, rk_ref) = refs[:16]
    if has_vres:
        vf_ref, v0_ref, v1_ref, v2_ref = refs[16:20]
        at_o, rt_o, bt_o, kt_o, bh_o, kh_o, vb_o, wc_o, bonus_o, g_o = refs[20:]
    else:
        at_o, rt_o, bt_o, kt_o, bh_o, kh_o, vb_o, wc_o, bonus_o, g_o, v_o = refs[16:]
    i = pl.program_id(0)
    tm = x_ref.shape[0]
    h = _rms(x_ref[...], g_ref[...])
    hp = _rms(xp_ref[...], g_ref[...])[HALO - 1:HALO] * (i > 0).astype(F32)
    row = lax.broadcasted_iota(jnp.int32, h.shape, 0)
    shifted = jnp.where(row == 0, jnp.broadcast_to(hp, h.shape), pltpu.roll(h, 1, 0))
    xx = shifted - h
    mix = mix_ref[...]
    xm = lambda n: (h + xx * mix[n:n + 1]).astype(BF16)
    xv = xm(2)
    r = _dot(xm(0), wrkv_ref[0])
    k = _dot(xm(1), wrkv_ref[1])
    v = _dot(xv, wrkv_ref[2])
    lora_w = _dot(jnp.tanh(_dot(xm(3), w1_ref[...])).astype(BF16), w2_ref[...])
    lw = -math.exp(-0.5) * jax.nn.sigmoid(w0_ref[...] + lora_w)
    a = jax.nn.sigmoid(a0_ref[...] + _dot(_dot(xm(4), a1_ref[...]).astype(BF16), a2_ref[...]))
    g_o[...] = _dot(jax.nn.sigmoid(_dot(xm(5), g1_ref[...])).astype(BF16), g2_ref[...])
    if has_vres:
        gate_v = jax.nn.sigmoid(v0_ref[...] + _dot(_dot(xv, v1_ref[...]).astype(BF16), v2_ref[...]))
        v = v + (vf_ref[...] - v) * gate_v
    else:
        v_o[...] = v
    kk = k * kk_ref[...]
    kk = kk * lax.rsqrt(jnp.maximum(_group_sum(kk * kk), 1e-24))
    k = k * (1.0 + (a - 1.0) * ka_ref[...])
    bonus_o[...] = _group_sum(r * k * rk_ref[...]) * v
    C = WKV_CHUNK
    rr = lax.broadcasted_iota(jnp.int32, (tm, tm), 0)
    cc = lax.broadcasted_iota(jnp.int32, (tm, tm), 1)
    tri = (((rr // C) == (cc // C)) & (cc <= rr)).astype(BF16)
    hi = lw.astype(BF16)
    rem = lw - hi.astype(F32)
    mid = rem.astype(BF16)
    lo = (rem - mid.astype(F32)).astype(BF16)
    cl = _dot(tri, hi) + _dot(tri, mid) + _dot(tri, lo)
    ends = [cl[c * C + C - 1:c * C + C] for c in range(tm // C)]
    cl_end = jnp.concatenate([jnp.broadcast_to(e, (C, e.shape[1])) for e in ends], axis=0)
    e_neg = jnp.exp(-cl)
    e_rem = jnp.exp(cl_end - cl)
    at_o[...] = (-kk * jnp.exp(cl - lw)).astype(BF16)
    rt_o[...] = (r * jnp.exp(cl)).astype(BF16)
    bt_o[...] = (kk * a * e_neg).astype(BF16)
    kt_o[...] = (k * e_neg).astype(BF16)
    bh_o[...] = (kk * a * e_rem).astype(BF16)
    kh_o[...] = (k * e_rem).astype(BF16)
    vb_o[...] = v.astype(BF16)
    for c, e in enumerate(ends):
        wc_o[c] = jnp.broadcast_to(jnp.exp(e), wc_o.shape[1:])


def _rwkv_pre(x, g, p, v_first):
    s, d = x.shape
    tm = _row_tile(s, WKV_TILE)
    row = pl.BlockSpec((tm, d), lambda i: (i, 0))
    prev = pl.BlockSpec((HALO, d), lambda i: (jnp.maximum(i * (tm // HALO) - 1, 0), 0))
    names = ["mix", "w_rkv", "w0", "w1", "w2", "a0", "a1", "a2", "g1", "g2", "k_k", "k_a", "r_k"]
    args = [x, x, g] + [p[n] for n in names]
    specs = [row, prev, _full(g.shape)] + [_full(p[n].shape) for n in names]
    has_vres = v_first is not None
    if has_vres:
        args += [v_first, p["v0"], p["v1"], p["v2"]]
        specs += [row] + [_full(p[n].shape) for n in ("v0", "v1", "v2")]
    n_ch = tm // WKV_CHUNK
    wc_spec = pl.BlockSpec((n_ch, SUBLANES, d), lambda i: (i, 0, 0))
    out_specs = [row] * 7 + [wc_spec, row, row]
    out_shape = ([jax.ShapeDtypeStruct((s, d), BF16)] * 7
                 + [jax.ShapeDtypeStruct((s // WKV_CHUNK, SUBLANES, d), F32)]
                 + [jax.ShapeDtypeStruct((s, d), F32)] * 2)
    if not has_vres:
        out_specs.append(row)
        out_shape.append(jax.ShapeDtypeStruct((s, d), F32))
    return pl.pallas_call(
        functools.partial(_rwkv_pre_kernel, has_vres=has_vres),
        grid=(s // tm,),
        in_specs=specs,
        out_specs=out_specs,
        out_shape=out_shape,
        compiler_params=_params("parallel"),
        name="rwkv_pre",
    )(*args)


def _wkv_kernel(at_ref, rt_ref, bt_ref, kt_ref, bh_ref, kh_ref, v_ref, wc_ref, bonus_ref,
                lnw_ref, lnb_ref, o_ref, s_ref, *, group, prec):
    C = WKV_CHUNK
    C2 = 2 * C
    n_pairs = o_ref.shape[1] // LANES
    n_groups = o_ref.shape[0] // (C * group)

    @pl.when(pl.program_id(0) == 0)
    def _():
        s_ref[...] = jnp.zeros(s_ref.shape, F32)

    lane = lax.broadcasted_iota(jnp.int32, (1, LANES), 1)
    head0 = lane < RWKV_HEAD
    m0 = head0.astype(F32)
    m1 = 1.0 - m0
    t2 = lax.broadcasted_iota(jnp.int32, (C2, C2), 0)
    s2 = lax.broadcasted_iota(jnp.int32, (C2, C2), 1)
    same_head = (t2 // C) == (s2 // C)
    strict = (same_head & (s2 < t2)).astype(F32)
    incl = (same_head & (s2 <= t2)).astype(F32)
    eye = (t2 == s2).astype(F32)

    def level_mask(b):
        return (((t2 // (2 * b)) == (s2 // (2 * b))) & ((t2 % (2 * b)) >= b)
                & ((s2 % (2 * b)) < b)).astype(F32)

    levels = []
    b_sz = 1
    while b_sz < C:
        levels.append(level_mask(b_sz))
        b_sz *= 2

    def stack(x):
        zero = jnp.zeros_like(x)
        return jnp.concatenate([jnp.where(head0, x, zero), jnp.where(head0, zero, x)], axis=0)

    def twice(x):
        return jnp.concatenate([x, x], axis=0)

    def cast(x):
        return x.astype(prec)

    def group_body(g, carry):
        probs = [(ci, p) for ci in range(group) for p in range(n_pairs)]
        offs = [pl.multiple_of((g * group + ci) * C, C) for ci in range(group)]

        def load(ref, ci, p):
            return cast(ref[pl.ds(offs[ci], C), p * LANES:(p + 1) * LANES])

        a_s = [stack(load(at_ref, ci, p)) for ci, p in probs]
        r_s = [stack(load(rt_ref, ci, p)) for ci, p in probs]
        v_s = [stack(load(v_ref, ci, p)) for ci, p in probs]
        hat = [jnp.concatenate([stack(load(bh_ref, ci, p)), stack(load(kh_ref, ci, p))], axis=0)
               for ci, p in probs]
        til = [jnp.concatenate([twice(load(bt_ref, ci, p)), twice(load(kt_ref, ci, p))], axis=0)
               for ci, p in probs]
        lm = [_dot_nt(jnp.concatenate([a, r], axis=0), t) for a, r, t in zip(a_s, r_s, til)]
        l_ab = [x[:C2, :C2] * strict for x in lm]
        l_ak = [cast(x[:C2, C2:] * strict) for x in lm]
        m_cat = [cast(jnp.concatenate([x[C2:, :C2] * incl, x[C2:, C2:] * incl], axis=1))
                 for x in lm]
        t_inv = [eye + l * levels[0] for l in l_ab]
        for lvl in levels[1:]:
            t_b = [cast(t) for t in t_inv]
            z = [_dot(t, cast(l * lvl)) for t, l in zip(t_b, l_ab)]
            t_inv = [t + _dot(cast(zz), tb) for t, zz, tb in zip(t_inv, z, t_b)]
        t_b = [cast(t) for t in t_inv]
        p_loc = [_dot(l, v) for l, v in zip(l_ak, v_s)]
        for ci in range(group):
            rows = pl.ds(offs[ci], C)
            c_idx = g * group + ci
            sel = [ci * n_pairs + p for p in range(n_pairs)]
            st = [s_ref[p] for p in range(n_pairs)]
            st_b = [cast(s) for s in st]
            x = [_dot_nt(a_s[i], sb) + p_loc[i] for i, sb in zip(sel, st_b)]
            u = [cast(_dot(t_b[i], cast(xx))) for i, xx in zip(sel, x)]
            uv = [jnp.concatenate([uu, v_s[i]], axis=0) for i, uu in zip(sel, u)]
            y = [_dot_nt(r_s[i], sb) + _dot(m_cat[i], w) for i, sb, w in zip(sel, st_b, uv)]
            for p in range(n_pairs):
                cols = slice(p * LANES, (p + 1) * LANES)
                w_end = wc_ref[c_idx, 0:1, cols]
                s_ref[p] = st[p] * w_end + _dot_tn(uv[p], hat[sel[p]])
                yp = y[p][:C] + y[p][C:]
                mu = (jnp.sum(yp * m0, axis=-1, keepdims=True) * m0
                      + jnp.sum(yp * m1, axis=-1, keepdims=True) * m1) * (1.0 / RWKV_HEAD)
                yc = yp - mu
                yc2 = yc * yc
                var = (jnp.sum(yc2 * m0, axis=-1, keepdims=True) * m0
                       + jnp.sum(yc2 * m1, axis=-1, keepdims=True) * m1) * (1.0 / RWKV_HEAD)
                yn = yc * lax.rsqrt(var + GN_EPS)
                o_ref[rows, cols] = yn * lnw_ref[:, cols] + lnb_ref[:, cols] + bonus_ref[rows, cols]
        return carry

    lax.fori_loop(0, n_groups, group_body, 0)


def _wkv(at, rt, bt, kt, bh, kh, vb, wc, bonus, ln_w, ln_b, prec=BF16):
    s, d = at.shape
    tt = _row_tile(s, WKV_STEP)
    group = 2
    row = pl.BlockSpec((tt, d), lambda t: (t, 0))
    return pl.pallas_call(
        functools.partial(_wkv_kernel, group=group, prec=prec),
        grid=(s // tt,),
        in_specs=[row] * 7 + [pl.BlockSpec((tt // WKV_CHUNK, SUBLANES, d), lambda t: (t, 0, 0)),
                              row, _full(ln_w.shape), _full(ln_b.shape)],
        out_specs=row,
        out_shape=jax.ShapeDtypeStruct((s, d), F32),
        scratch_shapes=[pltpu.VMEM((d // LANES, LANES, LANES), F32)],
        compiler_params=_params("arbitrary"),
        name="wkv7_scan",
    )(at, rt, bt, kt, bh, kh, vb, wc, bonus, ln_w, ln_b)


def _rot_cols(w):
    half = QK_ROPE_DIM // 2
    return jnp.concatenate([-w[..., half:], w[..., :half]], axis=-1)


def _mla_weights(w_in, w_uq, w_ukv):
    d = w_in.shape[0]
    lat = Q_LORA_RANK + KV_LORA_RANK
    w_kr = w_in[:, lat:]
    w_in_e = jnp.concatenate([w_in[:, :lat], w_kr, _rot_cols(w_kr)], axis=1).astype(BF16)
    uq = w_uq.reshape(Q_LORA_RANK, MLA_HEADS, QK_DIM)
    uq_rope = uq[:, :, QK_NOPE_DIM:]
    w_uq_e = jnp.concatenate(
        [uq[:, :, :QK_NOPE_DIM].reshape(Q_LORA_RANK, -1),
         uq_rope.reshape(Q_LORA_RANK, -1),
         _rot_cols(uq_rope).reshape(Q_LORA_RANK, -1)], axis=1).astype(BF16)
    ukv = w_ukv.reshape(KV_LORA_RANK, MLA_HEADS, QK_NOPE_DIM + V_HEAD_DIM)
    w_ukv_p = jnp.concatenate(
        [ukv[:, :, :QK_NOPE_DIM].reshape(KV_LORA_RANK, -1),
         ukv[:, :, QK_NOPE_DIM:].reshape(KV_LORA_RANK, -1)], axis=1).astype(BF16)
    del d
    return w_in_e, w_uq_e, w_ukv_p


def _mla_layer(x, g, cos_t, sin_t, w_in, q_norm, w_uq, kv_norm, w_ukv):
    w_in_e, w_uq_e, w_ukv_p = _mla_weights(w_in, w_uq, w_ukv)
    q_cat, k_cat, v = _mla_pre(x, g[None], w_in_e, q_norm[None], w_uq_e, kv_norm[None],
                               w_ukv_p, cos_t, sin_t)
    return _attention(q_cat, k_cat, v)


def _rwkv_layer(x, g, v_first, p, wkv_prec=BF16):
    row = lambda t: t.reshape(1, -1)
    pp = {
        "mix": p["mix"], "w_rkv": p["w_rkv"].astype(BF16),
        "w0": row(p["w0"]), "w1": p["w1"].astype(BF16), "w2": p["w2"].astype(BF16),
        "a0": row(p["a0"]), "a1": p["a1"].astype(BF16), "a2": p["a2"].astype(BF16),
        "g1": p["g1"].astype(BF16), "g2": p["g2"].astype(BF16),
        "k_k": row(p["k_k"]), "k_a": row(p["k_a"]), "r_k": row(p["r_k"]),
    }
    if v_first is not None:
        pp.update(v0=row(p["v0"]), v1=p["v1"].astype(BF16), v2=p["v2"].astype(BF16))
    outs = _rwkv_pre(x, g[None], pp, v_first)
    scan_in, gate = outs[:9], outs[9]
    z = _wkv(*scan_in, row(p["ln_w"]), row(p["ln_b"]), prec=wkv_prec)
    return z, gate, (outs[10] if v_first is None else v_first)


def _mix_ffn_layer(x, z, gate, w_o, g, layer, w_in_all, conv_w, conv_b, w_out_all, final_gain=None):
    return _mix_ffn(x, z, gate, w_o.astype(BF16), g[None], layer, w_in_all, conv_w, conv_b[None],
                    w_out_all, None if final_gain is None else final_gain[None])


def kernel(x, positions, norm_mix, norm_ffn, norm_final, mla_w_in, mla_q_norm, mla_w_uq, mla_kv_norm, mla_w_ukv, mla_w_o, rwkv_mix, rwkv_w_rkv, rwkv_w0, rwkv_w1, rwkv_w2, rwkv_a0, rwkv_a1, rwkv_a2, rwkv_v0, rwkv_v1, rwkv_v2, rwkv_g1, rwkv_g2, rwkv_k_k, rwkv_k_a, rwkv_r_k, rwkv_ln_w, rwkv_ln_b, rwkv_w_o, ffn_w_in, ffn_conv_w, ffn_conv_b, ffn_w_out):
    batch, seq, d = x.shape
    assert batch == 1
    depth = norm_mix.shape[0]
    half = QK_ROPE_DIM // 2
    inv_freq = ROPE_THETA ** (-jnp.arange(half, dtype=F32) / half)
    freq_row = jnp.tile(inv_freq, LANES // half)[None]
    cos_t, sin_t = _rope_tables(positions.astype(F32).reshape(seq, 1), freq_row)
    xs = x.reshape(seq, d)
    ffn_w_in_b = ffn_w_in.astype(BF16)
    ffn_w_out_b = ffn_w_out.astype(BF16)
    v_first = None
    for i in range(depth):
        j = i // 2
        if i % 2 == 0:
            z = _mla_layer(xs, norm_mix[i], cos_t, sin_t, mla_w_in[j], mla_q_norm[j], mla_w_uq[j],
                           mla_kv_norm[j], mla_w_ukv[j])
            gate, w_o = None, mla_w_o[j]
        else:
            p = {"mix": rwkv_mix[j], "w_rkv": rwkv_w_rkv[j], "w0": rwkv_w0[j], "w1": rwkv_w1[j],
                 "w2": rwkv_w2[j], "a0": rwkv_a0[j], "a1": rwkv_a1[j], "a2": rwkv_a2[j],
                 "g1": rwkv_g1[j], "g2": rwkv_g2[j], "k_k": rwkv_k_k[j], "k_a": rwkv_k_a[j],
                 "r_k": rwkv_r_k[j], "ln_w": rwkv_ln_w[j], "ln_b": rwkv_ln_b[j]}
            if j > 0:
                p.update(v0=rwkv_v0[j - 1], v1=rwkv_v1[j - 1], v2=rwkv_v2[j - 1])
            z, gate, v_first = _rwkv_layer(xs, norm_mix[i], v_first, p)
            w_o = rwkv_w_o[j]
        xs = _mix_ffn_layer(xs, z, gate, w_o, norm_ffn[i], i, ffn_w_in_b, ffn_conv_w[i], ffn_conv_b[i],
                            ffn_w_out_b, norm_final if i == depth - 1 else None)
    return xs.reshape(batch, seq, d)
```

```python
import functools
import math

import jax
import jax.numpy as jnp
from jax import lax
from jax.experimental import pallas as pl
from jax.experimental.pallas import tpu as pltpu

NORM_EPS = 1e-6
GN_EPS = 64e-5
MLA_HEADS = 8
QK_NOPE_DIM = 128
QK_ROPE_DIM = 64
QK_DIM = QK_NOPE_DIM + QK_ROPE_DIM
QK_PAD = 256
V_HEAD_DIM = 128
V_ROWS = V_HEAD_DIM + 16
Q_LORA_RANK = 384
KV_LORA_RANK = 256
ROPE_THETA = 10000.0
RWKV_HEAD = 64
LANES = 128
D_FF = 2816
HALO = 16
WKV_CHUNK = 64
WKV_TILE = 256
WKV_STEP = 512
SUBLANES = 8
ATTN_TK = 512
ATTN_CHAINS = 8

BF16 = jnp.bfloat16
F32 = jnp.float32
VMEM_LIMIT = 56 * 1024 * 1024


def _params(*sem):
    return pltpu.CompilerParams(dimension_semantics=sem, vmem_limit_bytes=VMEM_LIMIT)


def _dot(a, b):
    return jnp.dot(a, b, preferred_element_type=F32)


def _dot_nt(a, b):
    return lax.dot_general(a, b, (((1,), (1,)), ((), ())), preferred_element_type=F32)


def _dot_tn(a, b):
    return lax.dot_general(a, b, (((0,), (0,)), ((), ())), preferred_element_type=F32)


def _rms(x, g):
    return x * lax.rsqrt(jnp.mean(x * x, axis=-1, keepdims=True) + NORM_EPS) * g


def _row_tile(s, want):
    return min(want, s)


def _full(shape):
    return pl.BlockSpec(shape, lambda *_: (0,) * len(shape))


def _rope_table_kernel(pos_ref, freq_ref, cos_ref, sin_ref):
    ang = pos_ref[...] * freq_ref[...]
    cos_ref[...] = jnp.cos(ang)
    sin_ref[...] = jnp.sin(ang)


def _rope_tables(pos_col, freq_row):
    s = pos_col.shape[0]
    tm = _row_tile(s, 1024)
    return pl.pallas_call(
        _rope_table_kernel,
        grid=(s // tm,),
        in_specs=[pl.BlockSpec((tm, 1), lambda i: (i, 0)), _full((1, LANES))],
        out_specs=[pl.BlockSpec((tm, LANES), lambda i: (i, 0))] * 2,
        out_shape=[jax.ShapeDtypeStruct((s, LANES), F32)] * 2,
        compiler_params=_params("parallel"),
        name="rope_tables",
    )(pos_col, freq_row)


def _mla_pre_kernel(x_ref, g_ref, w_in_ref, qn_ref, w_uq_ref, kvn_ref, w_ukv_ref,
                    cos_ref, sin_ref, q_ref, k_ref, v_ref):
    h = _rms(x_ref[...], g_ref[...]).astype(BF16)
    lat = _dot(h, w_in_ref[...])
    q_lat = lat[:, :Q_LORA_RANK]
    c_kv = lat[:, Q_LORA_RANK:Q_LORA_RANK + KV_LORA_RANK]
    kr = lat[:, Q_LORA_RANK + KV_LORA_RANK:]
    cos = cos_ref[...]
    sin = sin_ref[...]
    k_rope = (kr[:, :QK_ROPE_DIM] * cos[:, :QK_ROPE_DIM]
              + kr[:, QK_ROPE_DIM:] * sin[:, :QK_ROPE_DIM])
    qn = _rms(q_lat, qn_ref[...]).astype(BF16)
    q_all = _dot(qn, w_uq_ref[...])
    n_nope = MLA_HEADS * QK_NOPE_DIM
    n_rope = MLA_HEADS * QK_ROPE_DIM
    reps = n_rope // LANES
    qscale = QK_DIM ** -0.5 * math.log2(math.e)
    q_rope = (q_all[:, n_nope:n_nope + n_rope] * jnp.tile(cos, (1, reps))
              + q_all[:, n_nope + n_rope:] * jnp.tile(sin, (1, reps))) * qscale
    kvn = _rms(c_kv, kvn_ref[...]).astype(BF16)
    kv = _dot(kvn, w_ukv_ref[...])
    tm = x_ref.shape[0]
    qn_t = (q_all[:, :n_nope] * qscale).T.astype(BF16)
    qr_t = q_rope.T.astype(BF16)
    v_t = kv[:, n_nope:].T.astype(BF16)
    zpad_t = jnp.zeros((QK_PAD - QK_DIM, tm), BF16)
    ones_t = jnp.ones((V_ROWS - V_HEAD_DIM, tm), BF16)
    zpad = jnp.zeros((tm, QK_PAD - QK_DIM), BF16)
    k_rope_b = k_rope.astype(BF16)
    for hd in range(MLA_HEADS):
        q_ref[hd, 0:QK_NOPE_DIM, :] = qn_t[hd * QK_NOPE_DIM:(hd + 1) * QK_NOPE_DIM]
        q_ref[hd, QK_NOPE_DIM:QK_DIM, :] = qr_t[hd * QK_ROPE_DIM:(hd + 1) * QK_ROPE_DIM]
        q_ref[hd, QK_DIM:QK_PAD, :] = zpad_t
        k_ref[hd, :, 0:QK_NOPE_DIM] = kv[:, hd * QK_NOPE_DIM:(hd + 1) * QK_NOPE_DIM].astype(BF16)
        k_ref[hd, :, QK_NOPE_DIM:QK_DIM] = k_rope_b
        k_ref[hd, :, QK_DIM:QK_PAD] = zpad
        v_ref[hd, 0, 0:V_HEAD_DIM, :] = v_t[hd * V_HEAD_DIM:(hd + 1) * V_HEAD_DIM]
        v_ref[hd, 0, V_HEAD_DIM:V_ROWS, :] = ones_t


def _mla_pre(x, g, w_in_e, q_norm, w_uq_e, kv_norm, w_ukv_p, cos_t, sin_t):
    s, d = x.shape
    tm = _row_tile(s, ATTN_TK)
    row = lambda w: pl.BlockSpec((tm, w), lambda i: (i, 0))
    return pl.pallas_call(
        _mla_pre_kernel,
        grid=(s // tm,),
        in_specs=[row(d), _full(g.shape), _full(w_in_e.shape), _full(q_norm.shape),
                  _full(w_uq_e.shape), _full(kv_norm.shape), _full(w_ukv_p.shape),
                  row(LANES), row(LANES)],
        out_specs=[pl.BlockSpec((MLA_HEADS, QK_PAD, tm), lambda i: (0, 0, i)),
                   pl.BlockSpec((MLA_HEADS, tm, QK_PAD), lambda i: (0, i, 0)),
                   pl.BlockSpec((MLA_HEADS, 1, V_ROWS, tm), lambda i: (0, i, 0, 0))],
        out_shape=[jax.ShapeDtypeStruct((MLA_HEADS, QK_PAD, s), BF16),
                   jax.ShapeDtypeStruct((MLA_HEADS, s, QK_PAD), BF16),
                   jax.ShapeDtypeStruct((MLA_HEADS, s // tm, V_ROWS, tm), BF16)],
        compiler_params=_params("parallel"),
        name="mla_pre",
    )(x, g, w_in_e, q_norm, w_uq_e, kv_norm, w_ukv_p, cos_t, sin_t)


def _attn_kernel(q_ref, k_ref, v_ref, o_ref, *scratch, tk, n_chains):
    qi = pl.program_id(1)
    accs = scratch[:n_chains]
    sbuf = [scratch[n_chains + 2 * c:n_chains + 2 * c + 2] for c in range(n_chains)]
    for acc in accs:
        acc[...] = jnp.zeros(acc.shape, F32)

    def scores(task):
        c, j, par, _ = task
        q_t = q_ref[:, c * tk:(c + 1) * tk]
        off = pl.multiple_of(j * tk, tk)
        s = _dot(k_ref[pl.ds(off, tk), :], q_t)
        sbuf[c][par][...] = s
        return jnp.max(s, axis=0, keepdims=True)

    def absorb(task, m_blk, m_old):
        c, j, par, masked = task
        s = sbuf[c][par][...]
        if masked:
            key = lax.broadcasted_iota(jnp.int32, s.shape, 0)
            qry = lax.broadcasted_iota(jnp.int32, s.shape, 1)
            s = jnp.where(key <= qry, s, jnp.finfo(F32).min)
            m_blk = jnp.max(s, axis=0, keepdims=True)
        m_new = jnp.maximum(m_old, m_blk)
        p = jnp.exp2(s - m_new)
        alpha = jnp.exp2(m_old - m_new)
        accs[c][...] = alpha * accs[c][...] + _dot(v_ref[j], p.astype(BF16))
        return m_new

    def run(tasks, lookahead, ms, pending):
        ms = list(ms)
        for i, task in enumerate(tasks):
            nxt = tasks[i + 1] if i + 1 < len(tasks) else lookahead
            staged = scores(nxt) if nxt is not None else None
            ms[task[0]] = absorb(task, pending, ms[task[0]])
            pending = staged
        return ms, pending

    chains = range(n_chains)
    n_full = n_chains * qi

    def body(jj, carry):
        j = 2 * jj
        tasks = [(c, j, 0, False) for c in chains] + [(c, j + 1, 1, False) for c in chains]
        ms, pending = run(tasks, (0, j + 2, 0, False), carry[:-1], carry[-1])
        return (*ms, pending)

    m0 = jnp.full((1, tk), -1e30, F32)
    carry = lax.fori_loop(0, n_full // 2, body, (*[m0] * n_chains, scores((0, 0, 0, False))))
    tail = [(c, n_full + kb, kb % 2, c == kb) for kb in chains for c in range(kb, n_chains)]
    run(tail, None, carry[:-1], carry[-1])
    for c, acc in enumerate(accs):
        out = acc[0:V_HEAD_DIM, :] / acc[V_HEAD_DIM:V_HEAD_DIM + 1, :]
        o_ref[c * tk:(c + 1) * tk, :] = out.T.astype(o_ref.dtype)


def _attention(q_t, k_cat, v_t):
    nh, s, _ = k_cat.shape
    tk = v_t.shape[-1]
    n_chains = min(ATTN_CHAINS, s // tk)
    assert n_chains % 2 == 0
    tq = n_chains * tk
    return pl.pallas_call(
        functools.partial(_attn_kernel, tk=tk, n_chains=n_chains),
        grid=(nh, s // tq),
        in_specs=[pl.BlockSpec((None, QK_PAD, tq), lambda h, i: (h, 0, i)),
                  pl.BlockSpec((None, s, QK_PAD), lambda h, i: (h, 0, 0)),
                  pl.BlockSpec((None, s // tk, V_ROWS, tk), lambda h, i: (h, 0, 0, 0))],
        out_specs=pl.BlockSpec((tq, V_HEAD_DIM), lambda h, i: (i, h)),
        out_shape=jax.ShapeDtypeStruct((s, nh * V_HEAD_DIM), BF16),
        scratch_shapes=([pltpu.VMEM((V_ROWS, tk), F32)] * n_chains
                        + [pltpu.VMEM((tk, tk), F32)] * (2 * n_chains)),
        compiler_params=_params("parallel", "arbitrary"),
        name="mla_attention",
    )(q_t, k_cat, v_t)


FFN_CHUNKS = ((0, 1024), (1024, 2048), (2048, D_FF))


def _mix_ffn_kernel(*refs, gated, final):
    refs = list(refs)
    o_ref = refs.pop()
    fg_ref = refs.pop() if final else None
    x_ref, xp_ref, z_ref, zp_ref = refs[:4]
    gt_ref, gtp_ref = refs[4:6] if gated else (None, None)
    wo_ref, g_ref, wi_ref, cw_ref, cb_ref, wout_ref = refs[-6:]
    i = pl.program_id(0)

    def mixed(xr, zr, gr):
        z = zr[...]
        if gated:
            z = z * gr[...]
        return xr[...] + _dot(z.astype(BF16), wo_ref[...])

    x1 = mixed(x_ref, z_ref, gt_ref)
    x1p = mixed(xp_ref, zp_ref, gtp_ref)
    keep = (i > 0).astype(F32)
    hn = jnp.concatenate([(_rms(x1p, g_ref[...]) * keep).astype(BF16),
                          _rms(x1, g_ref[...]).astype(BF16)], axis=0)
    def gate_up(lo, hi):
        return _dot(hn, wi_ref[:, lo:hi]), _dot(hn[HALO:], wi_ref[:, D_FF + lo:D_FF + hi])

    acc = x1
    nxt = gate_up(*FFN_CHUNKS[0])
    for n, (lo, hi) in enumerate(FFN_CHUNKS):
        gate, up = nxt
        if n + 1 < len(FFN_CHUNKS):
            nxt = gate_up(*FFN_CHUNKS[n + 1])
        conv = (gate[HALO:] * cw_ref[2:3, lo:hi]
                + pltpu.roll(gate, 1, 0)[HALO:] * cw_ref[1:2, lo:hi]
                + pltpu.roll(gate, 2, 0)[HALO:] * cw_ref[0:1, lo:hi] + cb_ref[:, lo:hi])
        act = (conv * jax.nn.sigmoid(conv) * up).astype(BF16)
        acc = acc + _dot(act, wout_ref[lo:hi, :])
    o_ref[...] = _rms(acc, fg_ref[...]) if final else acc


def _mix_ffn(x, z, gate, w_o, g, layer, w_in_all, conv_w, conv_b, w_out_all, final_gain=None):
    s, d = x.shape
    tm = _row_tile(s, 512)
    row = pl.BlockSpec((tm, d), lambda i: (i, 0))
    prev = pl.BlockSpec((HALO, d), lambda i: (jnp.maximum(i * (tm // HALO) - 1, 0), 0))

    def const(a, stacked=False):
        if stacked:
            return pl.BlockSpec((None,) + a.shape[1:], lambda i: (layer,) + (0,) * (a.ndim - 1),
                                pipeline_mode=pl.Buffered(1))
        return pl.BlockSpec(a.shape, lambda i: (0,) * a.ndim, pipeline_mode=pl.Buffered(1))

    gated = gate is not None
    final = final_gain is not None
    consts = [w_o, g, w_in_all, conv_w, conv_b, w_out_all] + ([final_gain] if final else [])
    args = [x, x, z, z] + ([gate, gate] if gated else []) + consts
    specs = ([row, prev, row, prev] + ([row, prev] if gated else [])
             + [const(a, stacked=a is w_in_all or a is w_out_all) for a in consts])
    return pl.pallas_call(
        functools.partial(_mix_ffn_kernel, gated=gated, final=final),
        grid=(s // tm,),
        in_specs=specs,
        out_specs=row,
        out_shape=jax.ShapeDtypeStruct((s, d), F32),
        compiler_params=_params("parallel"),
        name="mix_ffn",
    )(*args)


def _group_sum(x):
    w = 2 * LANES
    r = lax.broadcasted_iota(jnp.int32, (w, w), 0) // RWKV_HEAD
    c = lax.broadcasted_iota(jnp.int32, (w, w), 1) // RWKV_HEAD
    ones_bd = (r == c).astype(BF16)
    hi = x.astype(BF16)
    lo = (x - hi.astype(F32)).astype(BF16)
    cols = [_dot(hi[:, j:j + w], ones_bd) + _dot(lo[:, j:j + w], ones_bd)
            for j in range(0, x.shape[1], w)]
    return cols[0] if len(cols) == 1 else jnp.concatenate(cols, axis=1)


def _rwkv_pre_kernel(*refs, has_vres):
    (x_ref, xp_ref, g_ref, mix_ref, wrkv_ref, w0_ref, w1_ref, w2_ref, a0_ref, a1_ref, a2_ref,
     g1_ref, g2_ref, kk_ref, ka_ref, rk_ref) = refs[:16]
    if has_vres:
        vf_ref, v0_ref, v1_ref, v2_ref = refs[16:20]
        at_o, rt_o, bt_o, kt_o, bh_o, kh_o, vb_o, wc_o, bonus_o, g_o = refs[20:]
    else:
        at_o, rt_o, bt_o, kt_o, bh_o, kh_o, vb_o, wc_o, bonus_o, g_o, v_o = refs[16:]
    i = pl.program_id(0)
    tm = x_ref.shape[0]
    h = _rms(x_ref[...], g_ref[...])
    hp = _rms(xp_ref[...], g_ref[...])[HALO - 1:HALO] * (i > 0).astype(F32)
    row = lax.broadcasted_iota(jnp.int32, h.shape, 0)
    shifted = jnp.where(row == 0, jnp.broadcast_to(hp, h.shape), pltpu.roll(h, 1, 0))
    xx = shifted - h
    mix = mix_ref[...]
    xm = lambda n: (h + xx * mix[n:n + 1]).astype(BF16)
    xv = xm(2)
    r = _dot(xm(0), wrkv_ref[0])
    k = _dot(xm(1), wrkv_ref[1])
    v = _dot(xv, wrkv_ref[2])
    lora_w = _dot(jnp.tanh(_dot(xm(3), w1_ref[...])).astype(BF16), w2_ref[...])
    lw = -math.exp(-0.5) * jax.nn.sigmoid(w0_ref[...] + lora_w)
    a = jax.nn.sigmoid(a0_ref[...] + _dot(_dot(xm(4), a1_ref[...]).astype(BF16), a2_ref[...]))
    g_o[...] = _dot(jax.nn.sigmoid(_dot(xm(5), g1_ref[...])).astype(BF16), g2_ref[...])
    if has_vres:
        gate_v = jax.nn.sigmoid(v0_ref[...] + _dot(_dot(xv, v1_ref[...]).astype(BF16), v2_ref[...]))
        v = v + (vf_ref[...] - v) * gate_v
    else:
        v_o[...] = v
    kk = k * kk_ref[...]
    kk = kk * lax.rsqrt(jnp.maximum(_group_sum(kk * kk), 1e-24))
    k = k * (1.0 + (a - 1.0) * ka_ref[...])
    bonus_o[...] = _group_sum(r * k * rk_ref[...]) * v
    C = WKV_CHUNK
    rr = lax.broadcasted_iota(jnp.int32, (tm, tm), 0)
    cc = lax.broadcasted_iota(jnp.int32, (tm, tm), 1)
    tri = (((rr // C) == (cc // C)) & (cc <= rr)).astype(BF16)
    hi = lw.astype(BF16)
    rem = lw - hi.astype(F32)
    mid = rem.astype(BF16)
    lo = (rem - mid.astype(F32)).astype(BF16)
    cl = _dot(tri, hi) + _dot(tri, mid) + _dot(tri, lo)
    ends = [cl[c * C + C - 1:c * C + C] for c in range(tm // C)]
    cl_end = jnp.concatenate([jnp.broadcast_to(e, (C, e.shape[1])) for e in ends], axis=0)
    e_neg = jnp.exp(-cl)
    e_rem = jnp.exp(cl_end - cl)
    at_o[...] = (-kk * jnp.exp(cl - lw)).astype(BF16)
    rt_o[...] = (r * jnp.exp(cl)).astype(BF16)
    bt_o[...] = (kk * a * e_neg).astype(BF16)
    kt_o[...] = (k * e_neg).astype(BF16)
    bh_o[...] = (kk * a * e_rem).astype(BF16)
    kh_o[...] = (k * e_rem).astype(BF16)
    vb_o[...] = v.astype(BF16)
    for c, e in enumerate(ends):
        wc_o[c] = jnp.broadcast_to(jnp.exp(e), wc_o.shape[1:])


def _rwkv_pre(x, g, p, v_first):
    s, d = x.shape
    tm = _row_tile(s, WKV_TILE)
    row = pl.BlockSpec((tm, d), lambda i: (i, 0))
    prev = pl.BlockSpec((HALO, d), lambda i: (jnp.maximum(i * (tm // HALO) - 1, 0), 0))
    names = ["mix", "w_rkv", "w0", "w1", "w2", "a0", "a1", "a2", "g1", "g2", "k_k", "k_a", "r_k"]
    args = [x, x, g] + [p[n] for n in names]
    specs = [row, prev, _full(g.shape)] + [_full(p[n].shape) for n in names]
    has_vres = v_first is not None
    if has_vres:
        args += [v_first, p["v0"], p["v1"], p["v2"]]
        specs += [row] + [_full(p[n].shape) for n in ("v0", "v1", "v2")]
    n_ch = tm // WKV_CHUNK
    wc_spec = pl.BlockSpec((n_ch, SUBLANES, d), lambda i: (i, 0, 0))
    out_specs = [row] * 7 + [wc_spec, row, row]
    out_shape = ([jax.ShapeDtypeStruct((s, d), BF16)] * 7
                 + [jax.ShapeDtypeStruct((s // WKV_CHUNK, SUBLANES, d), F32)]
                 + [jax.ShapeDtypeStruct((s, d), F32)] * 2)
    if not has_vres:
        out_specs.append(row)
        out_shape.append(jax.ShapeDtypeStruct((s, d), F32))
    return pl.pallas_call(
        functools.partial(_rwkv_pre_kernel, has_vres=has_vres),
        grid=(s // tm,),
        in_specs=specs,
        out_specs=out_specs,
        out_shape=out_shape,
        compiler_params=_params("parallel"),
        name="rwkv_pre",
    )(*args)


def _wkv_kernel(at_ref, rt_ref, bt_ref, kt_ref, bh_ref, kh_ref, v_ref, wc_ref, bonus_ref,
                lnw_ref, lnb_ref, o_ref, s_ref, *, group):
    C = WKV_CHUNK
    C2 = 2 * C
    n_pairs = o_ref.shape[1] // LANES
    n_groups = o_ref.shape[0] // (C * group)

    @pl.when(pl.program_id(0) == 0)
    def _():
        s_ref[...] = jnp.zeros(s_ref.shape, F32)

    lane = lax.broadcasted_iota(jnp.int32, (1, LANES), 1)
    head0 = lane < RWKV_HEAD
    m0 = head0.astype(F32)
    m1 = 1.0 - m0
    t2 = lax.broadcasted_iota(jnp.int32, (C2, C2), 0)
    s2 = lax.broadcasted_iota(jnp.int32, (C2, C2), 1)
    same_head = (t2 // C) == (s2 // C)
    strict = (same_head & (s2 < t2)).astype(F32)
    incl = (same_head & (s2 <= t2)).astype(F32)
    eye = (t2 == s2).astype(F32)

    def level_mask(b):
        return (((t2 // (2 * b)) == (s2 // (2 * b))) & ((t2 % (2 * b)) >= b)
                & ((s2 % (2 * b)) < b)).astype(F32)

    levels = []
    b_sz = 1
    while b_sz < C:
        levels.append(level_mask(b_sz))
        b_sz *= 2

    def stack(x):
        zero = jnp.zeros_like(x)
        return jnp.concatenate([jnp.where(head0, x, zero), jnp.where(head0, zero, x)], axis=0)

    def twice(x):
        return jnp.concatenate([x, x], axis=0)

    def cast(x):
        return x.astype(BF16)

    def group_body(g, carry):
        probs = [(ci, p) for ci in range(group) for p in range(n_pairs)]
        offs = [pl.multiple_of((g * group + ci) * C, C) for ci in range(group)]

        def load(ref, ci, p):
            return cast(ref[pl.ds(offs[ci], C), p * LANES:(p + 1) * LANES])

        a_s = [stack(load(at_ref, ci, p)) for ci, p in probs]
        r_s = [stack(load(rt_ref, ci, p)) for ci, p in probs]
        v_s = [stack(load(v_ref, ci, p)) for ci, p in probs]
        hat = [jnp.concatenate([stack(load(bh_ref, ci, p)), stack(load(kh_ref, ci, p))], axis=0)
               for ci, p in probs]
        til = [jnp.concatenate([twice(load(bt_ref, ci, p)), twice(load(kt_ref, ci, p))], axis=0)
               for ci, p in probs]
        lm = [_dot_nt(jnp.concatenate([a, r], axis=0), t) for a, r, t in zip(a_s, r_s, til)]
        l_ab = [x[:C2, :C2] * strict for x in lm]
        l_ak = [cast(x[:C2, C2:] * strict) for x in lm]
        m_cat = [cast(jnp.concatenate([x[C2:, :C2] * incl, x[C2:, C2:] * incl], axis=1))
                 for x in lm]
        t_inv = [eye + l * levels[0] for l in l_ab]
        for lvl in levels[1:]:
            t_b = [cast(t) for t in t_inv]
            z = [_dot(t, cast(l * lvl)) for t, l in zip(t_b, l_ab)]
            t_inv = [t + _dot(cast(zz), tb) for t, zz, tb in zip(t_inv, z, t_b)]
        t_b = [cast(t) for t in t_inv]
        p_loc = [_dot(l, v) for l, v in zip(l_ak, v_s)]
        for ci in range(group):
            rows = pl.ds(offs[ci], C)
            c_idx = g * group + ci
            sel = [ci * n_pairs + p for p in range(n_pairs)]
            st = [s_ref[p] for p in range(n_pairs)]
            st_b = [cast(s) for s in st]
            x = [_dot_nt(a_s[i], sb) + p_loc[i] for i, sb in zip(sel, st_b)]
            u = [cast(_dot(t_b[i], cast(xx))) for i, xx in zip(sel, x)]
            uv = [jnp.concatenate([uu, v_s[i]], axis=0) for i, uu in zip(sel, u)]
            y = [_dot_nt(r_s[i], sb) + _dot(m_cat[i], w) for i, sb, w in zip(sel, st_b, uv)]
            for p in range(n_pairs):
                cols = slice(p * LANES, (p + 1) * LANES)
                w_end = wc_ref[c_idx, 0:1, cols]
                s_ref[p] = st[p] * w_end + _dot_tn(uv[p], hat[sel[p]])
                yp = y[p][:C] + y[p][C:]
                mu = (jnp.sum(yp * m0, axis=-1, keepdims=True) * m0
                      + jnp.sum(yp * m1, axis=-1, keepdims=True) * m1) * (1.0 / RWKV_HEAD)
                yc = yp - mu
                yc2 = yc * yc
                var = (jnp.sum(yc2 * m0, axis=-1, keepdims=True) * m0
                       + jnp.sum(yc2 * m1, axis=-1, keepdims=True) * m1) * (1.0 / RWKV_HEAD)
                yn = yc * lax.rsqrt(var + GN_EPS)
                o_ref[rows, cols] = yn * lnw_ref[:, cols] + lnb_ref[:, cols] + bonus_ref[rows, cols]
        return carry

    lax.fori_loop(0, n_groups, group_body, 0)


def _wkv(at, rt, bt, kt, bh, kh, vb, wc, bonus, ln_w, ln_b):
    s, d = at.shape
    tt = _row_tile(s, WKV_STEP)
    group = 2
    row = pl.BlockSpec((tt, d), lambda t: (t, 0))
    return pl.pallas_call(
        functools.partial(_wkv_kernel, group=group),
        grid=(s // tt,),
        in_specs=[row] * 7 + [pl.BlockSpec((tt // WKV_CHUNK, SUBLANES, d), lambda t: (t, 0, 0)),
                              row, _full(ln_w.shape), _full(ln_b.shape)],
        out_specs=row,
        out_shape=jax.ShapeDtypeStruct((s, d), F32),
        scratch_shapes=[pltpu.VMEM((d // LANES, LANES, LANES), F32)],
        compiler_params=_params("arbitrary"),
        name="wkv7_scan",
    )(at, rt, bt, kt, bh, kh, vb, wc, bonus, ln_w, ln_b)


def _rot_cols(w):
    half = QK_ROPE_DIM // 2
    return jnp.concatenate([-w[..., half:], w[..., :half]], axis=-1)


def _mla_weights(w_in, w_uq, w_ukv):
    d = w_in.shape[0]
    lat = Q_LORA_RANK + KV_LORA_RANK
    w_kr = w_in[:, lat:]
    w_in_e = jnp.concatenate([w_in[:, :lat], w_kr, _rot_cols(w_kr)], axis=1).astype(BF16)
    uq = w_uq.reshape(Q_LORA_RANK, MLA_HEADS, QK_DIM)
    uq_rope = uq[:, :, QK_NOPE_DIM:]
    w_uq_e = jnp.concatenate(
        [uq[:, :, :QK_NOPE_DIM].reshape(Q_LORA_RANK, -1),
         uq_rope.reshape(Q_LORA_RANK, -1),
         _rot_cols(uq_rope).reshape(Q_LORA_RANK, -1)], axis=1).astype(BF16)
    ukv = w_ukv.reshape(KV_LORA_RANK, MLA_HEADS, QK_NOPE_DIM + V_HEAD_DIM)
    w_ukv_p = jnp.concatenate(
        [ukv[:, :, :QK_NOPE_DIM].reshape(KV_LORA_RANK, -1),
         ukv[:, :, QK_NOPE_DIM:].reshape(KV_LORA_RANK, -1)], axis=1).astype(BF16)
    del d
    return w_in_e, w_uq_e, w_ukv_p


def _mla_layer(x, g, cos_t, sin_t, w_in, q_norm, w_uq, kv_norm, w_ukv):
    w_in_e, w_uq_e, w_ukv_p = _mla_weights(w_in, w_uq, w_ukv)
    q_cat, k_cat, v = _mla_pre(x, g[None], w_in_e, q_norm[None], w_uq_e, kv_norm[None],
                               w_ukv_p, cos_t, sin_t)
    return _attention(q_cat, k_cat, v)


def _rwkv_layer(x, g, v_first, p):
    row = lambda t: t.reshape(1, -1)
    pp = {
        "mix": p["mix"], "w_rkv": p["w_rkv"].astype(BF16),
        "w0": row(p["w0"]), "w1": p["w1"].astype(BF16), "w2": p["w2"].astype(BF16),
        "a0": row(p["a0"]), "a1": p["a1"].astype(BF16), "a2": p["a2"].astype(BF16),
        "g1": p["g1"].astype(BF16), "g2": p["g2"].astype(BF16),
        "k_k": row(p["k_k"]), "k_a": row(p["k_a"]), "r_k": row(p["r_k"]),
    }
    if v_first is not None:
        pp.update(v0=row(p["v0"]), v1=p["v1"].astype(BF16), v2=p["v2"].astype(BF16))
    outs = _rwkv_pre(x, g[None], pp, v_first)
    scan_in, gate = outs[:9], outs[9]
    z = _wkv(*scan_in, row(p["ln_w"]), row(p["ln_b"]))
    return z, gate, (outs[10] if v_first is None else v_first)


def _mix_ffn_layer(x, z, gate, w_o, g, layer, w_in_all, conv_w, conv_b, w_out_all, final_gain=None):
    return _mix_ffn(x, z, gate, w_o.astype(BF16), g[None], layer, w_in_all, conv_w, conv_b[None],
                    w_out_all, None if final_gain is None else final_gain[None])


def kernel(x, positions, norm_mix, norm_ffn, norm_final, mla_w_in, mla_q_norm, mla_w_uq, mla_kv_norm, mla_w_ukv, mla_w_o, rwkv_mix, rwkv_w_rkv, rwkv_w0, rwkv_w1, rwkv_w2, rwkv_a0, rwkv_a1, rwkv_a2, rwkv_v0, rwkv_v1, rwkv_v2, rwkv_g1, rwkv_g2, rwkv_k_k, rwkv_k_a, rwkv_r_k, rwkv_ln_w, rwkv_ln_b, rwkv_w_o, ffn_w_in, ffn_conv_w, ffn_conv_b, ffn_w_out):
    batch, seq, d = x.shape
    assert batch == 1
    depth = norm_mix.shape[0]
    half = QK_ROPE_DIM // 2
    inv_freq = ROPE_THETA ** (-jnp.arange(half, dtype=F32) / half)
    freq_row = jnp.tile(inv_freq, LANES // half)[None]
    cos_t, sin_t = _rope_tables(positions.astype(F32).reshape(seq, 1), freq_row)
    xs = x.reshape(seq, d)
    ffn_w_in_b = ffn_w_in.astype(BF16)
    ffn_w_out_b = ffn_w_out.astype(BF16)
    v_first = None
    for i in range(depth):
        j = i // 2
        if i % 2 == 0:
            z = _mla_layer(xs, norm_mix[i], cos_t, sin_t, mla_w_in[j], mla_q_norm[j], mla_w_uq[j],
                           mla_kv_norm[j], mla_w_ukv[j])
            gate, w_o = None, mla_w_o[j]
        else:
            p = {"mix": rwkv_mix[j], "w_rkv": rwkv_w_rkv[j], "w0": rwkv_w0[j], "w1": rwkv_w1[j],
                 "w2": rwkv_w2[j], "a0": rwkv_a0[j], "a1": rwkv_a1[j], "a2": rwkv_a2[j],
                 "g1": rwkv_g1[j], "g2": rwkv_g2[j], "k_k": rwkv_k_k[j], "k_a": rwkv_k_a[j],
                 "r_k": rwkv_r_k[j], "ln_w": rwkv_ln_w[j], "ln_b": rwkv_ln_b[j]}
            if j > 0:
                p.update(v0=rwkv_v0[j - 1], v1=rwkv_v1[j - 1], v2=rwkv_v2[j - 1])
            z, gate, v_first = _rwkv_layer(xs, norm_mix[i], v_first, p)
            w_o = rwkv_w_o[j]
        xs = _mix_ffn_layer(xs, z, gate, w_o, norm_ffn[i], i, ffn_w_in_b, ffn_conv_w[i], ffn_conv_b[i],
                            ffn_w_out_b, norm_final if i == depth - 1 else None)
    return xs.reshape(batch, seq, d)
```

```python
import functools
import math

import jax
import jax.numpy as jnp
from jax import lax
from jax.experimental import pallas as pl
from jax.experimental.pallas import tpu as pltpu

NORM_EPS = 1e-6
GN_EPS = 64e-5
MLA_HEADS = 8
QK_NOPE_DIM = 128
QK_ROPE_DIM = 64
QK_DIM = QK_NOPE_DIM + QK_ROPE_DIM
QK_PAD = 256
V_HEAD_DIM = 128
V_ROWS = V_HEAD_DIM + 16
Q_LORA_RANK = 384
KV_LORA_RANK = 256
ROPE_THETA = 10000.0
RWKV_HEAD = 64
LANES = 128
D_FF = 2816
HALO = 16
WKV_CHUNK = 64
WKV_TILE = 256
WKV_STEP = 512
ATTN_TK = 512
ATTN_CHAINS = 8

BF16 = jnp.bfloat16
F32 = jnp.float32
VMEM_LIMIT = 56 * 1024 * 1024


def _params(*sem):
    return pltpu.CompilerParams(dimension_semantics=sem, vmem_limit_bytes=VMEM_LIMIT)


def _dot(a, b):
    return jnp.dot(a, b, preferred_element_type=F32)


def _dot_nt(a, b):
    return lax.dot_general(a, b, (((1,), (1,)), ((), ())), preferred_element_type=F32)


def _dot_tn(a, b):
    return lax.dot_general(a, b, (((0,), (0,)), ((), ())), preferred_element_type=F32)


def _rms(x, g):
    return x * lax.rsqrt(jnp.mean(x * x, axis=-1, keepdims=True) + NORM_EPS) * g


def _row_tile(s, want):
    return min(want, s)


def _full(shape):
    return pl.BlockSpec(shape, lambda *_: (0,) * len(shape))


def _rope_table_kernel(pos_ref, freq_ref, cos_ref, sin_ref):
    ang = pos_ref[...] * freq_ref[...]
    cos_ref[...] = jnp.cos(ang)
    sin_ref[...] = jnp.sin(ang)


def _rope_tables(pos_col, freq_row):
    s = pos_col.shape[0]
    tm = _row_tile(s, 1024)
    return pl.pallas_call(
        _rope_table_kernel,
        grid=(s // tm,),
        in_specs=[pl.BlockSpec((tm, 1), lambda i: (i, 0)), _full((1, LANES))],
        out_specs=[pl.BlockSpec((tm, LANES), lambda i: (i, 0))] * 2,
        out_shape=[jax.ShapeDtypeStruct((s, LANES), F32)] * 2,
        compiler_params=_params("parallel"),
        name="rope_tables",
    )(pos_col, freq_row)


def _mla_pre_kernel(x_ref, g_ref, w_in_ref, qn_ref, w_uq_ref, kvn_ref, w_ukv_ref,
                    cos_ref, sin_ref, q_ref, k_ref, v_ref):
    h = _rms(x_ref[...], g_ref[...]).astype(BF16)
    lat = _dot(h, w_in_ref[...])
    q_lat = lat[:, :Q_LORA_RANK]
    c_kv = lat[:, Q_LORA_RANK:Q_LORA_RANK + KV_LORA_RANK]
    kr = lat[:, Q_LORA_RANK + KV_LORA_RANK:]
    cos = cos_ref[...]
    sin = sin_ref[...]
    k_rope = (kr[:, :QK_ROPE_DIM] * cos[:, :QK_ROPE_DIM]
              + kr[:, QK_ROPE_DIM:] * sin[:, :QK_ROPE_DIM])
    qn = _rms(q_lat, qn_ref[...]).astype(BF16)
    q_all = _dot(qn, w_uq_ref[...])
    n_nope = MLA_HEADS * QK_NOPE_DIM
    n_rope = MLA_HEADS * QK_ROPE_DIM
    reps = n_rope // LANES
    qscale = QK_DIM ** -0.5 * math.log2(math.e)
    q_rope = (q_all[:, n_nope:n_nope + n_rope] * jnp.tile(cos, (1, reps))
              + q_all[:, n_nope + n_rope:] * jnp.tile(sin, (1, reps))) * qscale
    kvn = _rms(c_kv, kvn_ref[...]).astype(BF16)
    kv = _dot(kvn, w_ukv_ref[...])
    tm = x_ref.shape[0]
    qn_t = (q_all[:, :n_nope] * qscale).T.astype(BF16)
    qr_t = q_rope.T.astype(BF16)
    v_t = kv[:, n_nope:].T.astype(BF16)
    zpad_t = jnp.zeros((QK_PAD - QK_DIM, tm), BF16)
    ones_t = jnp.ones((V_ROWS - V_HEAD_DIM, tm), BF16)
    zpad = jnp.zeros((tm, QK_PAD - QK_DIM), BF16)
    k_rope_b = k_rope.astype(BF16)
    for hd in range(MLA_HEADS):
        q_ref[hd, 0:QK_NOPE_DIM, :] = qn_t[hd * QK_NOPE_DIM:(hd + 1) * QK_NOPE_DIM]
        q_ref[hd, QK_NOPE_DIM:QK_DIM, :] = qr_t[hd * QK_ROPE_DIM:(hd + 1) * QK_ROPE_DIM]
        q_ref[hd, QK_DIM:QK_PAD, :] = zpad_t
        k_ref[hd, :, 0:QK_NOPE_DIM] = kv[:, hd * QK_NOPE_DIM:(hd + 1) * QK_NOPE_DIM].astype(BF16)
        k_ref[hd, :, QK_NOPE_DIM:QK_DIM] = k_rope_b
        k_ref[hd, :, QK_DIM:QK_PAD] = zpad
        v_ref[hd, 0, 0:V_HEAD_DIM, :] = v_t[hd * V_HEAD_DIM:(hd + 1) * V_HEAD_DIM]
        v_ref[hd, 0, V_HEAD_DIM:V_ROWS, :] = ones_t


def _mla_pre(x, g, w_in_e, q_norm, w_uq_e, kv_norm, w_ukv_p, cos_t, sin_t):
    s, d = x.shape
    tm = _row_tile(s, ATTN_TK)
    row = lambda w: pl.BlockSpec((tm, w), lambda i: (i, 0))
    return pl.pallas_call(
        _mla_pre_kernel,
        grid=(s // tm,),
        in_specs=[row(d), _full(g.shape), _full(w_in_e.shape), _full(q_norm.shape),
                  _full(w_uq_e.shape), _full(kv_norm.shape), _full(w_ukv_p.shape),
                  row(LANES), row(LANES)],
        out_specs=[pl.BlockSpec((MLA_HEADS, QK_PAD, tm), lambda i: (0, 0, i)),
                   pl.BlockSpec((MLA_HEADS, tm, QK_PAD), lambda i: (0, i, 0)),
                   pl.BlockSpec((MLA_HEADS, 1, V_ROWS, tm), lambda i: (0, i, 0, 0))],
        out_shape=[jax.ShapeDtypeStruct((MLA_HEADS, QK_PAD, s), BF16),
                   jax.ShapeDtypeStruct((MLA_HEADS, s, QK_PAD), BF16),
                   jax.ShapeDtypeStruct((MLA_HEADS, s // tm, V_ROWS, tm), BF16)],
        compiler_params=_params("parallel"),
        name="mla_pre",
    )(x, g, w_in_e, q_norm, w_uq_e, kv_norm, w_ukv_p, cos_t, sin_t)


def _attn_kernel(q_ref, k_ref, v_ref, o_ref, *scratch, tk, n_chains):
    qi = pl.program_id(1)
    accs = scratch[:n_chains]
    sbuf = [scratch[n_chains + 2 * c:n_chains + 2 * c + 2] for c in range(n_chains)]
    for acc in accs:
        acc[...] = jnp.zeros(acc.shape, F32)

    def scores(task):
        c, j, par, _ = task
        q_t = q_ref[:, c * tk:(c + 1) * tk]
        off = pl.multiple_of(j * tk, tk)
        s = _dot(k_ref[pl.ds(off, tk), :], q_t)
        sbuf[c][par][...] = s
        return jnp.max(s, axis=0, keepdims=True)

    def absorb(task, m_blk, m_old):
        c, j, par, masked = task
        s = sbuf[c][par][...]
        if masked:
            key = lax.broadcasted_iota(jnp.int32, s.shape, 0)
            qry = lax.broadcasted_iota(jnp.int32, s.shape, 1)
            s = jnp.where(key <= qry, s, jnp.finfo(F32).min)
            m_blk = jnp.max(s, axis=0, keepdims=True)
        m_new = jnp.maximum(m_old, m_blk)
        p = jnp.exp2(s - m_new)
        alpha = jnp.exp2(m_old - m_new)
        accs[c][...] = alpha * accs[c][...] + _dot(v_ref[j], p.astype(BF16))
        return m_new

    def run(tasks, lookahead, ms, pending):
        ms = list(ms)
        for i, task in enumerate(tasks):
            nxt = tasks[i + 1] if i + 1 < len(tasks) else lookahead
            staged = scores(nxt) if nxt is not None else None
            ms[task[0]] = absorb(task, pending, ms[task[0]])
            pending = staged
        return ms, pending

    chains = range(n_chains)
    n_full = n_chains * qi

    def body(jj, carry):
        j = 2 * jj
        tasks = [(c, j, 0, False) for c in chains] + [(c, j + 1, 1, False) for c in chains]
        ms, pending = run(tasks, (0, j + 2, 0, False), carry[:-1], carry[-1])
        return (*ms, pending)

    m0 = jnp.full((1, tk), -1e30, F32)
    carry = lax.fori_loop(0, n_full // 2, body, (*[m0] * n_chains, scores((0, 0, 0, False))))
    tail = [(c, n_full + kb, kb % 2, c == kb) for kb in chains for c in range(kb, n_chains)]
    run(tail, None, carry[:-1], carry[-1])
    for c, acc in enumerate(accs):
        out = acc[0:V_HEAD_DIM, :] / acc[V_HEAD_DIM:V_HEAD_DIM + 1, :]
        o_ref[c * tk:(c + 1) * tk, :] = out.T.astype(o_ref.dtype)


def _attention(q_t, k_cat, v_t):
    nh, s, _ = k_cat.shape
    tk = v_t.shape[-1]
    n_chains = min(ATTN_CHAINS, s // tk)
    assert n_chains % 2 == 0
    tq = n_chains * tk
    return pl.pallas_call(
        functools.partial(_attn_kernel, tk=tk, n_chains=n_chains),
        grid=(nh, s // tq),
        in_specs=[pl.BlockSpec((None, QK_PAD, tq), lambda h, i: (h, 0, i)),
                  pl.BlockSpec((None, s, QK_PAD), lambda h, i: (h, 0, 0)),
                  pl.BlockSpec((None, s // tk, V_ROWS, tk), lambda h, i: (h, 0, 0, 0))],
        out_specs=pl.BlockSpec((tq, V_HEAD_DIM), lambda h, i: (i, h)),
        out_shape=jax.ShapeDtypeStruct((s, nh * V_HEAD_DIM), BF16),
        scratch_shapes=([pltpu.VMEM((V_ROWS, tk), F32)] * n_chains
                        + [pltpu.VMEM((tk, tk), F32)] * (2 * n_chains)),
        compiler_params=_params("parallel", "arbitrary"),
        name="mla_attention",
    )(q_t, k_cat, v_t)


FFN_CHUNKS = ((0, 1024), (1024, 2048), (2048, D_FF))


def _mix_ffn_kernel(*refs, gated, final):
    refs = list(refs)
    o_ref = refs.pop()
    fg_ref = refs.pop() if final else None
    x_ref, xp_ref, z_ref, zp_ref = refs[:4]
    gt_ref, gtp_ref = refs[4:6] if gated else (None, None)
    wo_ref, g_ref, wi_ref, cw_ref, cb_ref, wout_ref = refs[-6:]
    i = pl.program_id(0)

    def mixed(xr, zr, gr):
        z = zr[...]
        if gated:
            z = z * gr[...]
        return xr[...] + _dot(z.astype(BF16), wo_ref[...])

    x1 = mixed(x_ref, z_ref, gt_ref)
    x1p = mixed(xp_ref, zp_ref, gtp_ref)
    keep = (i > 0).astype(F32)
    hn = jnp.concatenate([(_rms(x1p, g_ref[...]) * keep).astype(BF16),
                          _rms(x1, g_ref[...]).astype(BF16)], axis=0)
    def gate_up(lo, hi):
        return _dot(hn, wi_ref[:, lo:hi]), _dot(hn[HALO:], wi_ref[:, D_FF + lo:D_FF + hi])

    acc = x1
    nxt = gate_up(*FFN_CHUNKS[0])
    for n, (lo, hi) in enumerate(FFN_CHUNKS):
        gate, up = nxt
        if n + 1 < len(FFN_CHUNKS):
            nxt = gate_up(*FFN_CHUNKS[n + 1])
        conv = (gate[HALO:] * cw_ref[2:3, lo:hi]
                + pltpu.roll(gate, 1, 0)[HALO:] * cw_ref[1:2, lo:hi]
                + pltpu.roll(gate, 2, 0)[HALO:] * cw_ref[0:1, lo:hi] + cb_ref[:, lo:hi])
        act = (conv * jax.nn.sigmoid(conv) * up).astype(BF16)
        acc = acc + _dot(act, wout_ref[lo:hi, :])
    o_ref[...] = _rms(acc, fg_ref[...]) if final else acc


def _mix_ffn(x, z, gate, w_o, g, layer, w_in_all, conv_w, conv_b, w_out_all, final_gain=None):
    s, d = x.shape
    tm = _row_tile(s, 512)
    row = pl.BlockSpec((tm, d), lambda i: (i, 0))
    prev = pl.BlockSpec((HALO, d), lambda i: (jnp.maximum(i * (tm // HALO) - 1, 0), 0))

    def const(a, stacked=False):
        if stacked:
            return pl.BlockSpec((None,) + a.shape[1:], lambda i: (layer,) + (0,) * (a.ndim - 1),
                                pipeline_mode=pl.Buffered(1))
        return pl.BlockSpec(a.shape, lambda i: (0,) * a.ndim, pipeline_mode=pl.Buffered(1))

    gated = gate is not None
    final = final_gain is not None
    consts = [w_o, g, w_in_all, conv_w, conv_b, w_out_all] + ([final_gain] if final else [])
    args = [x, x, z, z] + ([gate, gate] if gated else []) + consts
    specs = ([row, prev, row, prev] + ([row, prev] if gated else [])
             + [const(a, stacked=a is w_in_all or a is w_out_all) for a in consts])
    return pl.pallas_call(
        functools.partial(_mix_ffn_kernel, gated=gated, final=final),
        grid=(s // tm,),
        in_specs=specs,
        out_specs=row,
        out_shape=jax.ShapeDtypeStruct((s, d), F32),
        compiler_params=_params("parallel"),
        name="mix_ffn",
    )(*args)


def _group_sum(x):
    w = 2 * LANES
    r = lax.broadcasted_iota(jnp.int32, (w, w), 0) // RWKV_HEAD
    c = lax.broadcasted_iota(jnp.int32, (w, w), 1) // RWKV_HEAD
    ones_bd = (r == c).astype(BF16)
    hi = x.astype(BF16)
    lo = (x - hi.astype(F32)).astype(BF16)
    cols = [_dot(hi[:, j:j + w], ones_bd) + _dot(lo[:, j:j + w], ones_bd)
            for j in range(0, x.shape[1], w)]
    return cols[0] if len(cols) == 1 else jnp.concatenate(cols, axis=1)


def _rwkv_pre_kernel(*refs, has_vres):
    (x_ref, xp_ref, g_ref, mix_ref, wrkv_ref, w0_ref, w1_ref, w2_ref, a0_ref, a1_ref, a2_ref,
     g1_ref, g2_ref, kk_ref, ka_ref, rk_ref) = refs[:16]
    if has_vres:
        vf_ref, v0_ref, v1_ref, v2_ref = refs[16:20]
        an_o, r_o, bn_o, k_o, vb_o, cl_o, bonus_o, g_o = refs[20:]
    else:
        an_o, r_o, bn_o, k_o, vb_o, cl_o, bonus_o, g_o, v_o = refs[16:]
    i = pl.program_id(0)
    tm = x_ref.shape[0]
    h = _rms(x_ref[...], g_ref[...])
    hp = _rms(xp_ref[...], g_ref[...])[HALO - 1:HALO] * (i > 0).astype(F32)
    row = lax.broadcasted_iota(jnp.int32, h.shape, 0)
    shifted = jnp.where(row == 0, jnp.broadcast_to(hp, h.shape), pltpu.roll(h, 1, 0))
    xx = shifted - h
    mix = mix_ref[...]
    xm = lambda n: (h + xx * mix[n:n + 1]).astype(BF16)
    xv = xm(2)
    r = _dot(xm(0), wrkv_ref[0])
    k = _dot(xm(1), wrkv_ref[1])
    v = _dot(xv, wrkv_ref[2])
    lora_w = _dot(jnp.tanh(_dot(xm(3), w1_ref[...])).astype(BF16), w2_ref[...])
    lw = -math.exp(-0.5) * jax.nn.sigmoid(w0_ref[...] + lora_w)
    a = jax.nn.sigmoid(a0_ref[...] + _dot(_dot(xm(4), a1_ref[...]).astype(BF16), a2_ref[...]))
    g_o[...] = _dot(jax.nn.sigmoid(_dot(xm(5), g1_ref[...])).astype(BF16), g2_ref[...])
    if has_vres:
        gate_v = jax.nn.sigmoid(v0_ref[...] + _dot(_dot(xv, v1_ref[...]).astype(BF16), v2_ref[...]))
        v = v + (vf_ref[...] - v) * gate_v
    else:
        v_o[...] = v
    kk = k * kk_ref[...]
    kk = kk * lax.rsqrt(jnp.maximum(_group_sum(kk * kk), 1e-24))
    k = k * (1.0 + (a - 1.0) * ka_ref[...])
    bonus_o[...] = _group_sum(r * k * rk_ref[...]) * v
    C = WKV_CHUNK
    rr = lax.broadcasted_iota(jnp.int32, (tm, tm), 0)
    cc = lax.broadcasted_iota(jnp.int32, (tm, tm), 1)
    tri = (((rr // C) == (cc // C)) & (cc <= rr)).astype(BF16)
    hi = lw.astype(BF16)
    rem = lw - hi.astype(F32)
    mid = rem.astype(BF16)
    lo = (rem - mid.astype(F32)).astype(BF16)
    cl_o[...] = _dot(tri, hi) + _dot(tri, mid) + _dot(tri, lo)
    an_o[...] = -kk
    r_o[...] = r
    bn_o[...] = kk * a
    k_o[...] = k
    vb_o[...] = v.astype(BF16)


def _rwkv_pre(x, g, p, v_first):
    s, d = x.shape
    tm = _row_tile(s, WKV_TILE)
    row = pl.BlockSpec((tm, d), lambda i: (i, 0))
    prev = pl.BlockSpec((HALO, d), lambda i: (jnp.maximum(i * (tm // HALO) - 1, 0), 0))
    names = ["mix", "w_rkv", "w0", "w1", "w2", "a0", "a1", "a2", "g1", "g2", "k_k", "k_a", "r_k"]
    args = [x, x, g] + [p[n] for n in names]
    specs = [row, prev, _full(g.shape)] + [_full(p[n].shape) for n in names]
    has_vres = v_first is not None
    if has_vres:
        args += [v_first, p["v0"], p["v1"], p["v2"]]
        specs += [row] + [_full(p[n].shape) for n in ("v0", "v1", "v2")]
    out_dtypes = [F32] * 4 + [BF16] + [F32] * 3
    out_specs = [row] * len(out_dtypes)
    out_shape = [jax.ShapeDtypeStruct((s, d), dt) for dt in out_dtypes]
    if not has_vres:
        out_specs.append(row)
        out_shape.append(jax.ShapeDtypeStruct((s, d), F32))
    return pl.pallas_call(
        functools.partial(_rwkv_pre_kernel, has_vres=has_vres),
        grid=(s // tm,),
        in_specs=specs,
        out_specs=out_specs,
        out_shape=out_shape,
        compiler_params=_params("parallel"),
        name="rwkv_pre",
    )(*args)


def _wkv_kernel(an_ref, r_ref, bn_ref, k_ref, v_ref, cl_ref, bonus_ref,
                lnw_ref, lnb_ref, o_ref, s_ref, *, group):
    C = WKV_CHUNK
    C2 = 2 * C
    n_pairs = o_ref.shape[1] // LANES
    n_groups = o_ref.shape[0] // (C * group)

    @pl.when(pl.program_id(0) == 0)
    def _():
        s_ref[...] = jnp.zeros(s_ref.shape, F32)

    lane = lax.broadcasted_iota(jnp.int32, (1, LANES), 1)
    head0 = lane < RWKV_HEAD
    m0 = head0.astype(F32)
    m1 = 1.0 - m0
    t2 = lax.broadcasted_iota(jnp.int32, (C2, C2), 0)
    s2 = lax.broadcasted_iota(jnp.int32, (C2, C2), 1)
    same_head = (t2 // C) == (s2 // C)
    strict = (same_head & (s2 < t2)).astype(F32)
    incl = (same_head & (s2 <= t2)).astype(F32)
    eye = (t2 == s2).astype(F32)

    def level_mask(b):
        return (((t2 // (2 * b)) == (s2 // (2 * b))) & ((t2 % (2 * b)) >= b)
                & ((s2 % (2 * b)) < b)).astype(F32)

    levels = []
    b_sz = 1
    while b_sz < C:
        levels.append(level_mask(b_sz))
        b_sz *= 2

    def stack(x):
        zero = jnp.zeros_like(x)
        return jnp.concatenate([jnp.where(head0, x, zero), jnp.where(head0, zero, x)], axis=0)

    def twice(x):
        return jnp.concatenate([x, x], axis=0)

    def cast(x):
        return x.astype(BF16)

    first_row = lax.broadcasted_iota(jnp.int32, (C, LANES), 0) == 0

    def group_body(g, carry):
        probs = [(ci, p) for ci in range(group) for p in range(n_pairs)]
        offs = [pl.multiple_of((g * group + ci) * C, C) for ci in range(group)]

        def load(ref, ci, p):
            return ref[pl.ds(offs[ci], C), p * LANES:(p + 1) * LANES]

        cl = [load(cl_ref, ci, p) for ci, p in probs]
        cl_end = [c[C - 1:C] for c in cl]
        cl_ex = [jnp.where(first_row, 0.0, pltpu.roll(c, 1, 0)) for c in cl]
        e_neg = [jnp.exp(-c) for c in cl]
        e_rem = [jnp.exp(ce - c) for c, ce in zip(cl, cl_end)]
        w_end = [jnp.exp(ce) for ce in cl_end]
        bn = [load(bn_ref, ci, p) for ci, p in probs]
        kn = [load(k_ref, ci, p) for ci, p in probs]
        a_s = [stack(cast(load(an_ref, ci, p) * jnp.exp(ce))) for (ci, p), ce in zip(probs, cl_ex)]
        r_s = [stack(cast(load(r_ref, ci, p) * jnp.exp(c))) for (ci, p), c in zip(probs, cl)]
        v_s = [stack(load(v_ref, ci, p)) for ci, p in probs]
        hat = [jnp.concatenate([stack(cast(b * e)), stack(cast(k * e))], axis=0)
               for b, k, e in zip(bn, kn, e_rem)]
        til = [jnp.concatenate([twice(cast(b * e)), twice(cast(k * e))], axis=0)
               for b, k, e in zip(bn, kn, e_neg)]
        lm = [_dot_nt(jnp.concatenate([a, r], axis=0), t) for a, r, t in zip(a_s, r_s, til)]
        l_ab = [x[:C2, :C2] * strict for x in lm]
        l_ak = [cast(x[:C2, C2:] * strict) for x in lm]
        m_cat = [cast(jnp.concatenate([x[C2:, :C2] * incl, x[C2:, C2:] * incl], axis=1))
                 for x in lm]
        t_inv = [eye + l * levels[0] for l in l_ab]
        for lvl in levels[1:]:
            t_b = [cast(t) for t in t_inv]
            z = [_dot(t, cast(l * lvl)) for t, l in zip(t_b, l_ab)]
            t_inv = [t + _dot(cast(zz), tb) for t, zz, tb in zip(t_inv, z, t_b)]
        t_b = [cast(t) for t in t_inv]
        p_loc = [_dot(l, v) for l, v in zip(l_ak, v_s)]
        for ci in range(group):
            rows = pl.ds(offs[ci], C)
            sel = [ci * n_pairs + p for p in range(n_pairs)]
            st = [s_ref[p] for p in range(n_pairs)]
            st_b = [cast(s) for s in st]
            x = [_dot_nt(a_s[i], sb) + p_loc[i] for i, sb in zip(sel, st_b)]
            u = [cast(_dot(t_b[i], cast(xx))) for i, xx in zip(sel, x)]
            uv = [jnp.concatenate([uu, v_s[i]], axis=0) for i, uu in zip(sel, u)]
            y = [_dot_nt(r_s[i], sb) + _dot(m_cat[i], w) for i, sb, w in zip(sel, st_b, uv)]
            for p in range(n_pairs):
                cols = slice(p * LANES, (p + 1) * LANES)
                s_ref[p] = st[p] * w_end[sel[p]] + _dot_tn(uv[p], hat[sel[p]])
                yp = y[p][:C] + y[p][C:]
                mu = (jnp.sum(yp * m0, axis=-1, keepdims=True) * m0
                      + jnp.sum(yp * m1, axis=-1, keepdims=True) * m1) * (1.0 / RWKV_HEAD)
                yc = yp - mu
                yc2 = yc * yc
                var = (jnp.sum(yc2 * m0, axis=-1, keepdims=True) * m0
                       + jnp.sum(yc2 * m1, axis=-1, keepdims=True) * m1) * (1.0 / RWKV_HEAD)
                yn = yc * lax.rsqrt(var + GN_EPS)
                o_ref[rows, cols] = yn * lnw_ref[:, cols] + lnb_ref[:, cols] + bonus_ref[rows, cols]
        return carry

    lax.fori_loop(0, n_groups, group_body, 0)


def _wkv(an, r, bn, k, vb, cl, bonus, ln_w, ln_b):
    s, d = an.shape
    tt = _row_tile(s, WKV_STEP)
    group = 2
    row = pl.BlockSpec((tt, d), lambda t: (t, 0))
    return pl.pallas_call(
        functools.partial(_wkv_kernel, group=group),
        grid=(s // tt,),
        in_specs=[row] * 7 + [_full(ln_w.shape), _full(ln_b.shape)],
        out_specs=row,
        out_shape=jax.ShapeDtypeStruct((s, d), F32),
        scratch_shapes=[pltpu.VMEM((d // LANES, LANES, LANES), F32)],
        compiler_params=_params("arbitrary"),
        name="wkv7_scan",
    )(an, r, bn, k, vb, cl, bonus, ln_w, ln_b)


def _rot_cols(w):
    half = QK_ROPE_DIM // 2
    return jnp.concatenate([-w[..., half:], w[..., :half]], axis=-1)


def _mla_weights(w_in, w_uq, w_ukv):
    d = w_in.shape[0]
    lat = Q_LORA_RANK + KV_LORA_RANK
    w_kr = w_in[:, lat:]
    w_in_e = jnp.concatenate([w_in[:, :lat], w_kr, _rot_cols(w_kr)], axis=1).astype(BF16)
    uq = w_uq.reshape(Q_LORA_RANK, MLA_HEADS, QK_DIM)
    uq_rope = uq[:, :, QK_NOPE_DIM:]
    w_uq_e = jnp.concatenate(
        [uq[:, :, :QK_NOPE_DIM].reshape(Q_LORA_RANK, -1),
         uq_rope.reshape(Q_LORA_RANK, -1),
         _rot_cols(uq_rope).reshape(Q_LORA_RANK, -1)], axis=1).astype(BF16)
    ukv = w_ukv.reshape(KV_LORA_RANK, MLA_HEADS, QK_NOPE_DIM + V_HEAD_DIM)
    w_ukv_p = jnp.concatenate(
        [ukv[:, :, :QK_NOPE_DIM].reshape(KV_LORA_RANK, -1),
         ukv[:, :, QK_NOPE_DIM:].reshape(KV_LORA_RANK, -1)], axis=1).astype(BF16)
    del d
    return w_in_e, w_uq_e, w_ukv_p


def _mla_layer(x, g, cos_t, sin_t, w_in, q_norm, w_uq, kv_norm, w_ukv):
    w_in_e, w_uq_e, w_ukv_p = _mla_weights(w_in, w_uq, w_ukv)
    q_cat, k_cat, v = _mla_pre(x, g[None], w_in_e, q_norm[None], w_uq_e, kv_norm[None],
                               w_ukv_p, cos_t, sin_t)
    return _attention(q_cat, k_cat, v)


def _rwkv_layer(x, g, v_first, p):
    row = lambda t: t.reshape(1, -1)
    pp = {
        "mix": p["mix"], "w_rkv": p["w_rkv"].astype(BF16),
        "w0": row(p["w0"]), "w1": p["w1"].astype(BF16), "w2": p["w2"].astype(BF16),
        "a0": row(p["a0"]), "a1": p["a1"].astype(BF16), "a2": p["a2"].astype(BF16),
        "g1": p["g1"].astype(BF16), "g2": p["g2"].astype(BF16),
        "k_k": row(p["k_k"]), "k_a": row(p["k_a"]), "r_k": row(p["r_k"]),
    }
    if v_first is not None:
        pp.update(v0=row(p["v0"]), v1=p["v1"].astype(BF16), v2=p["v2"].astype(BF16))
    outs = _rwkv_pre(x, g[None], pp, v_first)
    scan_in, gate = outs[:7], outs[7]
    z = _wkv(*scan_in, row(p["ln_w"]), row(p["ln_b"]))
    return z, gate, (outs[8] if v_first is None else v_first)


def _mix_ffn_layer(x, z, gate, w_o, g, layer, w_in_all, conv_w, conv_b, w_out_all, final_gain=None):
    return _mix_ffn(x, z, gate, w_o.astype(BF16), g[None], layer, w_in_all, conv_w, conv_b[None],
                    w_out_all, None if final_gain is None else final_gain[None])


def kernel(x, positions, norm_mix, norm_ffn, norm_final, mla_w_in, mla_q_norm, mla_w_uq, mla_kv_norm, mla_w_ukv, mla_w_o, rwkv_mix, rwkv_w_rkv, rwkv_w0, rwkv_w1, rwkv_w2, rwkv_a0, rwkv_a1, rwkv_a2, rwkv_v0, rwkv_v1, rwkv_v2, rwkv_g1, rwkv_g2, rwkv_k_k, rwkv_k_a, rwkv_r_k, rwkv_ln_w, rwkv_ln_b, rwkv_w_o, ffn_w_in, ffn_conv_w, ffn_conv_b, ffn_w_out):
    batch, seq, d = x.shape
    assert batch == 1
    depth = norm_mix.shape[0]
    half = QK_ROPE_DIM // 2
    inv_freq = ROPE_THETA ** (-jnp.arange(half, dtype=F32) / half)
    freq_row = jnp.tile(inv_freq, LANES // half)[None]
    cos_t, sin_t = _rope_tables(positions.astype(F32).reshape(seq, 1), freq_row)
    xs = x.reshape(seq, d)
    ffn_w_in_b = ffn_w_in.astype(BF16)
    ffn_w_out_b = ffn_w_out.astype(BF16)
    v_first = None
    for i in range(depth):
        j = i // 2
        if i % 2 == 0:
            z = _mla_layer(xs, norm_mix[i], cos_t, sin_t, mla_w_in[j], mla_q_norm[j], mla_w_uq[j],
                           mla_kv_norm[j], mla_w_ukv[j])
            gate, w_o = None, mla_w_o[j]
        else:
            p = {"mix": rwkv_mix[j], "w_rkv": rwkv_w_rkv[j], "w0": rwkv_w0[j], "w1": rwkv_w1[j],
                 "w2": rwkv_w2[j], "a0": rwkv_a0[j], "a1": rwkv_a1[j], "a2": rwkv_a2[j],
                 "g1": rwkv_g1[j], "g2": rwkv_g2[j], "k_k": rwkv_k_k[j], "k_a": rwkv_k_a[j],
                 "r_k": rwkv_r_k[j], "ln_w": rwkv_ln_w[j], "ln_b": rwkv_ln_b[j]}
            if j > 0:
                p.update(v0=rwkv_v0[j - 1], v1=rwkv_v1[j - 1], v2=rwkv_v2[j - 1])
            z, gate, v_first = _rwkv_layer(xs, norm_mix[i], v_first, p)
            w_o = rwkv_w_o[j]
        xs = _mix_ffn_layer(xs, z, gate, w_o, norm_ffn[i], i, ffn_w_in_b, ffn_conv_w[i], ffn_conv_b[i],
                            ffn_w_out_b, norm_final if i == depth - 1 else None)
    return xs.reshape(batch, seq, d)
```

```python
import functools
import math

import jax
import jax.numpy as jnp
from jax import lax
from jax.experimental import pallas as pl
from jax.experimental.pallas import tpu as pltpu

NORM_EPS = 1e-6
GN_EPS = 64e-5
MLA_HEADS = 8
QK_NOPE_DIM = 128
QK_ROPE_DIM = 64
QK_DIM = QK_NOPE_DIM + QK_ROPE_DIM
QK_PAD = 256
V_HEAD_DIM = 128
V_ROWS = V_HEAD_DIM + 16
Q_LORA_RANK = 384
KV_LORA_RANK = 256
ROPE_THETA = 10000.0
RWKV_HEAD = 64
LANES = 128
D_FF = 2816
HALO = 16
WKV_CHUNK = 64
WKV_TILE = 256
WKV_STEP = 512
ATTN_TK = 512
ATTN_CHAINS = 8

BF16 = jnp.bfloat16
F32 = jnp.float32
VMEM_LIMIT = 56 * 1024 * 1024


def _params(*sem):
    return pltpu.CompilerParams(dimension_semantics=sem, vmem_limit_bytes=VMEM_LIMIT)


def _dot(a, b):
    return jnp.dot(a, b, preferred_element_type=F32)


def _dot_nt(a, b):
    return lax.dot_general(a, b, (((1,), (1,)), ((), ())), preferred_element_type=F32)


def _dot_tn(a, b):
    return lax.dot_general(a, b, (((0,), (0,)), ((), ())), preferred_element_type=F32)


def _rms(x, g):
    return x * lax.rsqrt(jnp.mean(x * x, axis=-1, keepdims=True) + NORM_EPS) * g


def _row_tile(s, want):
    return min(want, s)


def _full(shape):
    return pl.BlockSpec(shape, lambda *_: (0,) * len(shape))


def _rope_table_kernel(pos_ref, freq_ref, cos_ref, sin_ref):
    ang = pos_ref[...] * freq_ref[...]
    cos_ref[...] = jnp.cos(ang)
    sin_ref[...] = jnp.sin(ang)


def _rope_tables(pos_col, freq_row):
    s = pos_col.shape[0]
    tm = _row_tile(s, 1024)
    return pl.pallas_call(
        _rope_table_kernel,
        grid=(s // tm,),
        in_specs=[pl.BlockSpec((tm, 1), lambda i: (i, 0)), _full((1, LANES))],
        out_specs=[pl.BlockSpec((tm, LANES), lambda i: (i, 0))] * 2,
        out_shape=[jax.ShapeDtypeStruct((s, LANES), F32)] * 2,
        compiler_params=_params("parallel"),
        name="rope_tables",
    )(pos_col, freq_row)


def _mla_pre_kernel(x_ref, g_ref, w_in_ref, qn_ref, w_uq_ref, kvn_ref, w_ukv_ref,
                    cos_ref, sin_ref, q_ref, k_ref, v_ref):
    h = _rms(x_ref[...], g_ref[...]).astype(BF16)
    lat = _dot(h, w_in_ref[...])
    q_lat = lat[:, :Q_LORA_RANK]
    c_kv = lat[:, Q_LORA_RANK:Q_LORA_RANK + KV_LORA_RANK]
    kr = lat[:, Q_LORA_RANK + KV_LORA_RANK:]
    cos = cos_ref[...]
    sin = sin_ref[...]
    k_rope = (kr[:, :QK_ROPE_DIM] * cos[:, :QK_ROPE_DIM]
              + kr[:, QK_ROPE_DIM:] * sin[:, :QK_ROPE_DIM])
    qn = _rms(q_lat, qn_ref[...]).astype(BF16)
    q_all = _dot(qn, w_uq_ref[...])
    n_nope = MLA_HEADS * QK_NOPE_DIM
    n_rope = MLA_HEADS * QK_ROPE_DIM
    reps = n_rope // LANES
    qscale = QK_DIM ** -0.5 * math.log2(math.e)
    q_rope = (q_all[:, n_nope:n_nope + n_rope] * jnp.tile(cos, (1, reps))
              + q_all[:, n_nope + n_rope:] * jnp.tile(sin, (1, reps))) * qscale
    kvn = _rms(c_kv, kvn_ref[...]).astype(BF16)
    kv = _dot(kvn, w_ukv_ref[...])
    tm = x_ref.shape[0]
    qn_t = (q_all[:, :n_nope] * qscale).T.astype(BF16)
    qr_t = q_rope.T.astype(BF16)
    v_t = kv[:, n_nope:].T.astype(BF16)
    zpad_t = jnp.zeros((QK_PAD - QK_DIM, tm), BF16)
    ones_t = jnp.ones((V_ROWS - V_HEAD_DIM, tm), BF16)
    zpad = jnp.zeros((tm, QK_PAD - QK_DIM), BF16)
    k_rope_b = k_rope.astype(BF16)
    for hd in range(MLA_HEADS):
        q_ref[hd, 0:QK_NOPE_DIM, :] = qn_t[hd * QK_NOPE_DIM:(hd + 1) * QK_NOPE_DIM]
        q_ref[hd, QK_NOPE_DIM:QK_DIM, :] = qr_t[hd * QK_ROPE_DIM:(hd + 1) * QK_ROPE_DIM]
        q_ref[hd, QK_DIM:QK_PAD, :] = zpad_t
        k_ref[hd, :, 0:QK_NOPE_DIM] = kv[:, hd * QK_NOPE_DIM:(hd + 1) * QK_NOPE_DIM].astype(BF16)
        k_ref[hd, :, QK_NOPE_DIM:QK_DIM] = k_rope_b
        k_ref[hd, :, QK_DIM:QK_PAD] = zpad
        v_ref[hd, 0, 0:V_HEAD_DIM, :] = v_t[hd * V_HEAD_DIM:(hd + 1) * V_HEAD_DIM]
        v_ref[hd, 0, V_HEAD_DIM:V_ROWS, :] = ones_t


def _mla_pre(x, g, w_in_e, q_norm, w_uq_e, kv_norm, w_ukv_p, cos_t, sin_t):
    s, d = x.shape
    tm = _row_tile(s, ATTN_TK)
    row = lambda w: pl.BlockSpec((tm, w), lambda i: (i, 0))
    return pl.pallas_call(
        _mla_pre_kernel,
        grid=(s // tm,),
        in_specs=[row(d), _full(g.shape), _full(w_in_e.shape), _full(q_norm.shape),
                  _full(w_uq_e.shape), _full(kv_norm.shape), _full(w_ukv_p.shape),
                  row(LANES), row(LANES)],
        out_specs=[pl.BlockSpec((MLA_HEADS, QK_PAD, tm), lambda i: (0, 0, i)),
                   pl.BlockSpec((MLA_HEADS, tm, QK_PAD), lambda i: (0, i, 0)),
                   pl.BlockSpec((MLA_HEADS, 1, V_ROWS, tm), lambda i: (0, i, 0, 0))],
        out_shape=[jax.ShapeDtypeStruct((MLA_HEADS, QK_PAD, s), BF16),
                   jax.ShapeDtypeStruct((MLA_HEADS, s, QK_PAD), BF16),
                   jax.ShapeDtypeStruct((MLA_HEADS, s // tm, V_ROWS, tm), BF16)],
        compiler_params=_params("parallel"),
        name="mla_pre",
    )(x, g, w_in_e, q_norm, w_uq_e, kv_norm, w_ukv_p, cos_t, sin_t)


def _attn_kernel(q_ref, k_ref, v_ref, o_ref, *scratch, tk, n_chains):
    qi = pl.program_id(1)
    accs = scratch[:n_chains]
    sbuf = [scratch[n_chains + 2 * c:n_chains + 2 * c + 2] for c in range(n_chains)]
    for acc in accs:
        acc[...] = jnp.zeros(acc.shape, F32)

    def scores(task):
        c, j, par, _ = task
        q_t = q_ref[:, c * tk:(c + 1) * tk]
        off = pl.multiple_of(j * tk, tk)
        s = _dot(k_ref[pl.ds(off, tk), :], q_t)
        sbuf[c][par][...] = s
        return jnp.max(s, axis=0, keepdims=True)

    def absorb(task, m_blk, m_old):
        c, j, par, masked = task
        s = sbuf[c][par][...]
        if masked:
            key = lax.broadcasted_iota(jnp.int32, s.shape, 0)
            qry = lax.broadcasted_iota(jnp.int32, s.shape, 1)
            s = jnp.where(key <= qry, s, jnp.finfo(F32).min)
            m_blk = jnp.max(s, axis=0, keepdims=True)
        m_new = jnp.maximum(m_old, m_blk)
        p = jnp.exp2(s - m_new)
        alpha = jnp.exp2(m_old - m_new)
        accs[c][...] = alpha * accs[c][...] + _dot(v_ref[j], p.astype(BF16))
        return m_new

    def run(tasks, lookahead, ms, pending):
        ms = list(ms)
        for i, task in enumerate(tasks):
            nxt = tasks[i + 1] if i + 1 < len(tasks) else lookahead
            staged = scores(nxt) if nxt is not None else None
            ms[task[0]] = absorb(task, pending, ms[task[0]])
            pending = staged
        return ms, pending

    chains = range(n_chains)
    n_full = n_chains * qi

    def body(jj, carry):
        j = 2 * jj
        tasks = [(c, j, 0, False) for c in chains] + [(c, j + 1, 1, False) for c in chains]
        ms, pending = run(tasks, (0, j + 2, 0, False), carry[:-1], carry[-1])
        return (*ms, pending)

    m0 = jnp.full((1, tk), -1e30, F32)
    carry = lax.fori_loop(0, n_full // 2, body, (*[m0] * n_chains, scores((0, 0, 0, False))))
    tail = [(c, n_full + kb, kb % 2, c == kb) for kb in chains for c in range(kb, n_chains)]
    run(tail, None, carry[:-1], carry[-1])
    for c, acc in enumerate(accs):
        out = acc[0:V_HEAD_DIM, :] / acc[V_HEAD_DIM:V_HEAD_DIM + 1, :]
        o_ref[c * tk:(c + 1) * tk, :] = out.T.astype(o_ref.dtype)


def _attention(q_t, k_cat, v_t):
    nh, s, _ = k_cat.shape
    tk = v_t.shape[-1]
    n_chains = min(ATTN_CHAINS, s // tk)
    assert n_chains % 2 == 0
    tq = n_chains * tk
    return pl.pallas_call(
        functools.partial(_attn_kernel, tk=tk, n_chains=n_chains),
        grid=(nh, s // tq),
        in_specs=[pl.BlockSpec((None, QK_PAD, tq), lambda h, i: (h, 0, i)),
                  pl.BlockSpec((None, s, QK_PAD), lambda h, i: (h, 0, 0)),
                  pl.BlockSpec((None, s // tk, V_ROWS, tk), lambda h, i: (h, 0, 0, 0))],
        out_specs=pl.BlockSpec((tq, V_HEAD_DIM), lambda h, i: (i, h)),
        out_shape=jax.ShapeDtypeStruct((s, nh * V_HEAD_DIM), BF16),
        scratch_shapes=([pltpu.VMEM((V_ROWS, tk), F32)] * n_chains
                        + [pltpu.VMEM((tk, tk), F32)] * (2 * n_chains)),
        compiler_params=_params("parallel", "arbitrary"),
        name="mla_attention",
    )(q_t, k_cat, v_t)


FFN_CHUNKS = ((0, 1024), (1024, 2048), (2048, D_FF))


def _mix_ffn_kernel(*refs, gated, final):
    refs = list(refs)
    o_ref = refs.pop()
    fg_ref = refs.pop() if final else None
    x_ref, xp_ref, z_ref, zp_ref = refs[:4]
    gt_ref, gtp_ref = refs[4:6] if gated else (None, None)
    wo_ref, g_ref, wi_ref, cw_ref, cb_ref, wout_ref = refs[-6:]
    i = pl.program_id(0)

    def mixed(xr, zr, gr):
        z = zr[...]
        if gated:
            z = z * gr[...]
        return xr[...] + _dot(z.astype(BF16), wo_ref[...])

    x1 = mixed(x_ref, z_ref, gt_ref)
    x1p = mixed(xp_ref, zp_ref, gtp_ref)
    keep = (i > 0).astype(F32)
    hn = jnp.concatenate([(_rms(x1p, g_ref[...]) * keep).astype(BF16),
                          _rms(x1, g_ref[...]).astype(BF16)], axis=0)
    def gate_up(lo, hi):
        return _dot(hn, wi_ref[:, lo:hi]), _dot(hn[HALO:], wi_ref[:, D_FF + lo:D_FF + hi])

    acc = x1
    nxt = gate_up(*FFN_CHUNKS[0])
    for n, (lo, hi) in enumerate(FFN_CHUNKS):
        gate, up = nxt
        if n + 1 < len(FFN_CHUNKS):
            nxt = gate_up(*FFN_CHUNKS[n + 1])
        conv = (gate[HALO:] * cw_ref[2:3, lo:hi]
                + pltpu.roll(gate, 1, 0)[HALO:] * cw_ref[1:2, lo:hi]
                + pltpu.roll(gate, 2, 0)[HALO:] * cw_ref[0:1, lo:hi] + cb_ref[:, lo:hi])
        act = (conv * jax.nn.sigmoid(conv) * up).astype(BF16)
        acc = acc + _dot(act, wout_ref[lo:hi, :])
    o_ref[...] = _rms(acc, fg_ref[...]) if final else acc


def _mix_ffn(x, z, gate, w_o, g, layer, w_in_all, conv_w, conv_b, w_out_all, final_gain=None):
    s, d = x.shape
    tm = _row_tile(s, 512)
    row = pl.BlockSpec((tm, d), lambda i: (i, 0))
    prev = pl.BlockSpec((HALO, d), lambda i: (jnp.maximum(i * (tm // HALO) - 1, 0), 0))

    def const(a, stacked=False):
        if stacked:
            return pl.BlockSpec((None,) + a.shape[1:], lambda i: (layer,) + (0,) * (a.ndim - 1),
                                pipeline_mode=pl.Buffered(1))
        return pl.BlockSpec(a.shape, lambda i: (0,) * a.ndim, pipeline_mode=pl.Buffered(1))

    gated = gate is not None
    final = final_gain is not None
    consts = [w_o, g, w_in_all, conv_w, conv_b, w_out_all] + ([final_gain] if final else [])
    args = [x, x, z, z] + ([gate, gate] if gated else []) + consts
    specs = ([row, prev, row, prev] + ([row, prev] if gated else [])
             + [const(a, stacked=a is w_in_all or a is w_out_all) for a in consts])
    return pl.pallas_call(
        functools.partial(_mix_ffn_kernel, gated=gated, final=final),
        grid=(s // tm,),
        in_specs=specs,
        out_specs=row,
        out_shape=jax.ShapeDtypeStruct((s, d), F32),
        compiler_params=_params("parallel"),
        name="mix_ffn",
    )(*args)


def _rwkv_pre_kernel(*refs, has_vres):
    (x_ref, xp_ref, g_ref, mix_ref, wrkv_ref, w0_ref, w1_ref, w2_ref, a0_ref, a1_ref, a2_ref,
     g1_ref, g2_ref) = refs[:13]
    if has_vres:
        vf_ref, v0_ref, v1_ref, v2_ref = refs[13:17]
    r_o, k_o, v_o, a_o, cl_o, g_o = refs[-6:]
    i = pl.program_id(0)
    tm = x_ref.shape[0]
    h = _rms(x_ref[...], g_ref[...])
    hp = _rms(xp_ref[...], g_ref[...])[HALO - 1:HALO] * (i > 0).astype(F32)
    row = lax.broadcasted_iota(jnp.int32, h.shape, 0)
    shifted = jnp.where(row == 0, jnp.broadcast_to(hp, h.shape), pltpu.roll(h, 1, 0))
    xx = shifted - h
    mix = mix_ref[...]
    xm = lambda n: (h + xx * mix[n:n + 1]).astype(BF16)
    xv = xm(2)
    r = _dot(xm(0), wrkv_ref[0])
    k = _dot(xm(1), wrkv_ref[1])
    v = _dot(xv, wrkv_ref[2])
    lora_w = _dot(jnp.tanh(_dot(xm(3), w1_ref[...])).astype(BF16), w2_ref[...])
    lw = -math.exp(-0.5) * jax.nn.sigmoid(w0_ref[...] + lora_w)
    a_o[...] = jax.nn.sigmoid(a0_ref[...] + _dot(_dot(xm(4), a1_ref[...]).astype(BF16), a2_ref[...]))
    g_o[...] = _dot(jax.nn.sigmoid(_dot(xm(5), g1_ref[...])).astype(BF16), g2_ref[...])
    if has_vres:
        gate_v = jax.nn.sigmoid(v0_ref[...] + _dot(_dot(xv, v1_ref[...]).astype(BF16), v2_ref[...]))
        v = v + (vf_ref[...] - v) * gate_v
    r_o[...] = r
    k_o[...] = k
    v_o[...] = v
    C = WKV_CHUNK
    rr = lax.broadcasted_iota(jnp.int32, (tm, tm), 0)
    cc = lax.broadcasted_iota(jnp.int32, (tm, tm), 1)
    tri = (((rr // C) == (cc // C)) & (cc <= rr)).astype(BF16)
    hi = lw.astype(BF16)
    rem = lw - hi.astype(F32)
    mid = rem.astype(BF16)
    lo = (rem - mid.astype(F32)).astype(BF16)
    cl_o[...] = _dot(tri, hi) + _dot(tri, mid) + _dot(tri, lo)


def _rwkv_pre(x, g, p, v_first):
    s, d = x.shape
    tm = _row_tile(s, WKV_TILE)
    row = pl.BlockSpec((tm, d), lambda i: (i, 0))
    prev = pl.BlockSpec((HALO, d), lambda i: (jnp.maximum(i * (tm // HALO) - 1, 0), 0))
    names = ["mix", "w_rkv", "w0", "w1", "w2", "a0", "a1", "a2", "g1", "g2"]
    args = [x, x, g] + [p[n] for n in names]
    specs = [row, prev, _full(g.shape)] + [_full(p[n].shape) for n in names]
    has_vres = v_first is not None
    if has_vres:
        args += [v_first, p["v0"], p["v1"], p["v2"]]
        specs += [row] + [_full(p[n].shape) for n in ("v0", "v1", "v2")]
    n_out = 6
    out_specs = [row] * n_out
    out_shape = [jax.ShapeDtypeStruct((s, d), F32)] * n_out
    return pl.pallas_call(
        functools.partial(_rwkv_pre_kernel, has_vres=has_vres),
        grid=(s // tm,),
        in_specs=specs,
        out_specs=out_specs,
        out_shape=out_shape,
        compiler_params=_params("parallel"),
        name="rwkv_pre",
    )(*args)


def _wkv_kernel(r_ref, k_ref, v_ref, a_ref, cl_ref, kk_ref, ka_ref, rk_ref,
                lnw_ref, lnb_ref, o_ref, s_ref, *, group):
    C = WKV_CHUNK
    C2 = 2 * C
    n_pairs = o_ref.shape[1] // LANES
    n_groups = o_ref.shape[0] // (C * group)

    @pl.when(pl.program_id(0) == 0)
    def _():
        s_ref[...] = jnp.zeros(s_ref.shape, F32)

    lane = lax.broadcasted_iota(jnp.int32, (1, LANES), 1)
    head0 = lane < RWKV_HEAD
    m0 = head0.astype(F32)
    m1 = 1.0 - m0
    t2 = lax.broadcasted_iota(jnp.int32, (C2, C2), 0)
    s2 = lax.broadcasted_iota(jnp.int32, (C2, C2), 1)
    same_head = (t2 // C) == (s2 // C)
    strict = (same_head & (s2 < t2)).astype(F32)
    incl = (same_head & (s2 <= t2)).astype(F32)
    eye = (t2 == s2).astype(F32)

    def level_mask(b):
        return (((t2 // (2 * b)) == (s2 // (2 * b))) & ((t2 % (2 * b)) >= b)
                & ((s2 % (2 * b)) < b)).astype(F32)

    levels = []
    b_sz = 1
    while b_sz < C:
        levels.append(level_mask(b_sz))
        b_sz *= 2

    def stack(x):
        zero = jnp.zeros_like(x)
        return jnp.concatenate([jnp.where(head0, x, zero), jnp.where(head0, zero, x)], axis=0)

    def twice(x):
        return jnp.concatenate([x, x], axis=0)

    def cast(x):
        return x.astype(BF16)

    first_row = lax.broadcasted_iota(jnp.int32, (C, LANES), 0) == 0

    def head_sum(x):
        return (jnp.sum(x * m0, axis=-1, keepdims=True) * m0
                + jnp.sum(x * m1, axis=-1, keepdims=True) * m1)

    def group_body(g, carry):
        probs = [(ci, p) for ci in range(group) for p in range(n_pairs)]
        offs = [pl.multiple_of((g * group + ci) * C, C) for ci in range(group)]

        def load(ref, ci, p):
            return ref[pl.ds(offs[ci], C), p * LANES:(p + 1) * LANES]

        cl = [load(cl_ref, ci, p) for ci, p in probs]
        cl_end = [c[C - 1:C] for c in cl]
        cl_ex = [jnp.where(first_row, 0.0, pltpu.roll(c, 1, 0)) for c in cl]
        e_neg = [jnp.exp(-c) for c in cl]
        e_rem = [jnp.exp(ce - c) for c, ce in zip(cl, cl_end)]
        w_end = [jnp.exp(ce) for ce in cl_end]
        lanes = [slice(p * LANES, (p + 1) * LANES) for _, p in probs]
        r = [load(r_ref, ci, p) for ci, p in probs]
        k = [load(k_ref, ci, p) for ci, p in probs]
        v = [load(v_ref, ci, p) for ci, p in probs]
        a = [load(a_ref, ci, p) for ci, p in probs]
        kk = [x * kk_ref[:, ln] for x, ln in zip(k, lanes)]
        kk = [x * lax.rsqrt(jnp.maximum(head_sum(x * x), 1e-24)) for x in kk]
        kn = [x * (1.0 + (aa - 1.0) * ka_ref[:, ln]) for x, aa, ln in zip(k, a, lanes)]
        bonus = [head_sum(rr * x * rk_ref[:, ln]) * vv for rr, x, vv, ln in zip(r, kn, v, lanes)]
        bn = [x * aa for x, aa in zip(kk, a)]
        a_s = [stack(cast(-x * jnp.exp(ce))) for x, ce in zip(kk, cl_ex)]
        r_s = [stack(cast(rr * jnp.exp(c))) for rr, c in zip(r, cl)]
        v_s = [stack(cast(vv)) for vv in v]
        hat = [jnp.concatenate([stack(cast(b * e)), stack(cast(k * e))], axis=0)
               for b, k, e in zip(bn, kn, e_rem)]
        til = [jnp.concatenate([twice(cast(b * e)), twice(cast(k * e))], axis=0)
               for b, k, e in zip(bn, kn, e_neg)]
        lm = [_dot_nt(jnp.concatenate([a, r], axis=0), t) for a, r, t in zip(a_s, r_s, til)]
        l_ab = [x[:C2, :C2] * strict for x in lm]
        l_ak = [cast(x[:C2, C2:] * strict) for x in lm]
        m_cat = [cast(jnp.concatenate([x[C2:, :C2] * incl, x[C2:, C2:] * incl], axis=1))
                 for x in lm]
        t_inv = [eye + l * levels[0] for l in l_ab]
        for lvl in levels[1:]:
            t_b = [cast(t) for t in t_inv]
            z = [_dot(t, cast(l * lvl)) for t, l in zip(t_b, l_ab)]
            t_inv = [t + _dot(cast(zz), tb) for t, zz, tb in zip(t_inv, z, t_b)]
        t_b = [cast(t) for t in t_inv]
        p_loc = [_dot(l, v) for l, v in zip(l_ak, v_s)]
        for ci in range(group):
            rows = pl.ds(offs[ci], C)
            sel = [ci * n_pairs + p for p in range(n_pairs)]
            st = [s_ref[p] for p in range(n_pairs)]
            st_b = [cast(s) for s in st]
            x = [_dot_nt(a_s[i], sb) + p_loc[i] for i, sb in zip(sel, st_b)]
            u = [cast(_dot(t_b[i], cast(xx))) for i, xx in zip(sel, x)]
            uv = [jnp.concatenate([uu, v_s[i]], axis=0) for i, uu in zip(sel, u)]
            y = [_dot_nt(r_s[i], sb) + _dot(m_cat[i], w) for i, sb, w in zip(sel, st_b, uv)]
            for p in range(n_pairs):
                cols = slice(p * LANES, (p + 1) * LANES)
                s_ref[p] = st[p] * w_end[sel[p]] + _dot_tn(uv[p], hat[sel[p]])
                yp = y[p][:C] + y[p][C:]
                yc = yp - head_sum(yp) * (1.0 / RWKV_HEAD)
                var = head_sum(yc * yc) * (1.0 / RWKV_HEAD)
                yn = yc * lax.rsqrt(var + GN_EPS)
                o_ref[rows, cols] = yn * lnw_ref[:, cols] + lnb_ref[:, cols] + bonus[sel[p]]
        return carry

    lax.fori_loop(0, n_groups, group_body, 0)


def _wkv(r, k, v, a, cl, k_k, k_a, r_k, ln_w, ln_b):
    s, d = r.shape
    tt = _row_tile(s, WKV_STEP)
    group = 2
    row = pl.BlockSpec((tt, d), lambda t: (t, 0))
    return pl.pallas_call(
        functools.partial(_wkv_kernel, group=group),
        grid=(s // tt,),
        in_specs=[row] * 5 + [_full(ln_w.shape)] * 5,
        out_specs=row,
        out_shape=jax.ShapeDtypeStruct((s, d), F32),
        scratch_shapes=[pltpu.VMEM((d // LANES, LANES, LANES), F32)],
        compiler_params=_params("arbitrary"),
        name="wkv7_scan",
    )(r, k, v, a, cl, k_k, k_a, r_k, ln_w, ln_b)


def _rot_cols(w):
    half = QK_ROPE_DIM // 2
    return jnp.concatenate([-w[..., half:], w[..., :half]], axis=-1)


def _mla_weights(w_in, w_uq, w_ukv):
    d = w_in.shape[0]
    lat = Q_LORA_RANK + KV_LORA_RANK
    w_kr = w_in[:, lat:]
    w_in_e = jnp.concatenate([w_in[:, :lat], w_kr, _rot_cols(w_kr)], axis=1).astype(BF16)
    uq = w_uq.reshape(Q_LORA_RANK, MLA_HEADS, QK_DIM)
    uq_rope = uq[:, :, QK_NOPE_DIM:]
    w_uq_e = jnp.concatenate(
        [uq[:, :, :QK_NOPE_DIM].reshape(Q_LORA_RANK, -1),
         uq_rope.reshape(Q_LORA_RANK, -1),
         _rot_cols(uq_rope).reshape(Q_LORA_RANK, -1)], axis=1).astype(BF16)
    ukv = w_ukv.reshape(KV_LORA_RANK, MLA_HEADS, QK_NOPE_DIM + V_HEAD_DIM)
    w_ukv_p = jnp.concatenate(
        [ukv[:, :, :QK_NOPE_DIM].reshape(KV_LORA_RANK, -1),
         ukv[:, :, QK_NOPE_DIM:].reshape(KV_LORA_RANK, -1)], axis=1).astype(BF16)
    del d
    return w_in_e, w_uq_e, w_ukv_p


def _mla_layer(x, g, cos_t, sin_t, w_in, q_norm, w_uq, kv_norm, w_ukv):
    w_in_e, w_uq_e, w_ukv_p = _mla_weights(w_in, w_uq, w_ukv)
    q_cat, k_cat, v = _mla_pre(x, g[None], w_in_e, q_norm[None], w_uq_e, kv_norm[None],
                               w_ukv_p, cos_t, sin_t)
    return _attention(q_cat, k_cat, v)


def _rwkv_layer(x, g, v_first, p):
    row = lambda t: t.reshape(1, -1)
    pp = {
        "mix": p["mix"], "w_rkv": p["w_rkv"].astype(BF16),
        "w0": row(p["w0"]), "w1": p["w1"].astype(BF16), "w2": p["w2"].astype(BF16),
        "a0": row(p["a0"]), "a1": p["a1"].astype(BF16), "a2": p["a2"].astype(BF16),
        "g1": p["g1"].astype(BF16), "g2": p["g2"].astype(BF16),
    }
    if v_first is not None:
        pp.update(v0=row(p["v0"]), v1=p["v1"].astype(BF16), v2=p["v2"].astype(BF16))
    outs = _rwkv_pre(x, g[None], pp, v_first)
    scan_in, gate = outs[:5], outs[5]
    z = _wkv(*scan_in, *(row(p[n]) for n in ("k_k", "k_a", "r_k", "ln_w", "ln_b")))
    return z, gate, (outs[2] if v_first is None else v_first)


def _mix_ffn_layer(x, z, gate, w_o, g, layer, w_in_all, conv_w, conv_b, w_out_all, final_gain=None):
    return _mix_ffn(x, z, gate, w_o.astype(BF16), g[None], layer, w_in_all, conv_w, conv_b[None],
                    w_out_all, None if final_gain is None else final_gain[None])


def kernel(x, positions, norm_mix, norm_ffn, norm_final, mla_w_in, mla_q_norm, mla_w_uq, mla_kv_norm, mla_w_ukv, mla_w_o, rwkv_mix, rwkv_w_rkv, rwkv_w0, rwkv_w1, rwkv_w2, rwkv_a0, rwkv_a1, rwkv_a2, rwkv_v0, rwkv_v1, rwkv_v2, rwkv_g1, rwkv_g2, rwkv_k_k, rwkv_k_a, rwkv_r_k, rwkv_ln_w, rwkv_ln_b, rwkv_w_o, ffn_w_in, ffn_conv_w, ffn_conv_b, ffn_w_out):
    batch, seq, d = x.shape
    assert batch == 1
    depth = norm_mix.shape[0]
    half = QK_ROPE_DIM // 2
    inv_freq = ROPE_THETA ** (-jnp.arange(half, dtype=F32) / half)
    freq_row = jnp.tile(inv_freq, LANES // half)[None]
    cos_t, sin_t = _rope_tables(positions.astype(F32).reshape(seq, 1), freq_row)
    xs = x.reshape(seq, d)
    ffn_w_in_b = ffn_w_in.astype(BF16)
    ffn_w_out_b = ffn_w_out.astype(BF16)
    v_first = None
    for i in range(depth):
        j = i // 2
        if i % 2 == 0:
            z = _mla_layer(xs, norm_mix[i], cos_t, sin_t, mla_w_in[j], mla_q_norm[j], mla_w_uq[j],
                           mla_kv_norm[j], mla_w_ukv[j])
            gate, w_o = None, mla_w_o[j]
        else:
            p = {"mix": rwkv_mix[j], "w_rkv": rwkv_w_rkv[j], "w0": rwkv_w0[j], "w1": rwkv_w1[j],
                 "w2": rwkv_w2[j], "a0": rwkv_a0[j], "a1": rwkv_a1[j], "a2": rwkv_a2[j],
                 "g1": rwkv_g1[j], "g2": rwkv_g2[j], "k_k": rwkv_k_k[j], "k_a": rwkv_k_a[j],
                 "r_k": rwkv_r_k[j], "ln_w": rwkv_ln_w[j], "ln_b": rwkv_ln_b[j]}
            if j > 0:
                p.update(v0=rwkv_v0[j - 1], v1=rwkv_v1[j - 1], v2=rwkv_v2[j - 1])
            z, gate, v_first = _rwkv_layer(xs, norm_mix[i], v_first, p)
            w_o = rwkv_w_o[j]
        xs = _mix_ffn_layer(xs, z, gate, w_o, norm_ffn[i], i, ffn_w_in_b, ffn_conv_w[i], ffn_conv_b[i],
                            ffn_w_out_b, norm_final if i == depth - 1 else None)
    return xs.reshape(batch, seq, d)
```

```python
import functools
import math

import jax
import jax.numpy as jnp
from jax import lax
from jax.experimental import pallas as pl
from jax.experimental.pallas import tpu as pltpu

NORM_EPS = 1e-6
GN_EPS = 64e-5
MLA_HEADS = 8
QK_NOPE_DIM = 128
QK_ROPE_DIM = 64
QK_DIM = QK_NOPE_DIM + QK_ROPE_DIM
QK_PAD = 256
V_HEAD_DIM = 128
V_ROWS = V_HEAD_DIM + 16
Q_LORA_RANK = 384
KV_LORA_RANK = 256
ROPE_THETA = 10000.0
RWKV_HEAD = 64
LANES = 128
D_FF = 2816
HALO = 16
WKV_CHUNK = 64
WKV_TILE = 256
WKV_STEP = 512
ATTN_TK = 512
ATTN_CHAINS = 8
ATTN_UNROLL = 4

BF16 = jnp.bfloat16
F32 = jnp.float32
VMEM_LIMIT = 56 * 1024 * 1024


def _params(*sem):
    return pltpu.CompilerParams(dimension_semantics=sem, vmem_limit_bytes=VMEM_LIMIT)


def _dot(a, b):
    return jnp.dot(a, b, preferred_element_type=F32)


def _dot_nt(a, b):
    return lax.dot_general(a, b, (((1,), (1,)), ((), ())), preferred_element_type=F32)


def _dot_tn(a, b):
    return lax.dot_general(a, b, (((0,), (0,)), ((), ())), preferred_element_type=F32)


def _rms(x, g):
    return x * lax.rsqrt(jnp.mean(x * x, axis=-1, keepdims=True) + NORM_EPS) * g


def _row_tile(s, want):
    return min(want, s)


def _full(shape):
    return pl.BlockSpec(shape, lambda *_: (0,) * len(shape))


def _rope_table_kernel(pos_ref, freq_ref, cos_ref, sin_ref):
    ang = pos_ref[...] * freq_ref[...]
    cos_ref[...] = jnp.cos(ang)
    sin_ref[...] = jnp.sin(ang)


def _rope_tables(pos_col, freq_row):
    s = pos_col.shape[0]
    tm = _row_tile(s, 1024)
    return pl.pallas_call(
        _rope_table_kernel,
        grid=(s // tm,),
        in_specs=[pl.BlockSpec((tm, 1), lambda i: (i, 0)), _full((1, LANES))],
        out_specs=[pl.BlockSpec((tm, LANES), lambda i: (i, 0))] * 2,
        out_shape=[jax.ShapeDtypeStruct((s, LANES), F32)] * 2,
        compiler_params=_params("parallel"),
        name="rope_tables",
    )(pos_col, freq_row)


def _mla_pre_kernel(x_ref, g_ref, w_in_ref, qn_ref, w_uq_ref, kvn_ref, w_ukv_ref,
                    cos_ref, sin_ref, q_ref, k_ref, v_ref):
    h = _rms(x_ref[...], g_ref[...]).astype(BF16)
    lat = _dot(h, w_in_ref[...])
    q_lat = lat[:, :Q_LORA_RANK]
    c_kv = lat[:, Q_LORA_RANK:Q_LORA_RANK + KV_LORA_RANK]
    kr = lat[:, Q_LORA_RANK + KV_LORA_RANK:]
    cos = cos_ref[...]
    sin = sin_ref[...]
    k_rope = (kr[:, :QK_ROPE_DIM] * cos[:, :QK_ROPE_DIM]
              + kr[:, QK_ROPE_DIM:] * sin[:, :QK_ROPE_DIM])
    qn = _rms(q_lat, qn_ref[...]).astype(BF16)
    q_all = _dot(qn, w_uq_ref[...])
    n_nope = MLA_HEADS * QK_NOPE_DIM
    n_rope = MLA_HEADS * QK_ROPE_DIM
    reps = n_rope // LANES
    qscale = QK_DIM ** -0.5 * math.log2(math.e)
    q_rope = (q_all[:, n_nope:n_nope + n_rope] * jnp.tile(cos, (1, reps))
              + q_all[:, n_nope + n_rope:] * jnp.tile(sin, (1, reps))) * qscale
    kvn = _rms(c_kv, kvn_ref[...]).astype(BF16)
    kv = _dot(kvn, w_ukv_ref[...])
    tm = x_ref.shape[0]
    qn_t = (q_all[:, :n_nope] * qscale).T.astype(BF16)
    qr_t = q_rope.T.astype(BF16)
    v_t = kv[:, n_nope:].T.astype(BF16)
    zpad_t = jnp.zeros((QK_PAD - QK_DIM, tm), BF16)
    ones_t = jnp.ones((V_ROWS - V_HEAD_DIM, tm), BF16)
    zpad = jnp.zeros((tm, QK_PAD - QK_DIM), BF16)
    k_rope_b = k_rope.astype(BF16)
    for hd in range(MLA_HEADS):
        q_ref[hd, 0:QK_NOPE_DIM, :] = qn_t[hd * QK_NOPE_DIM:(hd + 1) * QK_NOPE_DIM]
        q_ref[hd, QK_NOPE_DIM:QK_DIM, :] = qr_t[hd * QK_ROPE_DIM:(hd + 1) * QK_ROPE_DIM]
        q_ref[hd, QK_DIM:QK_PAD, :] = zpad_t
        k_ref[hd, :, 0:QK_NOPE_DIM] = kv[:, hd * QK_NOPE_DIM:(hd + 1) * QK_NOPE_DIM].astype(BF16)
        k_ref[hd, :, QK_NOPE_DIM:QK_DIM] = k_rope_b
        k_ref[hd, :, QK_DIM:QK_PAD] = zpad
        v_ref[hd, 0, 0:V_HEAD_DIM, :] = v_t[hd * V_HEAD_DIM:(hd + 1) * V_HEAD_DIM]
        v_ref[hd, 0, V_HEAD_DIM:V_ROWS, :] = ones_t


def _mla_pre(x, g, w_in_e, q_norm, w_uq_e, kv_norm, w_ukv_p, cos_t, sin_t):
    s, d = x.shape
    tm = _row_tile(s, ATTN_TK)
    row = lambda w: pl.BlockSpec((tm, w), lambda i: (i, 0))
    return pl.pallas_call(
        _mla_pre_kernel,
        grid=(s // tm,),
        in_specs=[row(d), _full(g.shape), _full(w_in_e.shape), _full(q_norm.shape),
                  _full(w_uq_e.shape), _full(kv_norm.shape), _full(w_ukv_p.shape),
                  row(LANES), row(LANES)],
        out_specs=[pl.BlockSpec((MLA_HEADS, QK_PAD, tm), lambda i: (0, 0, i)),
                   pl.BlockSpec((MLA_HEADS, tm, QK_PAD), lambda i: (0, i, 0)),
                   pl.BlockSpec((MLA_HEADS, 1, V_ROWS, tm), lambda i: (0, i, 0, 0))],
        out_shape=[jax.ShapeDtypeStruct((MLA_HEADS, QK_PAD, s), BF16),
                   jax.ShapeDtypeStruct((MLA_HEADS, s, QK_PAD), BF16),
                   jax.ShapeDtypeStruct((MLA_HEADS, s // tm, V_ROWS, tm), BF16)],
        compiler_params=_params("parallel"),
        name="mla_pre",
    )(x, g, w_in_e, q_norm, w_uq_e, kv_norm, w_ukv_p, cos_t, sin_t)


def _attn_kernel(q_ref, k_ref, v_ref, o_ref, *scratch, tk, n_chains):
    qi = pl.program_id(1)
    accs = scratch[:n_chains]
    sbuf = [scratch[n_chains + 2 * c:n_chains + 2 * c + 2] for c in range(n_chains)]
    for acc in accs:
        acc[...] = jnp.zeros(acc.shape, F32)

    def scores(task):
        c, j, par, _ = task
        q_t = q_ref[:, c * tk:(c + 1) * tk]
        off = pl.multiple_of(j * tk, tk)
        s = _dot(k_ref[pl.ds(off, tk), :], q_t)
        sbuf[c][par][...] = s
        return jnp.max(s, axis=0, keepdims=True)

    def absorb(task, m_blk, m_old):
        c, j, par, masked = task
        s = sbuf[c][par][...]
        if masked:
            key = lax.broadcasted_iota(jnp.int32, s.shape, 0)
            qry = lax.broadcasted_iota(jnp.int32, s.shape, 1)
            s = jnp.where(key <= qry, s, jnp.finfo(F32).min)
            m_blk = jnp.max(s, axis=0, keepdims=True)
        m_new = jnp.maximum(m_old, m_blk)
        p = jnp.exp2(s - m_new)
        alpha = jnp.exp2(m_old - m_new)
        accs[c][...] = alpha * accs[c][...] + _dot(v_ref[j], p.astype(BF16))
        return m_new

    def run(tasks, lookahead, ms, pending):
        ms = list(ms)
        for i, task in enumerate(tasks):
            nxt = tasks[i + 1] if i + 1 < len(tasks) else lookahead
            staged = scores(nxt) if nxt is not None else None
            ms[task[0]] = absorb(task, pending, ms[task[0]])
            pending = staged
        return ms, pending

    chains = range(n_chains)
    n_full = n_chains * qi

    def body(jj, carry):
        j = ATTN_UNROLL * jj
        tasks = [(c, j + b, b % 2, False) for b in range(ATTN_UNROLL) for c in chains]
        ms, pending = run(tasks, (0, j + ATTN_UNROLL, 0, False), carry[:-1], carry[-1])
        return (*ms, pending)

    m0 = jnp.full((1, tk), -1e30, F32)
    carry = lax.fori_loop(0, n_full // ATTN_UNROLL, body,
                          (*[m0] * n_chains, scores((0, 0, 0, False))))
    tail = [(c, n_full + kb, kb % 2, c == kb) for kb in chains for c in range(kb, n_chains)]
    run(tail, None, carry[:-1], carry[-1])
    for c, acc in enumerate(accs):
        out = acc[0:V_HEAD_DIM, :] / acc[V_HEAD_DIM:V_HEAD_DIM + 1, :]
        o_ref[c * tk:(c + 1) * tk, :] = out.T.astype(o_ref.dtype)


def _attention(q_t, k_cat, v_t):
    nh, s, _ = k_cat.shape
    tk = v_t.shape[-1]
    n_chains = min(ATTN_CHAINS, s // tk)
    assert n_chains % ATTN_UNROLL == 0 and ATTN_UNROLL % 2 == 0
    tq = n_chains * tk
    return pl.pallas_call(
        functools.partial(_attn_kernel, tk=tk, n_chains=n_chains),
        grid=(nh, s // tq),
        in_specs=[pl.BlockSpec((None, QK_PAD, tq), lambda h, i: (h, 0, i)),
                  pl.BlockSpec((None, s, QK_PAD), lambda h, i: (h, 0, 0)),
                  pl.BlockSpec((None, s // tk, V_ROWS, tk), lambda h, i: (h, 0, 0, 0))],
        out_specs=pl.BlockSpec((tq, V_HEAD_DIM), lambda h, i: (i, h)),
        out_shape=jax.ShapeDtypeStruct((s, nh * V_HEAD_DIM), BF16),
        scratch_shapes=([pltpu.VMEM((V_ROWS, tk), F32)] * n_chains
                        + [pltpu.VMEM((tk, tk), F32)] * (2 * n_chains)),
        compiler_params=_params("parallel", "arbitrary"),
        name="mla_attention",
    )(q_t, k_cat, v_t)


FFN_CHUNKS = ((0, 1024), (1024, 2048), (2048, D_FF))


def _mix_ffn_kernel(*refs, gated, final):
    refs = list(refs)
    o_ref = refs.pop()
    fg_ref = refs.pop() if final else None
    x_ref, xp_ref, z_ref, zp_ref = refs[:4]
    gt_ref, gtp_ref = refs[4:6] if gated else (None, None)
    wo_ref, g_ref, wi_ref, cw_ref, cb_ref, wout_ref = refs[-6:]
    i = pl.program_id(0)

    def mixed(xr, zr, gr):
        z = zr[...]
        if gated:
            z = z * gr[...]
        return xr[...] + _dot(z.astype(BF16), wo_ref[...])

    x1 = mixed(x_ref, z_ref, gt_ref)
    x1p = mixed(xp_ref, zp_ref, gtp_ref)
    keep = (i > 0).astype(F32)
    hn = jnp.concatenate([(_rms(x1p, g_ref[...]) * keep).astype(BF16),
                          _rms(x1, g_ref[...]).astype(BF16)], axis=0)
    def gate_up(lo, hi):
        return _dot(hn, wi_ref[:, lo:hi]), _dot(hn[HALO:], wi_ref[:, D_FF + lo:D_FF + hi])

    acc = x1
    nxt = gate_up(*FFN_CHUNKS[0])
    for n, (lo, hi) in enumerate(FFN_CHUNKS):
        gate, up = nxt
        if n + 1 < len(FFN_CHUNKS):
            nxt = gate_up(*FFN_CHUNKS[n + 1])
        conv = (gate[HALO:] * cw_ref[2:3, lo:hi]
                + pltpu.roll(gate, 1, 0)[HALO:] * cw_ref[1:2, lo:hi]
                + pltpu.roll(gate, 2, 0)[HALO:] * cw_ref[0:1, lo:hi] + cb_ref[:, lo:hi])
        act = (conv * jax.nn.sigmoid(conv) * up).astype(BF16)
        acc = acc + _dot(act, wout_ref[lo:hi, :])
    o_ref[...] = _rms(acc, fg_ref[...]) if final else acc


def _mix_ffn(x, z, gate, w_o, g, layer, w_in_all, conv_w, conv_b, w_out_all, final_gain=None):
    s, d = x.shape
    tm = _row_tile(s, 512)
    row = pl.BlockSpec((tm, d), lambda i: (i, 0))
    prev = pl.BlockSpec((HALO, d), lambda i: (jnp.maximum(i * (tm // HALO) - 1, 0), 0))

    def const(a, stacked=False):
        if stacked:
            return pl.BlockSpec((None,) + a.shape[1:], lambda i: (layer,) + (0,) * (a.ndim - 1),
                                pipeline_mode=pl.Buffered(1))
        return pl.BlockSpec(a.shape, lambda i: (0,) * a.ndim, pipeline_mode=pl.Buffered(1))

    gated = gate is not None
    final = final_gain is not None
    consts = [w_o, g, w_in_all, conv_w, conv_b, w_out_all] + ([final_gain] if final else [])
    args = [x, x, z, z] + ([gate, gate] if gated else []) + consts
    specs = ([row, prev, row, prev] + ([row, prev] if gated else [])
             + [const(a, stacked=a is w_in_all or a is w_out_all) for a in consts])
    return pl.pallas_call(
        functools.partial(_mix_ffn_kernel, gated=gated, final=final),
        grid=(s // tm,),
        in_specs=specs,
        out_specs=row,
        out_shape=jax.ShapeDtypeStruct((s, d), F32),
        compiler_params=_params("parallel"),
        name="mix_ffn",
    )(*args)


def _rwkv_pre_kernel(*refs, has_vres):
    (x_ref, xp_ref, g_ref, mix_ref, wrkv_ref, w0_ref, w1_ref, w2_ref, a0_ref, a1_ref, a2_ref,
     g1_ref, g2_ref) = refs[:13]
    if has_vres:
        vf_ref, v0_ref, v1_ref, v2_ref = refs[13:17]
    r_o, k_o, v_o, a_o, cl_o, g_o = refs[-6:]
    i = pl.program_id(0)
    tm = x_ref.shape[0]
    h = _rms(x_ref[...], g_ref[...])
    hp = _rms(xp_ref[...], g_ref[...])[HALO - 1:HALO] * (i > 0).astype(F32)
    row = lax.broadcasted_iota(jnp.int32, h.shape, 0)
    shifted = jnp.where(row == 0, jnp.broadcast_to(hp, h.shape), pltpu.roll(h, 1, 0))
    xx = shifted - h
    mix = mix_ref[...]
    xm = lambda n: (h + xx * mix[n:n + 1]).astype(BF16)
    xv = xm(2)
    r = _dot(xm(0), wrkv_ref[0])
    k = _dot(xm(1), wrkv_ref[1])
    v = _dot(xv, wrkv_ref[2])
    lora_w = _dot(jnp.tanh(_dot(xm(3), w1_ref[...])).astype(BF16), w2_ref[...])
    lw = -math.exp(-0.5) * jax.nn.sigmoid(w0_ref[...] + lora_w)
    a_o[...] = jax.nn.sigmoid(a0_ref[...] + _dot(_dot(xm(4), a1_ref[...]).astype(BF16), a2_ref[...]))
    g_o[...] = _dot(jax.nn.sigmoid(_dot(xm(5), g1_ref[...])).astype(BF16), g2_ref[...])
    if has_vres:
        gate_v = jax.nn.sigmoid(v0_ref[...] + _dot(_dot(xv, v1_ref[...]).astype(BF16), v2_ref[...]))
        v = v + (vf_ref[...] - v) * gate_v
    r_o[...] = r
    k_o[...] = k
    v_o[...] = v
    C = WKV_CHUNK
    rr = lax.broadcasted_iota(jnp.int32, (tm, tm), 0)
    cc = lax.broadcasted_iota(jnp.int32, (tm, tm), 1)
    tri = (((rr // C) == (cc // C)) & (cc <= rr)).astype(BF16)
    hi = lw.astype(BF16)
    rem = lw - hi.astype(F32)
    mid = rem.astype(BF16)
    lo = (rem - mid.astype(F32)).astype(BF16)
    cl_o[...] = _dot(tri, hi) + _dot(tri, mid) + _dot(tri, lo)


def _rwkv_pre(x, g, p, v_first):
    s, d = x.shape
    tm = _row_tile(s, WKV_TILE)
    row = pl.BlockSpec((tm, d), lambda i: (i, 0))
    prev = pl.BlockSpec((HALO, d), lambda i: (jnp.maximum(i * (tm // HALO) - 1, 0), 0))
    names = ["mix", "w_rkv", "w0", "w1", "w2", "a0", "a1", "a2", "g1", "g2"]
    args = [x, x, g] + [p[n] for n in names]
    specs = [row, prev, _full(g.shape)] + [_full(p[n].shape) for n in names]
    has_vres = v_first is not None
    if has_vres:
        args += [v_first, p["v0"], p["v1"], p["v2"]]
        specs += [row] + [_full(p[n].shape) for n in ("v0", "v1", "v2")]
    n_out = 6
    out_specs = [row] * n_out
    out_shape = [jax.ShapeDtypeStruct((s, d), F32)] * n_out
    return pl.pallas_call(
        functools.partial(_rwkv_pre_kernel, has_vres=has_vres),
        grid=(s // tm,),
        in_specs=specs,
        out_specs=out_specs,
        out_shape=out_shape,
        compiler_params=_params("parallel"),
        name="rwkv_pre",
    )(*args)


def _wkv_kernel(r_ref, k_ref, v_ref, a_ref, cl_ref, kk_ref, ka_ref, rk_ref,
                lnw_ref, lnb_ref, o_ref, s_ref, *, group):
    C = WKV_CHUNK
    C2 = 2 * C
    n_pairs = o_ref.shape[1] // LANES
    n_groups = o_ref.shape[0] // (C * group)

    @pl.when(pl.program_id(0) == 0)
    def _():
        s_ref[...] = jnp.zeros(s_ref.shape, F32)

    lane = lax.broadcasted_iota(jnp.int32, (1, LANES), 1)
    head0 = lane < RWKV_HEAD
    m0 = head0.astype(F32)
    m1 = 1.0 - m0
    t2 = lax.broadcasted_iota(jnp.int32, (C2, C2), 0)
    s2 = lax.broadcasted_iota(jnp.int32, (C2, C2), 1)
    same_head = (t2 // C) == (s2 // C)
    strict = (same_head & (s2 < t2)).astype(F32)
    incl = (same_head & (s2 <= t2)).astype(F32)
    eye = (t2 == s2).astype(F32)

    def level_mask(b):
        return (((t2 // (2 * b)) == (s2 // (2 * b))) & ((t2 % (2 * b)) >= b)
                & ((s2 % (2 * b)) < b)).astype(F32)

    levels = []
    b_sz = 1
    while b_sz < C:
        levels.append(level_mask(b_sz))
        b_sz *= 2

    def stack(x):
        zero = jnp.zeros_like(x)
        return jnp.concatenate([jnp.where(head0, x, zero), jnp.where(head0, zero, x)], axis=0)

    def twice(x):
        return jnp.concatenate([x, x], axis=0)

    def cast(x):
        return x.astype(BF16)

    first_row = lax.broadcasted_iota(jnp.int32, (C, LANES), 0) == 0

    def head_sum(x):
        return (jnp.sum(x * m0, axis=-1, keepdims=True) * m0
                + jnp.sum(x * m1, axis=-1, keepdims=True) * m1)

    def group_body(g, carry):
        probs = [(ci, p) for ci in range(group) for p in range(n_pairs)]
        offs = [pl.multiple_of((g * group + ci) * C, C) for ci in range(group)]

        def load(ref, ci, p):
            return ref[pl.ds(offs[ci], C), p * LANES:(p + 1) * LANES]

        cl = [load(cl_ref, ci, p) for ci, p in probs]
        cl_end = [c[C - 1:C] for c in cl]
        cl_ex = [jnp.where(first_row, 0.0, pltpu.roll(c, 1, 0)) for c in cl]
        e_neg = [jnp.exp(-c) for c in cl]
        e_rem = [jnp.exp(ce - c) for c, ce in zip(cl, cl_end)]
        w_end = [jnp.exp(ce) for ce in cl_end]
        lanes = [slice(p * LANES, (p + 1) * LANES) for _, p in probs]
        r = [load(r_ref, ci, p) for ci, p in probs]
        k = [load(k_ref, ci, p) for ci, p in probs]
        v = [load(v_ref, ci, p) for ci, p in probs]
        a = [load(a_ref, ci, p) for ci, p in probs]
        kk = [x * kk_ref[:, ln] for x, ln in zip(k, lanes)]
        kk = [x * lax.rsqrt(jnp.maximum(head_sum(x * x), 1e-24)) for x in kk]
        kn = [x * (1.0 + (aa - 1.0) * ka_ref[:, ln]) for x, aa, ln in zip(k, a, lanes)]
        bonus = [head_sum(rr * x * rk_ref[:, ln]) * vv for rr, x, vv, ln in zip(r, kn, v, lanes)]
        bn = [x * aa for x, aa in zip(kk, a)]
        a_s = [stack(cast(-x * jnp.exp(ce))) for x, ce in zip(kk, cl_ex)]
        r_s = [stack(cast(rr * jnp.exp(c))) for rr, c in zip(r, cl)]
        v_s = [stack(cast(vv)) for vv in v]
        hat = [jnp.concatenate([stack(cast(b * e)), stack(cast(k * e))], axis=0)
               for b, k, e in zip(bn, kn, e_rem)]
        til = [jnp.concatenate([twice(cast(b * e)), twice(cast(k * e))], axis=0)
               for b, k, e in zip(bn, kn, e_neg)]
        lm = [_dot_nt(jnp.concatenate([a, r], axis=0), t) for a, r, t in zip(a_s, r_s, til)]
        l_ab = [x[:C2, :C2] * strict for x in lm]
        l_ak = [cast(x[:C2, C2:] * strict) for x in lm]
        m_cat = [cast(jnp.concatenate([x[C2:, :C2] * incl, x[C2:, C2:] * incl], axis=1))
                 for x in lm]
        t_inv = [eye + l * levels[0] for l in l_ab]
        for lvl in levels[1:]:
            t_b = [cast(t) for t in t_inv]
            z = [_dot(t, cast(l * lvl)) for t, l in zip(t_b, l_ab)]
            t_inv = [t + _dot(cast(zz), tb) for t, zz, tb in zip(t_inv, z, t_b)]
        t_b = [cast(t) for t in t_inv]
        p_loc = [_dot(l, v) for l, v in zip(l_ak, v_s)]
        for ci in range(group):
            rows = pl.ds(offs[ci], C)
            sel = [ci * n_pairs + p for p in range(n_pairs)]
            st = [s_ref[p] for p in range(n_pairs)]
            st_b = [cast(s) for s in st]
            x = [_dot_nt(a_s[i], sb) + p_loc[i] for i, sb in zip(sel, st_b)]
            u = [cast(_dot(t_b[i], cast(xx))) for i, xx in zip(sel, x)]
            uv = [jnp.concatenate([uu, v_s[i]], axis=0) for i, uu in zip(sel, u)]
            y = [_dot_nt(r_s[i], sb) + _dot(m_cat[i], w) for i, sb, w in zip(sel, st_b, uv)]
            for p in range(n_pairs):
                cols = slice(p * LANES, (p + 1) * LANES)
                s_ref[p] = st[p] * w_end[sel[p]] + _dot_tn(uv[p], hat[sel[p]])
                yp = y[p][:C] + y[p][C:]
                yc = yp - head_sum(yp) * (1.0 / RWKV_HEAD)
                var = head_sum(yc * yc) * (1.0 / RWKV_HEAD)
                yn = yc * lax.rsqrt(var + GN_EPS)
                o_ref[rows, cols] = yn * lnw_ref[:, cols] + lnb_ref[:, cols] + bonus[sel[p]]
        return carry

    lax.fori_loop(0, n_groups, group_body, 0)


def _wkv(r, k, v, a, cl, k_k, k_a, r_k, ln_w, ln_b):
    s, d = r.shape
    tt = _row_tile(s, WKV_STEP)
    group = 2
    row = pl.BlockSpec((tt, d), lambda t: (t, 0))
    return pl.pallas_call(
        functools.partial(_wkv_kernel, group=group),
        grid=(s // tt,),
        in_specs=[row] * 5 + [_full(ln_w.shape)] * 5,
        out_specs=row,
        out_shape=jax.ShapeDtypeStruct((s, d), F32),
        scratch_shapes=[pltpu.VMEM((d // LANES, LANES, LANES), F32)],
        compiler_params=_params("arbitrary"),
        name="wkv7_scan",
    )(r, k, v, a, cl, k_k, k_a, r_k, ln_w, ln_b)


def _rot_cols(w):
    half = QK_ROPE_DIM // 2
    return jnp.concatenate([-w[..., half:], w[..., :half]], axis=-1)


def _mla_weights(w_in, w_uq, w_ukv):
    d = w_in.shape[0]
    lat = Q_LORA_RANK + KV_LORA_RANK
    w_kr = w_in[:, lat:]
    w_in_e = jnp.concatenate([w_in[:, :lat], w_kr, _rot_cols(w_kr)], axis=1).astype(BF16)
    uq = w_uq.reshape(Q_LORA_RANK, MLA_HEADS, QK_DIM)
    uq_rope = uq[:, :, QK_NOPE_DIM:]
    w_uq_e = jnp.concatenate(
        [uq[:, :, :QK_NOPE_DIM].reshape(Q_LORA_RANK, -1),
         uq_rope.reshape(Q_LORA_RANK, -1),
         _rot_cols(uq_rope).reshape(Q_LORA_RANK, -1)], axis=1).astype(BF16)
    ukv = w_ukv.reshape(KV_LORA_RANK, MLA_HEADS, QK_NOPE_DIM + V_HEAD_DIM)
    w_ukv_p = jnp.concatenate(
        [ukv[:, :, :QK_NOPE_DIM].reshape(KV_LORA_RANK, -1),
         ukv[:, :, QK_NOPE_DIM:].reshape(KV_LORA_RANK, -1)], axis=1).astype(BF16)
    del d
    return w_in_e, w_uq_e, w_ukv_p


def _mla_layer(x, g, cos_t, sin_t, w_in, q_norm, w_uq, kv_norm, w_ukv):
    w_in_e, w_uq_e, w_ukv_p = _mla_weights(w_in, w_uq, w_ukv)
    q_cat, k_cat, v = _mla_pre(x, g[None], w_in_e, q_norm[None], w_uq_e, kv_norm[None],
                               w_ukv_p, cos_t, sin_t)
    return _attention(q_cat, k_cat, v)


def _rwkv_layer(x, g, v_first, p):
    row = lambda t: t.reshape(1, -1)
    pp = {
        "mix": p["mix"], "w_rkv": p["w_rkv"].astype(BF16),
        "w0": row(p["w0"]), "w1": p["w1"].astype(BF16), "w2": p["w2"].astype(BF16),
        "a0": row(p["a0"]), "a1": p["a1"].astype(BF16), "a2": p["a2"].astype(BF16),
        "g1": p["g1"].astype(BF16), "g2": p["g2"].astype(BF16),
    }
    if v_first is not None:
        pp.update(v0=row(p["v0"]), v1=p["v1"].astype(BF16), v2=p["v2"].astype(BF16))
    outs = _rwkv_pre(x, g[None], pp, v_first)
    scan_in, gate = outs[:5], outs[5]
    z = _wkv(*scan_in, *(row(p[n]) for n in ("k_k", "k_a", "r_k", "ln_w", "ln_b")))
    return z, gate, (outs[2] if v_first is None else v_first)


def _mix_ffn_layer(x, z, gate, w_o, g, layer, w_in_all, conv_w, conv_b, w_out_all, final_gain=None):
    return _mix_ffn(x, z, gate, w_o.astype(BF16), g[None], layer, w_in_all, conv_w, conv_b[None],
                    w_out_all, None if final_gain is None else final_gain[None])


def kernel(x, positions, norm_mix, norm_ffn, norm_final, mla_w_in, mla_q_norm, mla_w_uq, mla_kv_norm, mla_w_ukv, mla_w_o, rwkv_mix, rwkv_w_rkv, rwkv_w0, rwkv_w1, rwkv_w2, rwkv_a0, rwkv_a1, rwkv_a2, rwkv_v0, rwkv_v1, rwkv_v2, rwkv_g1, rwkv_g2, rwkv_k_k, rwkv_k_a, rwkv_r_k, rwkv_ln_w, rwkv_ln_b, rwkv_w_o, ffn_w_in, ffn_conv_w, ffn_conv_b, ffn_w_out):
    batch, seq, d = x.shape
    assert batch == 1
    depth = norm_mix.shape[0]
    half = QK_ROPE_DIM // 2
    inv_freq = ROPE_THETA ** (-jnp.arange(half, dtype=F32) / half)
    freq_row = jnp.tile(inv_freq, LANES // half)[None]
    cos_t, sin_t = _rope_tables(positions.astype(F32).reshape(seq, 1), freq_row)
    xs = x.reshape(seq, d)
    ffn_w_in_b = ffn_w_in.astype(BF16)
    ffn_w_out_b = ffn_w_out.astype(BF16)
    v_first = None
    for i in range(depth):
        j = i // 2
        if i % 2 == 0:
            z = _mla_layer(xs, norm_mix[i], cos_t, sin_t, mla_w_in[j], mla_q_norm[j], mla_w_uq[j],
                           mla_kv_norm[j], mla_w_ukv[j])
            gate, w_o = None, mla_w_o[j]
        else:
            p = {"mix": rwkv_mix[j], "w_rkv": rwkv_w_rkv[j], "w0": rwkv_w0[j], "w1": rwkv_w1[j],
                 "w2": rwkv_w2[j], "a0": rwkv_a0[j], "a1": rwkv_a1[j], "a2": rwkv_a2[j],
                 "g1": rwkv_g1[j], "g2": rwkv_g2[j], "k_k": rwkv_k_k[j], "k_a": rwkv_k_a[j],
                 "r_k": rwkv_r_k[j], "ln_w": rwkv_ln_w[j], "ln_b": rwkv_ln_b[j]}
            if j > 0:
                p.update(v0=rwkv_v0[j - 1], v1=rwkv_v1[j - 1], v2=rwkv_v2[j - 1])
            z, gate, v_first = _rwkv_layer(xs, norm_mix[i], v_first, p)
            w_o = rwkv_w_o[j]
        xs = _mix_ffn_layer(xs, z, gate, w_o, norm_ffn[i], i, ffn_w_in_b, ffn_conv_w[i], ffn_conv_b[i],
                            ffn_w_out_b, norm_final if i == depth - 1 else None)
    return xs.reshape(batch, seq, d)
```

```python
import functools
import math

import jax
import jax.numpy as jnp
from jax import lax
from jax.experimental import pallas as pl
from jax.experimental.pallas import tpu as pltpu

NORM_EPS = 1e-6
GN_EPS = 64e-5
MLA_HEADS = 8
QK_NOPE_DIM = 128
QK_ROPE_DIM = 64
QK_DIM = QK_NOPE_DIM + QK_ROPE_DIM
QK_PAD = 256
V_HEAD_DIM = 128
V_ROWS = V_HEAD_DIM + 16
Q_LORA_RANK = 384
KV_LORA_RANK = 256
ROPE_THETA = 10000.0
RWKV_HEAD = 64
LANES = 128
D_FF = 2816
HALO = 16
WKV_CHUNK = 64
WKV_TILE = 256
WKV_STEP = 512
ATTN_TK = 512
ATTN_CHAINS = 8
ATTN_UNROLL = 4

BF16 = jnp.bfloat16
F32 = jnp.float32
VMEM_LIMIT = 56 * 1024 * 1024


def _params(*sem):
    return pltpu.CompilerParams(dimension_semantics=sem, vmem_limit_bytes=VMEM_LIMIT)


def _dot(a, b):
    return jnp.dot(a, b, preferred_element_type=F32)


def _dot_nt(a, b):
    return lax.dot_general(a, b, (((1,), (1,)), ((), ())), preferred_element_type=F32)


def _dot_tn(a, b):
    return lax.dot_general(a, b, (((0,), (0,)), ((), ())), preferred_element_type=F32)


def _rms(x, g):
    return x * lax.rsqrt(jnp.mean(x * x, axis=-1, keepdims=True) + NORM_EPS) * g


def _row_tile(s, want):
    return min(want, s)


def _full(shape):
    return pl.BlockSpec(shape, lambda *_: (0,) * len(shape))


ROPE_PACK = LANES // (QK_ROPE_DIM // 2)


def _rope_table_kernel(pos_ref, freq_ref, cos_ref, sin_ref):
    ang = pos_ref[...] * freq_ref[...]
    cos, sin = jnp.cos(ang), jnp.sin(ang)
    half = QK_ROPE_DIM // 2
    for j in range(ROPE_PACK):
        cos_ref[j] = jnp.tile(cos[:, j * half:(j + 1) * half], (1, ROPE_PACK))
        sin_ref[j] = jnp.tile(sin[:, j * half:(j + 1) * half], (1, ROPE_PACK))


def _rope_tables(pos_col, freq_row):
    s = pos_col.shape[0]
    rows = s // ROPE_PACK
    pos_packed = jnp.repeat(pos_col.reshape(ROPE_PACK, rows).T, QK_ROPE_DIM // 2, axis=1)
    tm = _row_tile(rows, 256)
    out = pl.BlockSpec((ROPE_PACK, tm, LANES), lambda i: (0, i, 0))
    cos_t, sin_t = pl.pallas_call(
        _rope_table_kernel,
        grid=(rows // tm,),
        in_specs=[pl.BlockSpec((tm, LANES), lambda i: (i, 0)), _full((1, LANES))],
        out_specs=[out] * 2,
        out_shape=[jax.ShapeDtypeStruct((ROPE_PACK, rows, LANES), F32)] * 2,
        compiler_params=_params("parallel"),
        name="rope_tables",
    )(pos_packed, freq_row)
    return cos_t.reshape(s, LANES), sin_t.reshape(s, LANES)


def _mla_pre_kernel(x_ref, g_ref, w_in_ref, qn_ref, w_uq_ref, kvn_ref, w_ukv_ref,
                    cos_ref, sin_ref, q_ref, k_ref, v_ref):
    h = _rms(x_ref[...], g_ref[...]).astype(BF16)
    lat = _dot(h, w_in_ref[...])
    q_lat = lat[:, :Q_LORA_RANK]
    c_kv = lat[:, Q_LORA_RANK:Q_LORA_RANK + KV_LORA_RANK]
    kr = lat[:, Q_LORA_RANK + KV_LORA_RANK:]
    cos = cos_ref[...]
    sin = sin_ref[...]
    k_rope = (kr[:, :QK_ROPE_DIM] * cos[:, :QK_ROPE_DIM]
              + kr[:, QK_ROPE_DIM:] * sin[:, :QK_ROPE_DIM])
    qn = _rms(q_lat, qn_ref[...]).astype(BF16)
    q_all = _dot(qn, w_uq_ref[...])
    n_nope = MLA_HEADS * QK_NOPE_DIM
    n_rope = MLA_HEADS * QK_ROPE_DIM
    reps = n_rope // LANES
    qscale = QK_DIM ** -0.5 * math.log2(math.e)
    q_rope = (q_all[:, n_nope:n_nope + n_rope] * jnp.tile(cos, (1, reps))
              + q_all[:, n_nope + n_rope:] * jnp.tile(sin, (1, reps))) * qscale
    kvn = _rms(c_kv, kvn_ref[...]).astype(BF16)
    kv = _dot(kvn, w_ukv_ref[...])
    tm = x_ref.shape[0]
    qn_t = (q_all[:, :n_nope] * qscale).T.astype(BF16)
    qr_t = q_rope.T.astype(BF16)
    v_t = kv[:, n_nope:].T.astype(BF16)
    zpad_t = jnp.zeros((QK_PAD - QK_DIM, tm), BF16)
    ones_t = jnp.ones((V_ROWS - V_HEAD_DIM, tm), BF16)
    zpad = jnp.zeros((tm, QK_PAD - QK_DIM), BF16)
    k_rope_b = k_rope.astype(BF16)
    for hd in range(MLA_HEADS):
        q_ref[hd, 0:QK_NOPE_DIM, :] = qn_t[hd * QK_NOPE_DIM:(hd + 1) * QK_NOPE_DIM]
        q_ref[hd, QK_NOPE_DIM:QK_DIM, :] = qr_t[hd * QK_ROPE_DIM:(hd + 1) * QK_ROPE_DIM]
        q_ref[hd, QK_DIM:QK_PAD, :] = zpad_t
        k_ref[hd, :, 0:QK_NOPE_DIM] = kv[:, hd * QK_NOPE_DIM:(hd + 1) * QK_NOPE_DIM].astype(BF16)
        k_ref[hd, :, QK_NOPE_DIM:QK_DIM] = k_rope_b
        k_ref[hd, :, QK_DIM:QK_PAD] = zpad
        v_ref[hd, 0, 0:V_HEAD_DIM, :] = v_t[hd * V_HEAD_DIM:(hd + 1) * V_HEAD_DIM]
        v_ref[hd, 0, V_HEAD_DIM:V_ROWS, :] = ones_t


def _mla_pre(x, g, w_in_e, q_norm, w_uq_e, kv_norm, w_ukv_p, cos_t, sin_t):
    s, d = x.shape
    tm = _row_tile(s, ATTN_TK)
    row = lambda w: pl.BlockSpec((tm, w), lambda i: (i, 0))
    return pl.pallas_call(
        _mla_pre_kernel,
        grid=(s // tm,),
        in_specs=[row(d), _full(g.shape), _full(w_in_e.shape), _full(q_norm.shape),
                  _full(w_uq_e.shape), _full(kv_norm.shape), _full(w_ukv_p.shape),
                  row(LANES), row(LANES)],
        out_specs=[pl.BlockSpec((MLA_HEADS, QK_PAD, tm), lambda i: (0, 0, i)),
                   pl.BlockSpec((MLA_HEADS, tm, QK_PAD), lambda i: (0, i, 0)),
                   pl.BlockSpec((MLA_HEADS, 1, V_ROWS, tm), lambda i: (0, i, 0, 0))],
        out_shape=[jax.ShapeDtypeStruct((MLA_HEADS, QK_PAD, s), BF16),
                   jax.ShapeDtypeStruct((MLA_HEADS, s, QK_PAD), BF16),
                   jax.ShapeDtypeStruct((MLA_HEADS, s // tm, V_ROWS, tm), BF16)],
        compiler_params=_params("parallel"),
        name="mla_pre",
    )(x, g, w_in_e, q_norm, w_uq_e, kv_norm, w_ukv_p, cos_t, sin_t)


def _attn_kernel(q_ref, k_ref, v_ref, o_ref, *scratch, tk, n_chains):
    qi = pl.program_id(1)
    accs = scratch[:n_chains]
    sbuf = [scratch[n_chains + 2 * c:n_chains + 2 * c + 2] for c in range(n_chains)]
    for acc in accs:
        acc[...] = jnp.zeros(acc.shape, F32)

    def scores(task):
        c, j, par, _ = task
        q_t = q_ref[:, c * tk:(c + 1) * tk]
        off = pl.multiple_of(j * tk, tk)
        s = _dot(k_ref[pl.ds(off, tk), :], q_t)
        sbuf[c][par][...] = s
        return jnp.max(s, axis=0, keepdims=True)

    def absorb(task, m_blk, m_old):
        c, j, par, masked = task
        s = sbuf[c][par][...]
        if masked:
            key = lax.broadcasted_iota(jnp.int32, s.shape, 0)
            qry = lax.broadcasted_iota(jnp.int32, s.shape, 1)
            s = jnp.where(key <= qry, s, jnp.finfo(F32).min)
            m_blk = jnp.max(s, axis=0, keepdims=True)
        m_new = jnp.maximum(m_old, m_blk)
        p = jnp.exp2(s - m_new)
        alpha = jnp.exp2(m_old - m_new)
        accs[c][...] = alpha * accs[c][...] + _dot(v_ref[j], p.astype(BF16))
        return m_new

    def run(tasks, lookahead, ms, pending):
        ms = list(ms)
        for i, task in enumerate(tasks):
            nxt = tasks[i + 1] if i + 1 < len(tasks) else lookahead
            staged = scores(nxt) if nxt is not None else None
            ms[task[0]] = absorb(task, pending, ms[task[0]])
            pending = staged
        return ms, pending

    chains = range(n_chains)
    n_full = n_chains * qi

    def body(jj, carry):
        j = ATTN_UNROLL * jj
        tasks = [(c, j + b, b % 2, False) for b in range(ATTN_UNROLL) for c in chains]
        ms, pending = run(tasks, (0, j + ATTN_UNROLL, 0, False), carry[:-1], carry[-1])
        return (*ms, pending)

    m0 = jnp.full((1, tk), -1e30, F32)
    carry = lax.fori_loop(0, n_full // ATTN_UNROLL, body,
                          (*[m0] * n_chains, scores((0, 0, 0, False))))
    tail = [(c, n_full + kb, kb % 2, c == kb) for kb in chains for c in range(kb, n_chains)]
    run(tail, None, carry[:-1], carry[-1])
    for c, acc in enumerate(accs):
        out = acc[0:V_HEAD_DIM, :] / acc[V_HEAD_DIM:V_HEAD_DIM + 1, :]
        o_ref[c * tk:(c + 1) * tk, :] = out.T.astype(o_ref.dtype)


def _attention(q_t, k_cat, v_t):
    nh, s, _ = k_cat.shape
    tk = v_t.shape[-1]
    n_chains = min(ATTN_CHAINS, s // tk)
    assert n_chains % ATTN_UNROLL == 0 and ATTN_UNROLL % 2 == 0
    tq = n_chains * tk
    return pl.pallas_call(
        functools.partial(_attn_kernel, tk=tk, n_chains=n_chains),
        grid=(nh, s // tq),
        in_specs=[pl.BlockSpec((None, QK_PAD, tq), lambda h, i: (h, 0, i)),
                  pl.BlockSpec((None, s, QK_PAD), lambda h, i: (h, 0, 0)),
                  pl.BlockSpec((None, s // tk, V_ROWS, tk), lambda h, i: (h, 0, 0, 0))],
        out_specs=pl.BlockSpec((tq, V_HEAD_DIM), lambda h, i: (i, h)),
        out_shape=jax.ShapeDtypeStruct((s, nh * V_HEAD_DIM), BF16),
        scratch_shapes=([pltpu.VMEM((V_ROWS, tk), F32)] * n_chains
                        + [pltpu.VMEM((tk, tk), F32)] * (2 * n_chains)),
        compiler_params=_params("parallel", "arbitrary"),
        name="mla_attention",
    )(q_t, k_cat, v_t)


FFN_CHUNKS = ((0, 1024), (1024, 2048), (2048, D_FF))


def _mix_ffn_kernel(*refs, gated, final):
    refs = list(refs)
    o_ref = refs.pop()
    fg_ref = refs.pop() if final else None
    x_ref, xp_ref, z_ref, zp_ref = refs[:4]
    gt_ref, gtp_ref = refs[4:6] if gated else (None, None)
    wo_ref, g_ref, wi_ref, cw_ref, cb_ref, wout_ref = refs[-6:]
    i = pl.program_id(0)

    def mixed(xr, zr, gr):
        z = zr[...]
        if gated:
            z = z * gr[...]
        return xr[...] + _dot(z.astype(BF16), wo_ref[...])

    x1 = mixed(x_ref, z_ref, gt_ref)
    x1p = mixed(xp_ref, zp_ref, gtp_ref)
    keep = (i > 0).astype(F32)
    hn = jnp.concatenate([(_rms(x1p, g_ref[...]) * keep).astype(BF16),
                          _rms(x1, g_ref[...]).astype(BF16)], axis=0)
    def gate_up(lo, hi):
        return _dot(hn, wi_ref[:, lo:hi]), _dot(hn[HALO:], wi_ref[:, D_FF + lo:D_FF + hi])

    acc = x1
    nxt = gate_up(*FFN_CHUNKS[0])
    for n, (lo, hi) in enumerate(FFN_CHUNKS):
        gate, up = nxt
        if n + 1 < len(FFN_CHUNKS):
            nxt = gate_up(*FFN_CHUNKS[n + 1])
        conv = (gate[HALO:] * cw_ref[2:3, lo:hi]
                + pltpu.roll(gate, 1, 0)[HALO:] * cw_ref[1:2, lo:hi]
                + pltpu.roll(gate, 2, 0)[HALO:] * cw_ref[0:1, lo:hi] + cb_ref[:, lo:hi])
        act = (conv * jax.nn.sigmoid(conv) * up).astype(BF16)
        acc = acc + _dot(act, wout_ref[lo:hi, :])
    o_ref[...] = _rms(acc, fg_ref[...]) if final else acc


def _mix_ffn(x, z, gate, w_o, g, layer, w_in_all, conv_w, conv_b, w_out_all, final_gain=None):
    s, d = x.shape
    tm = _row_tile(s, 512)
    row = pl.BlockSpec((tm, d), lambda i: (i, 0))
    prev = pl.BlockSpec((HALO, d), lambda i: (jnp.maximum(i * (tm // HALO) - 1, 0), 0))

    def const(a, stacked=False):
        if stacked:
            return pl.BlockSpec((None,) + a.shape[1:], lambda i: (layer,) + (0,) * (a.ndim - 1),
                                pipeline_mode=pl.Buffered(1))
        return pl.BlockSpec(a.shape, lambda i: (0,) * a.ndim, pipeline_mode=pl.Buffered(1))

    gated = gate is not None
    final = final_gain is not None
    consts = [w_o, g, w_in_all, conv_w, conv_b, w_out_all] + ([final_gain] if final else [])
    args = [x, x, z, z] + ([gate, gate] if gated else []) + consts
    specs = ([row, prev, row, prev] + ([row, prev] if gated else [])
             + [const(a, stacked=a is w_in_all or a is w_out_all) for a in consts])
    return pl.pallas_call(
        functools.partial(_mix_ffn_kernel, gated=gated, final=final),
        grid=(s // tm,),
        in_specs=specs,
        out_specs=row,
        out_shape=jax.ShapeDtypeStruct((s, d), F32),
        compiler_params=_params("parallel"),
        name="mix_ffn",
    )(*args)


def _rwkv_pre_kernel(*refs, has_vres):
    (x_ref, xp_ref, g_ref, mix_ref, wrkv_ref, w0_ref, w1_ref, w2_ref, a0_ref, a1_ref, a2_ref,
     g1_ref, g2_ref) = refs[:13]
    if has_vres:
        vf_ref, v0_ref, v1_ref, v2_ref = refs[13:17]
    r_o, k_o, v_o, a_o, cl_o, g_o = refs[-6:]
    i = pl.program_id(0)
    tm = x_ref.shape[0]
    h = _rms(x_ref[...], g_ref[...])
    hp = _rms(xp_ref[...], g_ref[...])[HALO - 1:HALO] * (i > 0).astype(F32)
    row = lax.broadcasted_iota(jnp.int32, h.shape, 0)
    shifted = jnp.where(row == 0, jnp.broadcast_to(hp, h.shape), pltpu.roll(h, 1, 0))
    xx = shifted - h
    mix = mix_ref[...]
    xm = lambda n: (h + xx * mix[n:n + 1]).astype(BF16)
    xv = xm(2)
    r = _dot(xm(0), wrkv_ref[0])
    k = _dot(xm(1), wrkv_ref[1])
    v = _dot(xv, wrkv_ref[2])
    lora_w = _dot(jnp.tanh(_dot(xm(3), w1_ref[...])).astype(BF16), w2_ref[...])
    lw = -math.exp(-0.5) * jax.nn.sigmoid(w0_ref[...] + lora_w)
    a_o[...] = jax.nn.sigmoid(a0_ref[...] + _dot(_dot(xm(4), a1_ref[...]).astype(BF16), a2_ref[...]))
    g_o[...] = _dot(jax.nn.sigmoid(_dot(xm(5), g1_ref[...])).astype(BF16), g2_ref[...])
    if has_vres:
        gate_v = jax.nn.sigmoid(v0_ref[...] + _dot(_dot(xv, v1_ref[...]).astype(BF16), v2_ref[...]))
        v = v + (vf_ref[...] - v) * gate_v
    r_o[...] = r
    k_o[...] = k
    v_o[...] = v
    C = WKV_CHUNK
    rr = lax.broadcasted_iota(jnp.int32, (tm, tm), 0)
    cc = lax.broadcasted_iota(jnp.int32, (tm, tm), 1)
    tri = (((rr // C) == (cc // C)) & (cc <= rr)).astype(BF16)
    hi = lw.astype(BF16)
    rem = lw - hi.astype(F32)
    mid = rem.astype(BF16)
    lo = (rem - mid.astype(F32)).astype(BF16)
    cl_o[...] = _dot(tri, hi) + _dot(tri, mid) + _dot(tri, lo)


def _rwkv_pre(x, g, p, v_first):
    s, d = x.shape
    tm = _row_tile(s, WKV_TILE)
    row = pl.BlockSpec((tm, d), lambda i: (i, 0))
    prev = pl.BlockSpec((HALO, d), lambda i: (jnp.maximum(i * (tm // HALO) - 1, 0), 0))
    names = ["mix", "w_rkv", "w0", "w1", "w2", "a0", "a1", "a2", "g1", "g2"]
    args = [x, x, g] + [p[n] for n in names]
    specs = [row, prev, _full(g.shape)] + [_full(p[n].shape) for n in names]
    has_vres = v_first is not None
    if has_vres:
        args += [v_first, p["v0"], p["v1"], p["v2"]]
        specs += [row] + [_full(p[n].shape) for n in ("v0", "v1", "v2")]
    n_out = 6
    out_specs = [row] * n_out
    out_shape = [jax.ShapeDtypeStruct((s, d), F32)] * n_out
    return pl.pallas_call(
        functools.partial(_rwkv_pre_kernel, has_vres=has_vres),
        grid=(s // tm,),
        in_specs=specs,
        out_specs=out_specs,
        out_shape=out_shape,
        compiler_params=_params("parallel"),
        name="rwkv_pre",
    )(*args)


def _wkv_kernel(r_ref, k_ref, v_ref, a_ref, cl_ref, kk_ref, ka_ref, rk_ref,
                lnw_ref, lnb_ref, o_ref, s_ref, *, group):
    C = WKV_CHUNK
    C2 = 2 * C
    n_pairs = o_ref.shape[1] // LANES
    n_groups = o_ref.shape[0] // (C * group)

    @pl.when(pl.program_id(0) == 0)
    def _():
        s_ref[...] = jnp.zeros(s_ref.shape, F32)

    lane = lax.broadcasted_iota(jnp.int32, (1, LANES), 1)
    head0 = lane < RWKV_HEAD
    m0 = head0.astype(F32)
    m1 = 1.0 - m0
    t2 = lax.broadcasted_iota(jnp.int32, (C2, C2), 0)
    s2 = lax.broadcasted_iota(jnp.int32, (C2, C2), 1)
    same_head = (t2 // C) == (s2 // C)
    strict = (same_head & (s2 < t2)).astype(F32)
    incl = (same_head & (s2 <= t2)).astype(F32)
    eye = (t2 == s2).astype(F32)

    def level_mask(b):
        return (((t2 // (2 * b)) == (s2 // (2 * b))) & ((t2 % (2 * b)) >= b)
                & ((s2 % (2 * b)) < b)).astype(F32)

    levels = []
    b_sz = 1
    while b_sz < C:
        levels.append(level_mask(b_sz))
        b_sz *= 2

    def stack(x):
        zero = jnp.zeros_like(x)
        return jnp.concatenate([jnp.where(head0, x, zero), jnp.where(head0, zero, x)], axis=0)

    def twice(x):
        return jnp.concatenate([x, x], axis=0)

    def cast(x):
        return x.astype(BF16)

    first_row = lax.broadcasted_iota(jnp.int32, (C, LANES), 0) == 0

    def head_sum(x):
        return (jnp.sum(x * m0, axis=-1, keepdims=True) * m0
                + jnp.sum(x * m1, axis=-1, keepdims=True) * m1)

    def group_body(g, carry):
        probs = [(ci, p) for ci in range(group) for p in range(n_pairs)]
        offs = [pl.multiple_of((g * group + ci) * C, C) for ci in range(group)]

        def load(ref, ci, p):
            return ref[pl.ds(offs[ci], C), p * LANES:(p + 1) * LANES]

        cl = [load(cl_ref, ci, p) for ci, p in probs]
        cl_end = [c[C - 1:C] for c in cl]
        cl_ex = [jnp.where(first_row, 0.0, pltpu.roll(c, 1, 0)) for c in cl]
        e_neg = [jnp.exp(-c) for c in cl]
        e_rem = [jnp.exp(ce - c) for c, ce in zip(cl, cl_end)]
        w_end = [jnp.exp(ce) for ce in cl_end]
        lanes = [slice(p * LANES, (p + 1) * LANES) for _, p in probs]
        r = [load(r_ref, ci, p) for ci, p in probs]
        k = [load(k_ref, ci, p) for ci, p in probs]
        v = [load(v_ref, ci, p) for ci, p in probs]
        a = [load(a_ref, ci, p) for ci, p in probs]
        kk = [x * kk_ref[:, ln] for x, ln in zip(k, lanes)]
        kk = [x * lax.rsqrt(jnp.maximum(head_sum(x * x), 1e-24)) for x in kk]
        kn = [x * (1.0 + (aa - 1.0) * ka_ref[:, ln]) for x, aa, ln in zip(k, a, lanes)]
        bonus = [head_sum(rr * x * rk_ref[:, ln]) * vv for rr, x, vv, ln in zip(r, kn, v, lanes)]
        bn = [x * aa for x, aa in zip(kk, a)]
        a_s = [stack(cast(-x * jnp.exp(ce))) for x, ce in zip(kk, cl_ex)]
        r_s = [stack(cast(rr * jnp.exp(c))) for rr, c in zip(r, cl)]
        v_s = [stack(cast(vv)) for vv in v]
        hat = [jnp.concatenate([stack(cast(b * e)), stack(cast(k * e))], axis=0)
               for b, k, e in zip(bn, kn, e_rem)]
        til = [jnp.concatenate([twice(cast(b * e)), twice(cast(k * e))], axis=0)
               for b, k, e in zip(bn, kn, e_neg)]
        lm = [_dot_nt(jnp.concatenate([a, r], axis=0), t) for a, r, t in zip(a_s, r_s, til)]
        l_ab = [x[:C2, :C2] * strict for x in lm]
        l_ak = [cast(x[:C2, C2:] * strict) for x in lm]
        m_cat = [cast(jnp.concatenate([x[C2:, :C2] * incl, x[C2:, C2:] * incl], axis=1))
                 for x in lm]
        t_inv = [eye + l * levels[0] for l in l_ab]
        for lvl in levels[1:]:
            t_b = [cast(t) for t in t_inv]
            z = [_dot(t, cast(l * lvl)) for t, l in zip(t_b, l_ab)]
            t_inv = [t + _dot(cast(zz), tb) for t, zz, tb in zip(t_inv, z, t_b)]
        t_b = [cast(t) for t in t_inv]
        p_loc = [_dot(l, v) for l, v in zip(l_ak, v_s)]
        for ci in range(group):
            rows = pl.ds(offs[ci], C)
            sel = [ci * n_pairs + p for p in range(n_pairs)]
            st = [s_ref[p] for p in range(n_pairs)]
            st_b = [cast(s) for s in st]
            x = [_dot_nt(a_s[i], sb) + p_loc[i] for i, sb in zip(sel, st_b)]
            u = [cast(_dot(t_b[i], cast(xx))) for i, xx in zip(sel, x)]
            uv = [jnp.concatenate([uu, v_s[i]], axis=0) for i, uu in zip(sel, u)]
            y = [_dot_nt(r_s[i], sb) + _dot(m_cat[i], w) for i, sb, w in zip(sel, st_b, uv)]
            for p in range(n_pairs):
                cols = slice(p * LANES, (p + 1) * LANES)
                s_ref[p] = st[p] * w_end[sel[p]] + _dot_tn(uv[p], hat[sel[p]])
                yp = y[p][:C] + y[p][C:]
                yc = yp - head_sum(yp) * (1.0 / RWKV_HEAD)
                var = head_sum(yc * yc) * (1.0 / RWKV_HEAD)
                yn = yc * lax.rsqrt(var + GN_EPS)
                o_ref[rows, cols] = yn * lnw_ref[:, cols] + lnb_ref[:, cols] + bonus[sel[p]]
        return carry

    lax.fori_loop(0, n_groups, group_body, 0)


def _wkv(r, k, v, a, cl, k_k, k_a, r_k, ln_w, ln_b):
    s, d = r.shape
    tt = _row_tile(s, WKV_STEP)
    group = 2
    row = pl.BlockSpec((tt, d), lambda t: (t, 0))
    return pl.pallas_call(
        functools.partial(_wkv_kernel, group=group),
        grid=(s // tt,),
        in_specs=[row] * 5 + [_full(ln_w.shape)] * 5,
        out_specs=row,
        out_shape=jax.ShapeDtypeStruct((s, d), F32),
        scratch_shapes=[pltpu.VMEM((d // LANES, LANES, LANES), F32)],
        compiler_params=_params("arbitrary"),
        name="wkv7_scan",
    )(r, k, v, a, cl, k_k, k_a, r_k, ln_w, ln_b)


def _rot_cols(w):
    half = QK_ROPE_DIM // 2
    return jnp.concatenate([-w[..., half:], w[..., :half]], axis=-1)


def _mla_weights(w_in, w_uq, w_ukv):
    d = w_in.shape[0]
    lat = Q_LORA_RANK + KV_LORA_RANK
    w_kr = w_in[:, lat:]
    w_in_e = jnp.concatenate([w_in[:, :lat], w_kr, _rot_cols(w_kr)], axis=1).astype(BF16)
    uq = w_uq.reshape(Q_LORA_RANK, MLA_HEADS, QK_DIM)
    uq_rope = uq[:, :, QK_NOPE_DIM:]
    w_uq_e = jnp.concatenate(
        [uq[:, :, :QK_NOPE_DIM].reshape(Q_LORA_RANK, -1),
         uq_rope.reshape(Q_LORA_RANK, -1),
         _rot_cols(uq_rope).reshape(Q_LORA_RANK, -1)], axis=1).astype(BF16)
    ukv = w_ukv.reshape(KV_LORA_RANK, MLA_HEADS, QK_NOPE_DIM + V_HEAD_DIM)
    w_ukv_p = jnp.concatenate(
        [ukv[:, :, :QK_NOPE_DIM].reshape(KV_LORA_RANK, -1),
         ukv[:, :, QK_NOPE_DIM:].reshape(KV_LORA_RANK, -1)], axis=1).astype(BF16)
    del d
    return w_in_e, w_uq_e, w_ukv_p


def _mla_layer(x, g, cos_t, sin_t, w_in, q_norm, w_uq, kv_norm, w_ukv):
    w_in_e, w_uq_e, w_ukv_p = _mla_weights(w_in, w_uq, w_ukv)
    q_cat, k_cat, v = _mla_pre(x, g[None], w_in_e, q_norm[None], w_uq_e, kv_norm[None],
                               w_ukv_p, cos_t, sin_t)
    return _attention(q_cat, k_cat, v)


def _rwkv_layer(x, g, v_first, p):
    row = lambda t: t.reshape(1, -1)
    pp = {
        "mix": p["mix"], "w_rkv": p["w_rkv"].astype(BF16),
        "w0": row(p["w0"]), "w1": p["w1"].astype(BF16), "w2": p["w2"].astype(BF16),
        "a0": row(p["a0"]), "a1": p["a1"].astype(BF16), "a2": p["a2"].astype(BF16),
        "g1": p["g1"].astype(BF16), "g2": p["g2"].astype(BF16),
    }
    if v_first is not None:
        pp.update(v0=row(p["v0"]), v1=p["v1"].astype(BF16), v2=p["v2"].astype(BF16))
    outs = _rwkv_pre(x, g[None], pp, v_first)
    scan_in, gate = outs[:5], outs[5]
    z = _wkv(*scan_in, *(row(p[n]) for n in ("k_k", "k_a", "r_k", "ln_w", "ln_b")))
    return z, gate, (outs[2] if v_first is None else v_first)


def _mix_ffn_layer(x, z, gate, w_o, g, layer, w_in_all, conv_w, conv_b, w_out_all, final_gain=None):
    return _mix_ffn(x, z, gate, w_o.astype(BF16), g[None], layer, w_in_all, conv_w, conv_b[None],
                    w_out_all, None if final_gain is None else final_gain[None])


def kernel(x, positions, norm_mix, norm_ffn, norm_final, mla_w_in, mla_q_norm, mla_w_uq, mla_kv_norm, mla_w_ukv, mla_w_o, rwkv_mix, rwkv_w_rkv, rwkv_w0, rwkv_w1, rwkv_w2, rwkv_a0, rwkv_a1, rwkv_a2, rwkv_v0, rwkv_v1, rwkv_v2, rwkv_g1, rwkv_g2, rwkv_k_k, rwkv_k_a, rwkv_r_k, rwkv_ln_w, rwkv_ln_b, rwkv_w_o, ffn_w_in, ffn_conv_w, ffn_conv_b, ffn_w_out):
    batch, seq, d = x.shape
    assert batch == 1
    depth = norm_mix.shape[0]
    half = QK_ROPE_DIM // 2
    inv_freq = ROPE_THETA ** (-jnp.arange(half, dtype=F32) / half)
    freq_row = jnp.tile(inv_freq, LANES // half)[None]
    cos_t, sin_t = _rope_tables(positions.astype(F32).reshape(seq, 1), freq_row)
    xs = x.reshape(seq, d)
    ffn_w_in_b = ffn_w_in.astype(BF16)
    ffn_w_out_b = ffn_w_out.astype(BF16)
    v_first = None
    for i in range(depth):
        j = i // 2
        if i % 2 == 0:
            z = _mla_layer(xs, norm_mix[i], cos_t, sin_t, mla_w_in[j], mla_q_norm[j], mla_w_uq[j],
                           mla_kv_norm[j], mla_w_ukv[j])
            gate, w_o = None, mla_w_o[j]
        else:
            p = {"mix": rwkv_mix[j], "w_rkv": rwkv_w_rkv[j], "w0": rwkv_w0[j], "w1": rwkv_w1[j],
                 "w2": rwkv_w2[j], "a0": rwkv_a0[j], "a1": rwkv_a1[j], "a2": rwkv_a2[j],
                 "g1": rwkv_g1[j], "g2": rwkv_g2[j], "k_k": rwkv_k_k[j], "k_a": rwkv_k_a[j],
                 "r_k": rwkv_r_k[j], "ln_w": rwkv_ln_w[j], "ln_b": rwkv_ln_b[j]}
            if j > 0:
                p.update(v0=rwkv_v0[j - 1], v1=rwkv_v1[j - 1], v2=rwkv_v2[j - 1])
            z, gate, v_first = _rwkv_layer(xs, norm_mix[i], v_first, p)
            w_o = rwkv_w_o[j]
        xs = _mix_ffn_layer(xs, z, gate, w_o, norm_ffn[i], i, ffn_w_in_b, ffn_conv_w[i], ffn_conv_b[i],
                            ffn_w_out_b, norm_final if i == depth - 1 else None)
    return xs.reshape(batch, seq, d)
```
